```python
import math
import jax, jax.numpy as jnp
from jax import lax
import numpy as np

D_MODEL = 2048
BATCH = 4
SEQ = 4096
DEPTH = 4

N_HY = (DEPTH + 1) // 2
N_SSD = DEPTH // 2

LRU_WIDTH = D_MODEL // 2
LRU_HEADS = 8
LRU_BLOCK = LRU_WIDTH // LRU_HEADS
LRU_C = 8.0
CONV_K = 4

FOX_HEADS = 8
FOX_HEAD_DIM = (D_MODEL // 2) // FOX_HEADS
FOX_WIDTH = FOX_HEADS * FOX_HEAD_DIM
Q_BLOCK = 128

HY_IN = 2 * LRU_WIDTH + 3 * FOX_WIDTH + FOX_HEADS
MIX_WIDTH = LRU_WIDTH + FOX_WIDTH

SSD_INNER = 2 * D_MODEL
SSD_HEAD_DIM = 64
SSD_HEADS = SSD_INNER // SSD_HEAD_DIM
SSD_GROUPS = 8
SSD_STATE = 128
SSD_CHUNK = 128
SSD_CONV_DIM = SSD_INNER + 2 * SSD_GROUPS * SSD_STATE
SSD_IN = SSD_INNER + SSD_CONV_DIM + SSD_HEADS

MOE_GROUPS = 4
MOE_PER_GROUP = 4
MOE_TOPK = 2
MOE_FF = D_MODEL // 2

ALPHA = (2 * DEPTH) ** 0.25
BETA = (8 * DEPTH) ** -0.25
LN_EPS = 1e-5
RMS_EPS = 1e-5

kernel_name = "hybrid_rglru_fox_ssd_hmoe_deepnorm"


def layer_norm(x, g, b):
    xf = x.astype(jnp.float32)
    mu = jnp.mean(xf, axis=-1, keepdims=True)
    var = jnp.mean(jnp.square(xf - mu), axis=-1, keepdims=True)
    return ((xf - mu) * lax.rsqrt(var + LN_EPS) * g + b).astype(x.dtype)


def causal_dwconv(x, w, b):
    k_width = w.shape[0]
    s = x.shape[1]
    xp = jnp.pad(x, ((0, 0), (k_width - 1, 0), (0, 0)))
    out = b
    for k in range(k_width):
        out = out + xp[:, k:k + s] * w[k]
    return out


def rg_lru(xc, w_r, b_r, w_i, b_i, lam):
    bn, s, w = xc.shape
    xh = xc.reshape(bn, s, LRU_HEADS, LRU_BLOCK)
    r = jax.nn.sigmoid((jnp.einsum('bshi,hij->bshj', xh, w_r) + b_r).astype(jnp.float32)).reshape(bn, s, w)
    i = jax.nn.sigmoid((jnp.einsum('bshi,hij->bshj', xh, w_i) + b_i).astype(jnp.float32)).reshape(bn, s, w)
    log_a = -LRU_C * r * jax.nn.softplus(-lam.astype(jnp.float32))
    a = jnp.exp(log_a)
    u = jnp.sqrt(-jnp.expm1(2.0 * log_a)) * (i * xc.astype(jnp.float32))

    def combine(left, right):
        a1, b1 = left
        a2, b2 = right
        return a1 * a2, a2 * b1 + b2

    _, h = lax.associative_scan(combine, (a, u), axis=1)
    return h


def fox_attention(q, k, v, logf):
    bn, s, h, dh = q.shape
    nb = s // Q_BLOCK
    F = jnp.cumsum(logf, axis=1).transpose(0, 2, 1)
    qb = q.reshape(bn, nb, Q_BLOCK, h, dh).transpose(1, 0, 2, 3, 4)
    fb = F.reshape(bn, h, nb, Q_BLOCK).transpose(2, 0, 1, 3)
    kpos = jnp.arange(s)
    scale = dh ** -0.5

    def block(args):
        qi, fi, bi = args
        sc = jnp.einsum('bqhd,bkhd->bhqk', qi, k, preferred_element_type=jnp.float32) * scale
        sc = sc + fi[..., :, None] - F[:, :, None, :]
        qpos = bi * Q_BLOCK + jnp.arange(Q_BLOCK)
        sc = jnp.where(kpos[None, :] <= qpos[:, None], sc, -jnp.inf)
        p = jax.nn.softmax(sc, axis=-1)
        return jnp.einsum('bhqk,bkhd->bqhd', p.astype(v.dtype), v)

    out = lax.map(block, (qb, fb, jnp.arange(nb)))
    return out.transpose(1, 0, 2, 3, 4).reshape(bn, s, h, dh)


def hybrid_mixer(x, w_in, conv_w, conv_b, w_r, b_r, w_i, b_i, lam, b_f, w_out):
    bn, s, _ = x.shape
    proj = x @ w_in
    o1 = LRU_WIDTH
    o2 = 2 * LRU_WIDTH
    o3 = o2 + FOX_WIDTH
    o4 = o3 + FOX_WIDTH
    o5 = o4 + FOX_WIDTH
    xl, gate, q, k, v, fl = jnp.split(proj, [o1, o2, o3, o4, o5], axis=-1)
    xc = causal_dwconv(xl, conv_w, conv_b)
    h = rg_lru(xc, w_r, b_r, w_i, b_i, lam)
    y_lru = (h * jax.nn.gelu(gate.astype(jnp.float32))).astype(x.dtype)
    logf = jax.nn.log_sigmoid(fl.astype(jnp.float32) + b_f)
    shp = (bn, s, FOX_HEADS, FOX_HEAD_DIM)
    y_att = fox_attention(q.reshape(shp), k.reshape(shp), v.reshape(shp), logf).reshape(bn, s, FOX_WIDTH)
    return jnp.concatenate([y_lru, y_att.astype(x.dtype)], axis=-1) @ w_out


def ssd_scan(xs, dt, A, bm, cm):
    bn, s, h, p = xs.shape
    g = bm.shape[2]
    n = bm.shape[3]
    e = h // g
    L = SSD_CHUNK
    nc = s // L
    X = (xs * dt[..., None]).reshape(bn, nc, L, g, e, p)
    acs = jnp.cumsum((dt * A).reshape(bn, nc, L, g, e), axis=2)
    bc = bm.reshape(bn, nc, L, g, n)
    cc = cm.reshape(bn, nc, L, g, n)
    seg = acs[:, :, :, None] - acs[:, :, None, :]
    causal = jnp.tril(jnp.ones((L, L), dtype=bool))
    decay = jnp.exp(jnp.where(causal[:, :, None, None], seg, -jnp.inf))
    cb = jnp.einsum('bclgn,bcsgn->bclsg', cc, bc)
    y_diag = jnp.einsum('bclsge,bcsgep->bclgep', cb[..., None] * decay, X)
    xd = X * jnp.exp(acs[:, :, -1:] - acs)[..., None]
    states = jnp.einsum('bclgn,bclgep->bcgepn', bc, xd)
    chunk_decay = jnp.exp(acs[:, :, -1])

    def step(carry, inp):
        st, dec = inp
        return carry * dec[..., None, None] + st, carry

    init = jnp.zeros((bn, g, e, p, n), dtype=jnp.float32)
    _, prev = lax.scan(step, init, (states.transpose(1, 0, 2, 3, 4, 5), chunk_decay.transpose(1, 0, 2, 3)))
    prev = prev.transpose(1, 0, 2, 3, 4, 5)
    y_off = jnp.einsum('bclgn,bcgepn->bclgep', cc, prev) * jnp.exp(acs)[..., None]
    return (y_diag + y_off).reshape(bn, s, h, p)


def ssd_mixer(x, w_in, conv_w, conv_b, dt_bias, a_log, d_skip, norm_g, w_out):
    bn, s, _ = x.shape
    proj = x @ w_in
    z, xbc, dt = jnp.split(proj, [SSD_INNER, SSD_INNER + SSD_CONV_DIM], axis=-1)
    xbc = jax.nn.silu(causal_dwconv(xbc, conv_w, conv_b).astype(jnp.float32))
    xs, bm, cm = jnp.split(xbc, [SSD_INNER, SSD_INNER + SSD_GROUPS * SSD_STATE], axis=-1)
    xs = xs.reshape(bn, s, SSD_HEADS, SSD_HEAD_DIM)
    bm = bm.reshape(bn, s, SSD_GROUPS, SSD_STATE)
    cm = cm.reshape(bn, s, SSD_GROUPS, SSD_STATE)
    dt = jax.nn.softplus(dt.astype(jnp.float32) + dt_bias)
    A = -jnp.exp(a_log.astype(jnp.float32))
    y = ssd_scan(xs, dt, A, bm, cm) + d_skip.astype(jnp.float32)[:, None] * xs
    y = y.reshape(bn, s, SSD_INNER) * jax.nn.silu(z.astype(jnp.float32))
    yg = y.reshape(bn, s, SSD_GROUPS, SSD_INNER // SSD_GROUPS)
    yg = yg * lax.rsqrt(jnp.mean(jnp.square(yg), axis=-1, keepdims=True) + RMS_EPS)
    y = yg.reshape(bn, s, SSD_INNER) * norm_g
    return y.astype(x.dtype) @ w_out


def hier_moe(x, w_r1, b_r1, w_r2, b_r2, w_gate, w_up, w_down):
    bn, s, d = x.shape
    t = x.reshape(-1, d)
    p1 = jax.nn.softmax((t @ w_r1).astype(jnp.float32) + b_r1, axis=-1)
    pg, g_idx = lax.top_k(p1, 1)
    l2 = jnp.einsum('td,gde->tge', t, w_r2).astype(jnp.float32) + b_r2
    l2 = jnp.take_along_axis(l2, g_idx[:, :, None], axis=1)[:, 0]
    v2, e_idx = lax.top_k(l2, MOE_TOPK)
    p2 = jax.nn.softmax(v2, axis=-1)
    w_e = jnp.einsum('tk,tke->te', p2, jax.nn.one_hot(e_idx, MOE_PER_GROUP, dtype=jnp.float32))
    comb = (pg * jax.nn.one_hot(g_idx[:, 0], MOE_GROUPS, dtype=jnp.float32))[:, :, None] * w_e[:, None, :]
    comb = comb.astype(x.dtype)
    out = jnp.zeros_like(t)
    for gi in range(MOE_GROUPS):
        for ei in range(MOE_PER_GROUP):
            hid = jax.nn.silu(t @ w_gate[gi, ei]) * (t @ w_up[gi, ei])
            out = out + (hid * comb[:, gi, ei, None]) @ w_down[gi, ei]
    return out.reshape(bn, s, d)


def setup_inputs(seed: int = 0) -> dict:
    key = jax.random.key(seed)
    ks = jax.random.split(key, 32)

    def nrm(k, shape, scale):
        return jax.random.normal(k, shape, jnp.float32) * scale

    x = nrm(ks[0], (BATCH, SEQ, D_MODEL), 1.0)
    ln_g = 1.0 + nrm(ks[1], (DEPTH, 2, D_MODEL), 0.02)
    ln_b = nrm(ks[2], (DEPTH, 2, D_MODEL), 0.02)

    hy_w_in = nrm(ks[3], (N_HY, D_MODEL, HY_IN), D_MODEL ** -0.5)
    hy_conv_w = nrm(ks[4], (N_HY, CONV_K, LRU_WIDTH), CONV_K ** -0.5)
    hy_conv_b = nrm(ks[5], (N_HY, LRU_WIDTH), 0.02)
    hy_w_r = nrm(ks[6], (N_HY, LRU_HEADS, LRU_BLOCK, LRU_BLOCK), LRU_BLOCK ** -0.5)
    hy_b_r = nrm(ks[7], (N_HY, LRU_HEADS, LRU_BLOCK), 0.02)
    hy_w_i = nrm(ks[8], (N_HY, LRU_HEADS, LRU_BLOCK, LRU_BLOCK), LRU_BLOCK ** -0.5)
    hy_b_i = nrm(ks[9], (N_HY, LRU_HEADS, LRU_BLOCK), 0.02)
    u = jax.random.uniform(ks[10], (N_HY, LRU_WIDTH), jnp.float32, 0.9, 0.999)
    sg = u ** (1.0 / LRU_C)
    hy_lambda = jnp.log(sg) - jnp.log1p(-sg)
    hy_b_f = jnp.linspace(2.0, 6.0, FOX_HEADS, dtype=jnp.float32)[None, :] + nrm(ks[11], (N_HY, FOX_HEADS), 0.1)
    hy_w_out = nrm(ks[12], (N_HY, MIX_WIDTH, D_MODEL), BETA * MIX_WIDTH ** -0.5)

    ssd_w_in = nrm(ks[13], (N_SSD, D_MODEL, SSD_IN), D_MODEL ** -0.5)
    ssd_conv_w = nrm(ks[14], (N_SSD, CONV_K, SSD_CONV_DIM), CONV_K ** -0.5)
    ssd_conv_b = nrm(ks[15], (N_SSD, SSD_CONV_DIM), 0.02)
    ud = jax.random.uniform(ks[16], (N_SSD, SSD_HEADS), jnp.float32)
    dt0 = jnp.exp(ud * (math.log(0.1) - math.log(0.001)) + math.log(0.001))
    ssd_dt_bias = dt0 + jnp.log(-jnp.expm1(-dt0))
    ssd_a_log = jnp.log(jax.random.uniform(ks[17], (N_SSD, SSD_HEADS), jnp.float32, 1.0, 16.0))
    ssd_d = 1.0 + nrm(ks[18], (N_SSD, SSD_HEADS), 0.1)
    ssd_norm_g = 1.0 + nrm(ks[19], (N_SSD, SSD_INNER), 0.02)
    ssd_w_out = nrm(ks[20], (N_SSD, SSD_INNER, D_MODEL), BETA * SSD_INNER ** -0.5)

    moe_w_r1 = nrm(ks[21], (DEPTH, D_MODEL, MOE_GROUPS), D_MODEL ** -0.5)
    moe_b_r1 = nrm(ks[22], (DEPTH, MOE_GROUPS), 0.01)
    moe_w_r2 = nrm(ks[23], (DEPTH, MOE_GROUPS, D_MODEL, MOE_PER_GROUP), D_MODEL ** -0.5)
    moe_b_r2 = nrm(ks[24], (DEPTH, MOE_GROUPS, MOE_PER_GROUP), 0.01)
    moe_w_gate = nrm(ks[25], (DEPTH, MOE_GROUPS, MOE_PER_GROUP, D_MODEL, MOE_FF), D_MODEL ** -0.5)
    moe_w_up = nrm(ks[26], (DEPTH, MOE_GROUPS, MOE_PER_GROUP, D_MODEL, MOE_FF), D_MODEL ** -0.5)
    moe_w_down = nrm(ks[27], (DEPTH, MOE_GROUPS, MOE_PER_GROUP, MOE_FF, D_MODEL), BETA * MOE_FF ** -0.5)

    return {
        "x": x, "ln_g": ln_g, "ln_b": ln_b,
        "hy_w_in": hy_w_in, "hy_conv_w": hy_conv_w, "hy_conv_b": hy_conv_b,
        "hy_w_r": hy_w_r, "hy_b_r": hy_b_r, "hy_w_i": hy_w_i, "hy_b_i": hy_b_i,
        "hy_lambda": hy_lambda, "hy_b_f": hy_b_f, "hy_w_out": hy_w_out,
        "ssd_w_in": ssd_w_in, "ssd_conv_w": ssd_conv_w, "ssd_conv_b": ssd_conv_b,
        "ssd_dt_bias": ssd_dt_bias, "ssd_a_log": ssd_a_log, "ssd_d": ssd_d,
        "ssd_norm_g": ssd_norm_g, "ssd_w_out": ssd_w_out,
        "moe_w_r1": moe_w_r1, "moe_b_r1": moe_b_r1, "moe_w_r2": moe_w_r2, "moe_b_r2": moe_b_r2,
        "moe_w_gate": moe_w_gate, "moe_w_up": moe_w_up, "moe_w_down": moe_w_down,
    }


def reference(x, ln_g, ln_b,
              hy_w_in, hy_conv_w, hy_conv_b, hy_w_r, hy_b_r, hy_w_i, hy_b_i, hy_lambda, hy_b_f, hy_w_out,
              ssd_w_in, ssd_conv_w, ssd_conv_b, ssd_dt_bias, ssd_a_log, ssd_d, ssd_norm_g, ssd_w_out,
              moe_w_r1, moe_b_r1, moe_w_r2, moe_b_r2, moe_w_gate, moe_w_up, moe_w_down):
    for layer in range(DEPTH):
        j = layer // 2
        if layer % 2 == 0:
            m = hybrid_mixer(x, hy_w_in[j], hy_conv_w[j], hy_conv_b[j], hy_w_r[j], hy_b_r[j],
                             hy_w_i[j], hy_b_i[j], hy_lambda[j], hy_b_f[j], hy_w_out[j])
        else:
            m = ssd_mixer(x, ssd_w_in[j], ssd_conv_w[j], ssd_conv_b[j], ssd_dt_bias[j],
                          ssd_a_log[j], ssd_d[j], ssd_norm_g[j], ssd_w_out[j])
        x = layer_norm(ALPHA * x + m, ln_g[layer, 0], ln_b[layer, 0])
        f = hier_moe(x, moe_w_r1[layer], moe_b_r1[layer], moe_w_r2[layer], moe_b_r2[layer],
                     moe_w_gate[layer], moe_w_up[layer], moe_w_down[layer])
        x = layer_norm(ALPHA * x + f, ln_g[layer, 1], ln_b[layer, 1])
    return x
```

```python
import functools

import jax
import jax.numpy as jnp
from jax import lax
from jax.experimental import pallas as pl
from jax.experimental.pallas import tpu as pltpu

F32 = jnp.float32
BF16 = jnp.bfloat16

_VMEM_LIMIT_BYTES = 56 * 1024 * 1024
_LANES = 128
_SUBLANES = 8

LN_EPS = 1e-5
RMS_EPS = 1e-5
LRU_C = 8.0
CONV_K = 4
SSD_CHUNK = 128
SSD_HEAD_DIM = 64
SSD_STATE = 128
MOE_GROUPS = 4
MOE_PER_GROUP = 4
N_EXPERTS = MOE_GROUPS * MOE_PER_GROUP
ROUTER_COLS = 128
ROUTER_L2_ROW0 = 8


def _params(*sem):
    return pltpu.CompilerParams(dimension_semantics=sem, vmem_limit_bytes=_VMEM_LIMIT_BYTES)


def _sigmoid(x):
    return 1.0 / (1.0 + jnp.exp(-x))


def _softplus(x):
    return jnp.maximum(x, 0.0) + jnp.log1p(jnp.exp(-jnp.abs(x)))


def _shift_rows(x, tail, d):
    r = pltpu.roll(x, d, axis=0)
    t = pltpu.roll(tail, d, axis=0)
    row = lax.broadcasted_iota(jnp.int32, tail.shape, 0)
    first = jnp.where(row < d, t, r[:_SUBLANES])
    return jnp.concatenate([first, r[_SUBLANES:]], axis=0)


def _causal_conv(x, tail, w_ref, b_ref, cols):
    acc = b_ref[:, cols] + w_ref[CONV_K - 1:CONV_K, cols] * x
    for d in range(1, CONV_K):
        acc = acc + w_ref[CONV_K - 1 - d:CONV_K - d, cols] * _shift_rows(x, tail, d)
    return acc


def _cumsum_rows(x):
    n = x.shape[0]
    row = lax.broadcasted_iota(jnp.int32, x.shape, 0)
    d = 1
    while d < n:
        x = x + jnp.where(row >= d, pltpu.roll(x, d, axis=0), 0.0)
        d *= 2
    return x


def _cumsum_lanes(x):
    n = x.shape[1]
    lane = lax.broadcasted_iota(jnp.int32, x.shape, 1)
    d = 1
    while d < n:
        x = x + jnp.where(lane >= d, pltpu.roll(x, d, axis=1), 0.0)
        d *= 2
    return x


def _layer_norm(s, g, b):
    mu = jnp.mean(s, axis=-1, keepdims=True)
    d = s - mu
    var = jnp.mean(d * d, axis=-1, keepdims=True)
    return d * lax.rsqrt(var + LN_EPS) * g + b


def _mm_body(x_ref, w_ref, o_ref):
    o_ref[...] = jnp.dot(x_ref[...], w_ref[...], preferred_element_type=F32).astype(o_ref.dtype)


def _matmul(x, w, out_dtype, tm, tn, name):
    m, k = x.shape
    n = w.shape[1]
    tm, tn = min(tm, m), min(tn, n)
    return pl.pallas_call(
        _mm_body,
        grid=(m // tm, n // tn),
        in_specs=[pl.BlockSpec((tm, k), lambda i, j: (i, 0)),
                  pl.BlockSpec((k, tn), lambda i, j: (0, j))],
        out_specs=pl.BlockSpec((tm, tn), lambda i, j: (i, j)),
        out_shape=jax.ShapeDtypeStruct((m, n), out_dtype),
        compiler_params=_params("parallel", "parallel"),
        name=name,
    )(x, w)


def _mm_ln_body(a_ref, w_ref, x_ref, g_ref, b_ref, of_ref, ob_ref, acc_ref, *, alpha):
    k = pl.program_id(1)

    @pl.when(k == 0)
    def _():
        acc_ref[...] = jnp.zeros_like(acc_ref)

    acc_ref[...] += jnp.dot(a_ref[...], w_ref[...], preferred_element_type=F32)

    @pl.when(k == pl.num_programs(1) - 1)
    def _():
        y = _layer_norm(alpha * x_ref[...] + acc_ref[...], g_ref[...], b_ref[...])
        of_ref[...] = y
        ob_ref[...] = y.astype(BF16)


def _matmul_residual_ln(a, w, x, g, b, alpha, tm, tk, name):
    m, kdim = a.shape
    n = w.shape[1]
    tm, tk = min(tm, m), min(tk, kdim)
    return pl.pallas_call(
        functools.partial(_mm_ln_body, alpha=alpha),
        grid=(m // tm, kdim // tk),
        in_specs=[pl.BlockSpec((tm, tk), lambda i, k: (i, k)),
                  pl.BlockSpec((tk, n), lambda i, k: (k, 0)),
                  pl.BlockSpec((tm, n), lambda i, k: (i, 0)),
                  pl.BlockSpec((1, n), lambda i, k: (0, 0)),
                  pl.BlockSpec((1, n), lambda i, k: (0, 0))],
        out_specs=[pl.BlockSpec((tm, n), lambda i, k: (i, 0)),
                   pl.BlockSpec((tm, n), lambda i, k: (i, 0))],
        out_shape=[jax.ShapeDtypeStruct((m, n), F32), jax.ShapeDtypeStruct((m, n), BF16)],
        scratch_shapes=[pltpu.VMEM((tm, n), F32)],
        compiler_params=_params("parallel", "arbitrary"),
        name=name,
    )(a, w, x, g.reshape(1, n), b.reshape(1, n))


def _fox_prep_body(fl_ref, bf_ref, f_ref):
    x = fl_ref[0] + bf_ref[...]
    f_ref[0] = _cumsum_lanes(-_softplus(-x))


def _fox_prep(fl_t, b_f):
    bn, h, s = fl_t.shape
    return pl.pallas_call(
        _fox_prep_body,
        grid=(bn,),
        in_specs=[pl.BlockSpec((1, h, s), lambda b: (b, 0, 0)),
                  pl.BlockSpec((h, 1), lambda b: (0, 0))],
        out_specs=pl.BlockSpec((1, h, s), lambda b: (b, 0, 0)),
        out_shape=jax.ShapeDtypeStruct((bn, h, s), F32),
        compiler_params=_params("parallel"),
        name="fox_prep",
    )(fl_t, b_f.reshape(h, 1))


def _lru_body(xl_ref, gate_ref, cw_ref, cb_ref, wr_ref, br_ref, wi_ref, bi_ref, lam_ref,
              y_ref, tail_ref, h_ref, *, heads, block):
    s = pl.program_id(1)

    @pl.when(s == 0)
    def _():
        tail_ref[...] = jnp.zeros_like(tail_ref)
        h_ref[...] = jnp.zeros_like(h_ref)

    xl = xl_ref[0].astype(F32)
    ts = xl.shape[0]
    xc = _causal_conv(xl, tail_ref[...], cw_ref, cb_ref, slice(None))
    tail_ref[...] = xl[ts - _SUBLANES:, :]

    xcb = xc.astype(BF16)
    r_parts, i_parts = [], []
    for hd in range(heads):
        xh = xcb[:, hd * block:(hd + 1) * block]
        r_parts.append(jnp.dot(xh, wr_ref[hd], preferred_element_type=F32))
        i_parts.append(jnp.dot(xh, wi_ref[hd], preferred_element_type=F32))
    r = _sigmoid(jnp.concatenate(r_parts, axis=1) + br_ref[...])
    ig = _sigmoid(jnp.concatenate(i_parts, axis=1) + bi_ref[...])
    log_a = -LRU_C * r * _softplus(-lam_ref[...])
    a = jnp.exp(log_a)
    u = jnp.sqrt(-jnp.tanh(log_a) * (a * a + 1.0)) * (ig * xc)

    row = lax.broadcasted_iota(jnp.int32, a.shape, 0)
    d = 1
    while d < ts:
        keep = row >= d
        u = u + a * jnp.where(keep, pltpu.roll(u, d, axis=0), 0.0)
        a = a * jnp.where(keep, pltpu.roll(a, d, axis=0), 1.0)
        d *= 2
    h = u + a * h_ref[...]
    h_ref[...] = h[ts - 1:ts, :]
    y_ref[0] = (h * jax.nn.gelu(gate_ref[0].astype(F32))).astype(y_ref.dtype)


def _lru(proj, conv_w, conv_b, w_r, b_r, w_i, b_i, lam, ts):
    bn, s, _ = proj.shape
    heads, block, _ = w_r.shape
    width = heads * block
    ts = min(ts, s)
    vec = lambda v: v.reshape(1, width)
    const2 = lambda b, i: (0, 0)
    const3 = lambda b, i: (0, 0, 0)
    return pl.pallas_call(
        functools.partial(_lru_body, heads=heads, block=block),
        grid=(bn, s // ts),
        in_specs=[pl.BlockSpec((1, ts, width), lambda b, i: (b, i, 0)),
                  pl.BlockSpec((1, ts, width), lambda b, i: (b, i, 1)),
                  pl.BlockSpec((CONV_K, width), const2),
                  pl.BlockSpec((1, width), const2),
                  pl.BlockSpec((heads, block, block), const3),
                  pl.BlockSpec((1, width), const2),
                  pl.BlockSpec((heads, block, block), const3),
                  pl.BlockSpec((1, width), const2),
                  pl.BlockSpec((1, width), const2)],
        out_specs=pl.BlockSpec((1, ts, width), lambda b, i: (b, i, 0)),
        out_shape=jax.ShapeDtypeStruct((bn, s, width), BF16),
        scratch_shapes=[pltpu.VMEM((_SUBLANES, width), F32), pltpu.VMEM((1, width), F32)],
        compiler_params=_params("parallel", "arbitrary"),
        name="rg_lru",
    )(proj, proj, conv_w, vec(conv_b), w_r.astype(BF16), vec(b_r), w_i.astype(BF16), vec(b_i), vec(lam))


def _fox_attn_body(q_ref, k_ref, v_ref, fc_ref, fr_ref, o_ref, *, tq, scale):
    qi = pl.program_id(2)
    q = q_ref[0]
    fq = fc_ref[0, 0]
    dh = q.shape[1]

    def scores(j):
        start = pl.multiple_of(j * tq, tq)
        k = k_ref[0, pl.ds(start, tq), :]
        sc = lax.dot_general(q, k, (((1,), (1,)), ((), ())), preferred_element_type=F32)
        return sc * scale + fq - fr_ref[0, 0, :, pl.ds(start, tq)], start

    def update(carry, sc, start):
        m, l, acc = carry
        m_new = jnp.maximum(m, jnp.max(sc, axis=1, keepdims=True))
        corr = jnp.exp(m - m_new)
        p = jnp.exp(sc - m_new)
        l = corr * l + jnp.sum(p, axis=1, keepdims=True)
        v = v_ref[0, pl.ds(start, tq), :]
        acc = corr * acc + jnp.dot(p.astype(BF16), v, preferred_element_type=F32)
        return m_new, l, acc

    def body(j, carry):
        sc, start = scores(j)
        return update(carry, sc, start)

    init = (jnp.full((tq, 1), -1e30, F32), jnp.zeros((tq, 1), F32), jnp.zeros((tq, dh), F32))
    carry = lax.fori_loop(0, qi, body, init)
    sc, start = scores(qi)
    rq = lax.broadcasted_iota(jnp.int32, sc.shape, 0)
    ck = lax.broadcasted_iota(jnp.int32, sc.shape, 1)
    _, l, acc = update(carry, jnp.where(ck <= rq, sc, -1e30), start)
    o_ref[0] = (acc / l).astype(o_ref.dtype)


def _fox_attention(proj, f, heads, dh, q_col0, tq):
    bn, s, _ = proj.shape
    tq = min(tq, s)
    qb, kb, vb = q_col0 // dh, q_col0 // dh + heads, q_col0 // dh + 2 * heads
    return pl.pallas_call(
        functools.partial(_fox_attn_body, tq=tq, scale=dh ** -0.5),
        grid=(bn, heads, s // tq),
        in_specs=[pl.BlockSpec((1, tq, dh), lambda b, h, i: (b, i, qb + h)),
                  pl.BlockSpec((1, s, dh), lambda b, h, i: (b, 0, kb + h)),
                  pl.BlockSpec((1, s, dh), lambda b, h, i: (b, 0, vb + h)),
                  pl.BlockSpec((1, 1, tq, 1), lambda b, h, i: (b, h, i, 0)),
                  pl.BlockSpec((1, 1, 1, s), lambda b, h, i: (b, h, 0, 0))],
        out_specs=pl.BlockSpec((1, tq, dh), lambda b, h, i: (b, i, h)),
        out_shape=jax.ShapeDtypeStruct((bn, s, heads * dh), BF16),
        compiler_params=_params("parallel", "parallel", "arbitrary"),
        name="fox_attention",
    )(proj, proj, proj, f[..., None], f[:, :, None, :])


def _ssd_body(xbc_ref, z_ref, dt_ref, cw_ref, cb_ref, dtb_ref, alog_ref, dexp_ref, ng_ref, o_ref,
              act_ref, tail_ref, state_ref, acst_ref, dtt_ref, wt_ref, *, groups, inner):
    c = pl.program_id(1)
    L = SSD_CHUNK
    gw = inner // groups
    pairs = gw // _LANES
    hpg = gw // SSD_HEAD_DIM
    conv_dim = act_ref.shape[1]

    @pl.when(c == 0)
    def _():
        tail_ref[...] = jnp.zeros_like(tail_ref)
        state_ref[...] = jnp.zeros_like(state_ref)

    for j0 in range(0, conv_dim, gw):
        cols = slice(j0, j0 + gw)
        raw = xbc_ref[0, :, cols].astype(F32)
        acc = _causal_conv(raw, tail_ref[:, cols], cw_ref, cb_ref, cols)
        tail_ref[:, cols] = raw[L - _SUBLANES:, :]
        act_ref[:, cols] = acc * _sigmoid(acc)

    dt = _softplus(dt_ref[0] + dtb_ref[...])
    acs = _cumsum_rows(dt * (-jnp.exp(alog_ref[...])))
    acs_t = acs.T
    dt_t = dt.T
    acst_ref[...] = acs_t
    dtt_ref[...] = dt_t
    wt_ref[...] = dt_t * jnp.exp(acs_t[:, L - 1:L] - acs_t)
    dec_last = jnp.exp(acs[L - 1:L, :])

    rq = lax.broadcasted_iota(jnp.int32, (L, L), 0)
    ck = lax.broadcasted_iota(jnp.int32, (L, L), 1)
    causal = ck <= rq
    lo = lax.broadcasted_iota(jnp.int32, (L, _LANES), 1) < SSD_HEAD_DIM
    lo_row = lo[:1]

    for g in range(groups):
        bmat = act_ref[:, inner + g * SSD_STATE:inner + (g + 1) * SSD_STATE]
        cmat = act_ref[:, inner + (groups + g) * SSD_STATE:inner + (groups + g + 1) * SSD_STATE]
        cbm = lax.dot_general(cmat.astype(BF16), bmat.astype(BF16), (((1,), (1,)), ((), ())),
                              preferred_element_type=F32)
        b_t = bmat.T
        ys = []
        for j in range(pairs):
            col0 = g * gw + j * _LANES
            xs_p = act_ref[:, col0:col0 + _LANES]
            m_parts, c_parts, bw_parts, dl = [], [], [], []
            for e in range(2):
                hd = g * hpg + 2 * j + e
                col = acs[:, hd:hd + 1]
                row = acst_ref[hd:hd + 1, :]
                dec = jnp.exp(jnp.where(causal, col - row, -jnp.inf))
                m_parts.append((cbm * dec * dtt_ref[hd:hd + 1, :]).astype(BF16))
                c_parts.append((cmat * jnp.exp(col)).astype(BF16))
                bw_parts.append((b_t * wt_ref[hd:hd + 1, :]).astype(BF16))
                dl.append(dec_last[:, hd:hd + 1])
            prev = state_ref[:, col0:col0 + _LANES]
            x_top = jnp.where(lo, xs_p, 0.0).astype(BF16)
            x_bot = jnp.where(lo, 0.0, xs_p).astype(BF16)
            p_top = jnp.where(lo, prev, 0.0).astype(BF16)
            p_bot = jnp.where(lo, 0.0, prev).astype(BF16)
            lhs = jnp.concatenate(m_parts + c_parts, axis=1)
            rhs = jnp.concatenate([x_top, x_bot, p_top, p_bot], axis=0)
            y = jnp.dot(lhs, rhs, preferred_element_type=F32)
            st = jnp.dot(jnp.concatenate(bw_parts, axis=1), jnp.concatenate([x_top, x_bot], axis=0),
                         preferred_element_type=F32)
            state_ref[:, col0:col0 + _LANES] = prev * jnp.where(lo_row, dl[0], dl[1]) + st
            y = y + dexp_ref[:, col0:col0 + _LANES] * xs_p
            zp = z_ref[0, :, col0:col0 + _LANES].astype(F32)
            ys.append(y * (zp * _sigmoid(zp)))
        yg = jnp.concatenate(ys, axis=1)
        ms = jnp.mean(yg * yg, axis=1, keepdims=True)
        gcols = slice(g * gw, (g + 1) * gw)
        o_ref[0, :, gcols] = (yg * lax.rsqrt(ms + RMS_EPS) * ng_ref[:, gcols]).astype(o_ref.dtype)


def _ssd_core(xbc, z, dt, conv_w, conv_b, dt_bias, a_log, d_skip, norm_g, groups):
    bn, s, conv_dim = xbc.shape
    inner = z.shape[2]
    heads = a_log.shape[0]
    L = SSD_CHUNK
    pad = lambda v: jnp.pad(v, (0, _LANES - heads)).reshape(1, _LANES)
    const2 = lambda b, c: (0, 0)
    return pl.pallas_call(
        functools.partial(_ssd_body, groups=groups, inner=inner),
        grid=(bn, s // L),
        in_specs=[pl.BlockSpec((1, L, conv_dim), lambda b, c: (b, c, 0)),
                  pl.BlockSpec((1, L, inner), lambda b, c: (b, c, 0)),
                  pl.BlockSpec((1, L, _LANES), lambda b, c: (b, c, 0)),
                  pl.BlockSpec((CONV_K, conv_dim), const2),
                  pl.BlockSpec((1, conv_dim), const2),
                  pl.BlockSpec((1, _LANES), const2),
                  pl.BlockSpec((1, _LANES), const2),
                  pl.BlockSpec((1, inner), const2),
                  pl.BlockSpec((1, inner), const2)],
        out_specs=pl.BlockSpec((1, L, inner), lambda b, c: (b, c, 0)),
        out_shape=jax.ShapeDtypeStruct((bn, s, inner), BF16),
        scratch_shapes=[pltpu.VMEM((L, conv_dim), F32),
                        pltpu.VMEM((_SUBLANES, conv_dim), F32),
                        pltpu.VMEM((SSD_STATE, inner), F32),
                        pltpu.VMEM((_LANES, L), F32),
                        pltpu.VMEM((_LANES, L), F32),
                        pltpu.VMEM((_LANES, L), F32)],
        compiler_params=_params("parallel", "arbitrary"),
        name="ssd_core",
    )(xbc, z, dt, conv_w, conv_b.reshape(1, conv_dim), pad(dt_bias), pad(a_log),
      jnp.repeat(d_skip, SSD_HEAD_DIM).reshape(1, inner), norm_g.reshape(1, inner))


def _router_body(x_ref, whl_ref, wh_ref, b_ref, oi_ref, of_ref, cnt_ref, base_ref):
    i = pl.program_id(0)

    @pl.when(i == 0)
    def _():
        base_ref[...] = jnp.zeros_like(base_ref)

    x = x_ref[...]
    tm = x.shape[0]
    xh = x.astype(BF16)
    xl = (x - xh.astype(F32)).astype(BF16)
    r = jnp.dot(xh, whl_ref[...], preferred_element_type=F32)
    logits = (r[:, :ROUTER_COLS] + r[:, ROUTER_COLS:]
              + jnp.dot(xl, wh_ref[...], preferred_element_type=F32) + b_ref[...])
    lt = logits.T

    row = lax.broadcasted_iota(jnp.int32, (_SUBLANES, tm), 0)
    neg = -jnp.inf
    l1 = jnp.where(row < MOE_GROUPS, lt[:_SUBLANES], neg)
    e1 = jnp.exp(l1 - jnp.max(l1, axis=0, keepdims=True))
    p1 = e1 / jnp.sum(e1, axis=0, keepdims=True)
    pg = jnp.max(p1, axis=0, keepdims=True)
    gidx = jnp.min(jnp.where(p1 == pg, row, _SUBLANES), axis=0, keepdims=True)

    sel = jnp.full((_SUBLANES, tm), neg, F32)
    for gi in range(MOE_GROUPS):
        r0 = ROUTER_L2_ROW0 + _SUBLANES * gi
        sel = jnp.where((gidx == gi) & (row < MOE_PER_GROUP), lt[r0:r0 + _SUBLANES], sel)
    va = jnp.max(sel, axis=0, keepdims=True)
    ia = jnp.min(jnp.where(sel == va, row, _SUBLANES), axis=0, keepdims=True)
    sel_b = jnp.where(row == ia, neg, sel)
    vb = jnp.max(sel_b, axis=0, keepdims=True)
    ib = jnp.min(jnp.where(sel_b == vb, row, _SUBLANES), axis=0, keepdims=True)
    t = jnp.exp(vb - va)
    ca = pg * (1.0 / (1.0 + t))
    cb = pg * (t / (1.0 + t))
    ea = gidx * MOE_PER_GROUP + ia
    eb = gidx * MOE_PER_GROUP + ib

    erow = lax.broadcasted_iota(jnp.int32, (N_EXPERTS, tm), 0)
    hit_a = erow == ea
    hit_b = erow == eb
    oh = jnp.where(hit_a, 1.0, 0.0) + jnp.where(hit_b, 1.0, 0.0)
    before = (lax.broadcasted_iota(jnp.int32, (tm, tm), 0)
              < lax.broadcasted_iota(jnp.int32, (tm, tm), 1))
    prefix = jnp.dot(oh.astype(BF16), jnp.where(before, 1.0, 0.0).astype(BF16),
                     preferred_element_type=F32) + base_ref[...]
    rank_a = jnp.sum(jnp.where(hit_a, prefix, 0.0), axis=0, keepdims=True)
    rank_b = jnp.sum(jnp.where(hit_b, prefix, 0.0), axis=0, keepdims=True)
    base_ref[...] += jnp.sum(oh, axis=1, keepdims=True)
    cnt_ref[...] = jnp.broadcast_to(base_ref[...], cnt_ref.shape)

    zi = jnp.zeros((1, tm), jnp.int32)
    oi_ref[...] = jnp.concatenate(
        [ea, eb, rank_a.astype(jnp.int32), rank_b.astype(jnp.int32), zi, zi, zi, zi], axis=0)
    zf = jnp.zeros((1, tm), F32)
    of_ref[...] = jnp.concatenate([ca, cb, zf, zf, zf, zf, zf, zf], axis=0)


def _router(x, w_r1, b_r1, w_r2, b_r2, tm):
    t, d = x.shape
    tm = min(tm, t)
    w = jnp.zeros((d, ROUTER_COLS), F32).at[:, :MOE_GROUPS].set(w_r1)
    b = jnp.zeros((ROUTER_COLS,), F32).at[:MOE_GROUPS].set(b_r1)
    for gi in range(MOE_GROUPS):
        c0 = ROUTER_L2_ROW0 + _SUBLANES * gi
        w = w.at[:, c0:c0 + MOE_PER_GROUP].set(w_r2[gi])
        b = b.at[c0:c0 + MOE_PER_GROUP].set(b_r2[gi])
    wh = w.astype(BF16)
    wl = (w - wh.astype(F32)).astype(BF16)
    const = lambda i: (0, 0)
    return pl.pallas_call(
        _router_body,
        grid=(t // tm,),
        in_specs=[pl.BlockSpec((tm, d), lambda i: (i, 0)),
                  pl.BlockSpec((d, 2 * ROUTER_COLS), const),
                  pl.BlockSpec((d, ROUTER_COLS), const),
                  pl.BlockSpec((1, ROUTER_COLS), const)],
        out_specs=[pl.BlockSpec((_SUBLANES, tm), lambda i: (0, i)),
                   pl.BlockSpec((_SUBLANES, tm), lambda i: (0, i)),
                   pl.BlockSpec((N_EXPERTS, _LANES), const)],
        out_shape=[jax.ShapeDtypeStruct((_SUBLANES, t), jnp.int32),
                   jax.ShapeDtypeStruct((_SUBLANES, t), F32),
                   jax.ShapeDtypeStruct((N_EXPERTS, _LANES), F32)],
        scratch_shapes=[pltpu.VMEM((N_EXPERTS, 1), F32)],
        compiler_params=_params("arbitrary"),
        name="moe_router",
    )(x, jnp.concatenate([wh, wl], axis=1), wh, b.reshape(1, ROUTER_COLS))


def _row_copy(src, src_row, dst, dst_row, sem):
    return pltpu.make_async_copy(src.at[pl.ds(src_row, 1)], dst.at[pl.ds(dst_row, 1)], sem)


def _dispatch_body(pos_ref, x_ref, xs_in_ref, xs_ref, sem, *, td, t_total):
    del xs_in_ref
    base = pl.program_id(0) * td

    def issue(r, carry):
        _row_copy(x_ref, r, xs_ref, pos_ref[base + r], sem).start()
        _row_copy(x_ref, r, xs_ref, pos_ref[t_total + base + r], sem).start()
        return carry

    def drain(r, carry):
        _row_copy(x_ref, 0, xs_ref, 0, sem).wait()
        _row_copy(x_ref, 0, xs_ref, 0, sem).wait()
        return carry

    lax.fori_loop(0, td, issue, 0)
    lax.fori_loop(0, td, drain, 0)


def _dispatch(x, pos, n_rows, td):
    t, d = x.shape
    td = min(td, t)
    return pl.pallas_call(
        functools.partial(_dispatch_body, td=td, t_total=t),
        grid_spec=pltpu.PrefetchScalarGridSpec(
            num_scalar_prefetch=1,
            grid=(t // td,),
            in_specs=[pl.BlockSpec((td, d), lambda i, pos: (i, 0)),
                      pl.BlockSpec(memory_space=pl.ANY)],
            out_specs=pl.BlockSpec(memory_space=pl.ANY),
            scratch_shapes=[pltpu.SemaphoreType.DMA(())]),
        out_shape=jax.ShapeDtypeStruct((n_rows, d), x.dtype),
        input_output_aliases={2: 0},
        compiler_params=_params("arbitrary"),
        name="moe_dispatch",
    )(pos, x, jnp.zeros((n_rows, d), x.dtype))


def _expert_body(blk_ref, te_ref, nv_ref, xs_ref, wg_ref, wu_ref, wd_ref, o_ref):
    del blk_ref, te_ref
    valid = pl.program_id(0) < nv_ref[0]

    @pl.when(jnp.logical_not(valid))
    def _():
        o_ref[...] = jnp.zeros_like(o_ref)

    @pl.when(valid)
    def _():
        x = xs_ref[...].astype(BF16)
        g = jnp.dot(x, wg_ref[0], preferred_element_type=F32)
        u = jnp.dot(x, wu_ref[0], preferred_element_type=F32)
        hid = (g * _sigmoid(g)) * u
        o_ref[...] = jnp.dot(hid.astype(BF16), wd_ref[0], preferred_element_type=F32)


def _experts(xs, w_gate, w_up, w_down, blk, tile_expert, n_valid, tm):
    p, d = xs.shape
    ff = w_gate.shape[2]
    n_tiles = p // tm
    rows = lambda i, blk, te, nv: (blk[i], 0)
    wsel = lambda i, blk, te, nv: (te[i], 0, 0)
    return pl.pallas_call(
        _expert_body,
        grid_spec=pltpu.PrefetchScalarGridSpec(
            num_scalar_prefetch=3,
            grid=(n_tiles,),
            in_specs=[pl.BlockSpec((tm, d), rows),
                      pl.BlockSpec((1, d, ff), wsel),
                      pl.BlockSpec((1, d, ff), wsel),
                      pl.BlockSpec((1, ff, d), wsel)],
            out_specs=pl.BlockSpec((tm, d), lambda i, blk, te, nv: (i, 0))),
        out_shape=jax.ShapeDtypeStruct((p, d), F32),
        compiler_params=_params("arbitrary"),
        name="moe_experts",
    )(blk, tile_expert, n_valid, xs, w_gate, w_up, w_down)


def _combine_body(pos_ref, ys_ref, x_ref, c_ref, g_ref, b_ref, of_ref, ob_ref, buf_a, buf_b, sem,
                  *, tc, t_total, alpha):
    base = pl.program_id(0) * tc

    def issue(r, carry):
        _row_copy(ys_ref, pos_ref[base + r], buf_a, r, sem).start()
        _row_copy(ys_ref, pos_ref[t_total + base + r], buf_b, r, sem).start()
        return carry

    def drain(r, carry):
        _row_copy(ys_ref, 0, buf_a, 0, sem).wait()
        _row_copy(ys_ref, 0, buf_b, 0, sem).wait()
        return carry

    lax.fori_loop(0, tc, issue, 0)
    lax.fori_loop(0, tc, drain, 0)
    c = c_ref[...]
    f = c[:, 0:1] * buf_a[...] + c[:, 1:2] * buf_b[...]
    y = _layer_norm(alpha * x_ref[...] + f, g_ref[...], b_ref[...])
    of_ref[...] = y
    ob_ref[...] = y.astype(BF16)


def _combine_ln(ys, pos, x, c, g, b, alpha, tc):
    t, d = x.shape
    tc = min(tc, t)
    const = lambda i, pos: (0, 0)
    tile = lambda i, pos: (i, 0)
    return pl.pallas_call(
        functools.partial(_combine_body, tc=tc, t_total=t, alpha=alpha),
        grid_spec=pltpu.PrefetchScalarGridSpec(
            num_scalar_prefetch=1,
            grid=(t // tc,),
            in_specs=[pl.BlockSpec(memory_space=pl.ANY),
                      pl.BlockSpec((tc, d), tile),
                      pl.BlockSpec((tc, 2), tile),
                      pl.BlockSpec((1, d), const),
                      pl.BlockSpec((1, d), const)],
            out_specs=[pl.BlockSpec((tc, d), tile), pl.BlockSpec((tc, d), tile)],
            scratch_shapes=[pltpu.VMEM((tc, d), F32), pltpu.VMEM((tc, d), F32),
                            pltpu.SemaphoreType.DMA(())]),
        out_shape=[jax.ShapeDtypeStruct((t, d), F32), jax.ShapeDtypeStruct((t, d), BF16)],
        compiler_params=_params("arbitrary"),
        name="moe_combine_ln",
    )(pos, ys, x, c, g.reshape(1, d), b.reshape(1, d))


def _moe_ln(x, w_r1, b_r1, w_r2, b_r2, w_gate, w_up, w_down, g, b, alpha, tiles):
    t, d = x.shape
    tm_e = min(tiles["expert_rows"], t)
    oi, of, cnt = _router(x, w_r1, b_r1, w_r2, b_r2, tiles["router"])
    counts = cnt[:, 0].astype(jnp.int32)
    seg_tiles = (counts + tm_e - 1) // tm_e
    tile_end = jnp.cumsum(seg_tiles)
    seg_start = (tile_end - seg_tiles) * tm_e
    pos = jnp.concatenate([seg_start[oi[0]] + oi[2], seg_start[oi[1]] + oi[3]])
    n_tiles = 2 * t // tm_e + N_EXPERTS
    n_valid = tile_end[-1:]
    blk = jnp.minimum(jnp.arange(n_tiles, dtype=jnp.int32), n_valid - 1)
    tile_expert = jnp.minimum(jnp.searchsorted(tile_end, blk, side="right"), N_EXPERTS - 1).astype(jnp.int32)

    xs = _dispatch(x, pos, n_tiles * tm_e, tiles["dispatch"])
    ff = w_gate.shape[-1]
    ys = _experts(xs, w_gate.reshape(N_EXPERTS, d, ff).astype(BF16),
                  w_up.reshape(N_EXPERTS, d, ff).astype(BF16),
                  w_down.reshape(N_EXPERTS, ff, d).astype(BF16), blk, tile_expert, n_valid, tm_e)
    return _combine_ln(ys, pos, x, of[:2].T, g, b, alpha, tiles["combine"])


def _hybrid_mixer_ln(xf, xb, bn, w_in, conv_w, conv_b, w_r, b_r, w_i, b_i, lam, b_f, w_out, g, b, alpha, tiles):
    t, d = xf.shape
    s = t // bn
    heads, block, _ = w_r.shape
    lru_w = heads * block
    fox_heads = b_f.shape[0]
    main = w_in.shape[1] - fox_heads
    fox_w = (main - 2 * lru_w) // 3
    dh = fox_w // fox_heads
    proj = _matmul(xb, w_in[:, :main].astype(BF16), BF16, tiles["mm_m"], tiles["mm_n"], "hy_in_proj")
    w_fl = jnp.pad(w_in[:, main:], ((0, 0), (0, _LANES - fox_heads))).astype(BF16)
    fl = _matmul(xb, w_fl, F32, tiles["mm_m"], _LANES, "hy_forget_proj")[:, :fox_heads]
    f = _fox_prep(fl.reshape(bn, s, fox_heads).transpose(0, 2, 1), b_f)
    proj = proj.reshape(bn, s, main)
    y_lru = _lru(proj, conv_w, conv_b, w_r, b_r, w_i, b_i, lam, tiles["lru"])
    y_att = _fox_attention(proj, f, fox_heads, dh, 2 * lru_w, tiles["attn"])
    y = jnp.concatenate([y_lru, y_att], axis=-1).reshape(t, lru_w + fox_w)
    return _matmul_residual_ln(y, w_out.astype(BF16), xf, g, b, alpha, tiles["ln_m"], tiles["ln_k"], "hy_out_proj_ln")


def _ssd_mixer_ln(xf, xb, bn, w_in, conv_w, conv_b, dt_bias, a_log, d_skip, norm_g, w_out, g, b, alpha, tiles):
    t, d = xf.shape
    s = t // bn
    heads = a_log.shape[0]
    inner = norm_g.shape[0]
    conv_dim = conv_w.shape[1]
    groups = (conv_dim - inner) // (2 * SSD_STATE)
    z = _matmul(xb, w_in[:, :inner].astype(BF16), BF16, tiles["mm_m"], tiles["mm_n"], "ssd_z_proj")
    xbc = _matmul(xb, w_in[:, inner:inner + conv_dim].astype(BF16), BF16, tiles["mm_m"], tiles["mm_n"], "ssd_xbc_proj")
    w_dt = jnp.pad(w_in[:, inner + conv_dim:], ((0, 0), (0, _LANES - heads))).astype(BF16)
    dt = _matmul(xb, w_dt, F32, tiles["mm_m"], _LANES, "ssd_dt_proj")
    y = _ssd_core(xbc.reshape(bn, s, conv_dim), z.reshape(bn, s, inner), dt.reshape(bn, s, _LANES),
                  conv_w, conv_b, dt_bias, a_log, d_skip, norm_g, groups)
    return _matmul_residual_ln(y.reshape(t, inner), w_out.astype(BF16), xf, g, b, alpha,
                               tiles["ln_m"], tiles["ln_k"], "ssd_out_proj_ln")


_TILES = dict(mm_m=1024, mm_n=1024, ln_m=512, ln_k=1024, lru=256, attn=512,
              router=512, dispatch=512, expert_rows=512, combine=256)


def kernel(x, ln_g, ln_b, hy_w_in, hy_conv_w, hy_conv_b, hy_w_r, hy_b_r, hy_w_i, hy_b_i, hy_lambda, hy_b_f, hy_w_out, ssd_w_in, ssd_conv_w, ssd_conv_b, ssd_dt_bias, ssd_a_log, ssd_d, ssd_norm_g, ssd_w_out, moe_w_r1, moe_b_r1, moe_w_r2, moe_b_r2, moe_w_gate, moe_w_up, moe_w_down):
    bn, s, d = x.shape
    depth = ln_g.shape[0]
    alpha = (2 * depth) ** 0.25
    xf = x.reshape(bn * s, d)
    xb = xf.astype(BF16)
    for layer in range(depth):
        j = layer // 2
        if layer % 2 == 0:
            xf, xb = _hybrid_mixer_ln(xf, xb, bn, hy_w_in[j], hy_conv_w[j], hy_conv_b[j], hy_w_r[j], hy_b_r[j],
                                      hy_w_i[j], hy_b_i[j], hy_lambda[j], hy_b_f[j], hy_w_out[j],
                                      ln_g[layer, 0], ln_b[layer, 0], alpha, _TILES)
        else:
            xf, xb = _ssd_mixer_ln(xf, xb, bn, ssd_w_in[j], ssd_conv_w[j], ssd_conv_b[j], ssd_dt_bias[j],
                                   ssd_a_log[j], ssd_d[j], ssd_norm_g[j], ssd_w_out[j],
                                   ln_g[layer, 0], ln_b[layer, 0], alpha, _TILES)
        xf, xb = _moe_ln(xf, moe_w_r1[layer], moe_b_r1[layer], moe_w_r2[layer], moe_b_r2[layer],
                         moe_w_gate[layer], moe_w_up[layer], moe_w_down[layer],
                         ln_g[layer, 1], ln_b[layer, 1], alpha, _TILES)
    return xf.reshape(bn, s, d)
```

```python
import functools
import math

import jax
import jax.numpy as jnp
from jax import lax
from jax.experimental import pallas as pl
from jax.experimental.pallas import tpu as pltpu

F32 = jnp.float32
BF16 = jnp.bfloat16

_VMEM_LIMIT_BYTES = 56 * 1024 * 1024
_LANES = 128
_SUBLANES = 8

LN_EPS = 1e-5
RMS_EPS = 1e-5
LRU_C = 8.0
CONV_K = 4
SSD_CHUNK = 128
SSD_HEAD_DIM = 64
SSD_STATE = 128
MOE_GROUPS = 4
MOE_PER_GROUP = 4
N_EXPERTS = MOE_GROUPS * MOE_PER_GROUP
ROUTER_COLS = 128
ROUTER_L2_ROW0 = 8
LOG2E = math.log2(math.e)


def _params(*sem):
    return pltpu.CompilerParams(dimension_semantics=sem, vmem_limit_bytes=_VMEM_LIMIT_BYTES)


def _sigmoid(x):
    return 1.0 / (1.0 + jnp.exp(-x))


def _softplus(x):
    e = jnp.exp(-jnp.abs(x))
    u = 1.0 + e
    return jnp.maximum(x, 0.0) + jnp.where(u == 1.0, e, jnp.log(u) * (e / (u - 1.0)))


def _shift_rows(x, tail, d):
    r = pltpu.roll(x, d, axis=0)
    t = pltpu.roll(tail, d, axis=0)
    row = lax.broadcasted_iota(jnp.int32, tail.shape, 0)
    first = jnp.where(row < d, t, r[:_SUBLANES])
    return jnp.concatenate([first, r[_SUBLANES:]], axis=0)


def _causal_conv(x, tail, w_ref, b_ref):
    acc = b_ref[...] + w_ref[CONV_K - 1:CONV_K, :] * x
    for d in range(1, CONV_K):
        acc = acc + w_ref[CONV_K - 1 - d:CONV_K - d, :] * _shift_rows(x, tail, d)
    return acc


def _cumsum_rows(x):
    n = x.shape[0]
    row = lax.broadcasted_iota(jnp.int32, x.shape, 0)
    d = 1
    while d < n:
        x = x + jnp.where(row >= d, pltpu.roll(x, d, axis=0), 0.0)
        d *= 2
    return x


def _cumsum_lanes(x):
    n = x.shape[1]
    lane = lax.broadcasted_iota(jnp.int32, x.shape, 1)
    d = 1
    while d < n:
        x = x + jnp.where(lane >= d, pltpu.roll(x, d, axis=1), 0.0)
        d *= 2
    return x


def _layer_norm(s, g, b):
    mu = jnp.mean(s, axis=-1, keepdims=True)
    d = s - mu
    var = jnp.mean(d * d, axis=-1, keepdims=True)
    return d * lax.rsqrt(var + LN_EPS) * g + b


def _proj_body(x_ref, w_ref, *rest, epilogue, tiles_per_seq):
    if epilogue.startswith("conv"):
        cw_ref, cb_ref, o_ref, carry_ref = rest
    else:
        (o_ref,) = rest
    acc = jnp.dot(x_ref[...], w_ref[...], preferred_element_type=F32)
    if epilogue.startswith("conv"):
        @pl.when(pl.program_id(1) % tiles_per_seq == 0)
        def _():
            carry_ref[...] = jnp.zeros_like(carry_ref)

        tail = carry_ref[...]
        carry_ref[...] = acc[acc.shape[0] - _SUBLANES:, :]
        acc = _causal_conv(acc, tail, cw_ref, cb_ref)
    if epilogue.endswith("silu"):
        acc = acc * _sigmoid(acc)
    elif epilogue == "gelu":
        acc = jax.nn.gelu(acc)
    o_ref[...] = acc.astype(o_ref.dtype)


def _proj(x, w, out_dtype, tm, tn, name, epilogue="none", conv=None, seq_len=None):
    m, k = x.shape
    n = w.shape[1]
    tm, tn = min(tm, m), min(tn, n)
    in_specs = [pl.BlockSpec((tm, k), lambda j, i: (i, 0)),
                pl.BlockSpec((k, tn), lambda j, i: (0, j))]
    args = [x, w]
    scratch = []
    tiles_per_seq = 1
    if conv is not None:
        assert seq_len % tm == 0
        tiles_per_seq = seq_len // tm
        in_specs += [pl.BlockSpec((CONV_K, tn), lambda j, i: (0, j)),
                     pl.BlockSpec((1, tn), lambda j, i: (0, j))]
        args += [conv[0], conv[1].reshape(1, n)]
        scratch = [pltpu.VMEM((_SUBLANES, tn), F32)]
    return pl.pallas_call(
        functools.partial(_proj_body, epilogue=epilogue, tiles_per_seq=tiles_per_seq),
        grid=(n // tn, m // tm),
        in_specs=in_specs,
        out_specs=pl.BlockSpec((tm, tn), lambda j, i: (i, j)),
        out_shape=jax.ShapeDtypeStruct((m, n), out_dtype),
        scratch_shapes=scratch,
        compiler_params=_params("parallel", "arbitrary"),
        name=name,
    )(*args)


def _mm_ln_body(a_ref, w_ref, x_ref, g_ref, b_ref, of_ref, ob_ref, acc_ref, *, alpha):
    k = pl.program_id(1)
    part = jnp.dot(a_ref[...], w_ref[...], preferred_element_type=F32)

    @pl.when(k == 0)
    def _():
        acc_ref[...] = part

    @pl.when(k > 0)
    def _():
        acc_ref[...] += part

    @pl.when(k == pl.num_programs(1) - 1)
    def _():
        y = _layer_norm(alpha * x_ref[...] + acc_ref[...], g_ref[...], b_ref[...])
        of_ref[...] = y
        ob_ref[...] = y.astype(BF16)


def _matmul_residual_ln(a, w, x, g, b, alpha, tm, tk, name):
    m, kdim = a.shape
    n = w.shape[1]
    tm, tk = min(tm, m), min(tk, kdim)
    return pl.pallas_call(
        functools.partial(_mm_ln_body, alpha=alpha),
        grid=(m // tm, kdim // tk),
        in_specs=[pl.BlockSpec((tm, tk), lambda i, k: (i, k)),
                  pl.BlockSpec((tk, n), lambda i, k: (k, 0)),
                  pl.BlockSpec((tm, n), lambda i, k: (i, 0)),
                  pl.BlockSpec((1, n), lambda i, k: (0, 0)),
                  pl.BlockSpec((1, n), lambda i, k: (0, 0))],
        out_specs=[pl.BlockSpec((tm, n), lambda i, k: (i, 0)),
                   pl.BlockSpec((tm, n), lambda i, k: (i, 0))],
        out_shape=[jax.ShapeDtypeStruct((m, n), F32), jax.ShapeDtypeStruct((m, n), BF16)],
        scratch_shapes=[pltpu.VMEM((tm, n), F32)],
        compiler_params=_params("parallel", "arbitrary"),
        name=name,
    )(a, w, x, g.reshape(1, n), b.reshape(1, n))


def _fox_prep_body(fl_ref, bf_ref, f_ref):
    x = fl_ref[0] + bf_ref[...]
    f_ref[0] = _cumsum_lanes(-_softplus(-x)) * LOG2E


def _fox_prep(fl_t, b_f):
    bn, h, s = fl_t.shape
    return pl.pallas_call(
        _fox_prep_body,
        grid=(bn,),
        in_specs=[pl.BlockSpec((1, h, s), lambda b: (b, 0, 0)),
                  pl.BlockSpec((h, 1), lambda b: (0, 0))],
        out_specs=pl.BlockSpec((1, h, s), lambda b: (b, 0, 0)),
        out_shape=jax.ShapeDtypeStruct((bn, h, s), F32),
        compiler_params=_params("parallel"),
        name="fox_prep",
    )(fl_t, b_f.reshape(h, 1))


def _lru_body(xc_ref, gg_ref, wr_ref, br_ref, wi_ref, bi_ref, lam_ref, y_ref, h_ref, *, heads, block):
    @pl.when(pl.program_id(1) == 0)
    def _():
        h_ref[...] = jnp.zeros_like(h_ref)

    xcb = xc_ref[0]
    xc = xcb.astype(F32)
    ts = xc.shape[0]
    r_parts, i_parts = [], []
    for hd in range(heads):
        xh = xcb[:, hd * block:(hd + 1) * block]
        r_parts.append(jnp.dot(xh, wr_ref[hd], preferred_element_type=F32))
        i_parts.append(jnp.dot(xh, wi_ref[hd], preferred_element_type=F32))
    r = _sigmoid(jnp.concatenate(r_parts, axis=1) + br_ref[...])
    ig = _sigmoid(jnp.concatenate(i_parts, axis=1) + bi_ref[...])
    log_a = -LRU_C * r * _softplus(-lam_ref[...])
    a = jnp.exp(log_a)
    u = jnp.sqrt(-jnp.tanh(log_a) * (a * a + 1.0)) * (ig * xc)

    row = lax.broadcasted_iota(jnp.int32, a.shape, 0)
    d = 1
    while d < ts:
        keep = row >= d
        u = u + a * jnp.where(keep, pltpu.roll(u, d, axis=0), 0.0)
        a = a * jnp.where(keep, pltpu.roll(a, d, axis=0), 1.0)
        d *= 2
    h = u + a * h_ref[...]
    h_ref[...] = h[ts - 1:ts, :]
    y_ref[0] = (h * gg_ref[0].astype(F32)).astype(y_ref.dtype)


def _lru(xc, gg, w_r, b_r, w_i, b_i, lam, ts):
    bn, s, width = xc.shape
    heads, block, _ = w_r.shape
    ts = min(ts, s)
    vec = lambda v: v.reshape(1, width)
    const2 = lambda b, i: (0, 0)
    const3 = lambda b, i: (0, 0, 0)
    tile = pl.BlockSpec((1, ts, width), lambda b, i: (b, i, 0))
    return pl.pallas_call(
        functools.partial(_lru_body, heads=heads, block=block),
        grid=(bn, s // ts),
        in_specs=[tile, tile,
                  pl.BlockSpec((heads, block, block), const3),
                  pl.BlockSpec((1, width), const2),
                  pl.BlockSpec((heads, block, block), const3),
                  pl.BlockSpec((1, width), const2),
                  pl.BlockSpec((1, width), const2)],
        out_specs=tile,
        out_shape=jax.ShapeDtypeStruct((bn, s, width), BF16),
        scratch_shapes=[pltpu.VMEM((1, width), F32)],
        compiler_params=_params("parallel", "arbitrary"),
        name="rg_lru",
    )(xc, gg, w_r.astype(BF16), vec(b_r), w_i.astype(BF16), vec(b_i), vec(lam))


def _split3(f):
    hi = f.astype(BF16).astype(F32)
    r = f - hi
    mid = r.astype(BF16).astype(F32)
    return hi, mid, r - mid


def _fox_attn_body(q_ref, k_ref, v_ref, fc_ref, o_ref, kx_ref, *, tq):
    qi = pl.program_id(2)
    s_len = k_ref.shape[1]
    dh = q_ref.shape[2]

    @pl.when(qi == 0)
    def _():
        for c0 in range(0, s_len, tq):
            hi, mid, lo = _split3(fc_ref[0, 0, c0:c0 + tq, :])
            lane = lax.broadcasted_iota(jnp.int32, (tq, _LANES), 1)
            kx = jnp.where(lane < 3, 1.0, jnp.where(lane == 3, -hi, jnp.where(lane == 4, -mid,
                           jnp.where(lane == 5, -lo, 0.0))))
            kx_ref[c0:c0 + tq, :] = kx.astype(BF16)

    hi, mid, lo = _split3(fc_ref[0, 0, pl.ds(pl.multiple_of(qi * tq, tq), tq), :])
    lane = lax.broadcasted_iota(jnp.int32, (tq, _LANES), 1)
    qx = jnp.where(lane == 0, hi, jnp.where(lane == 1, mid, jnp.where(lane == 2, lo,
                   jnp.where(lane < 6, 1.0, 0.0))))
    q = jnp.concatenate([q_ref[0], qx.astype(BF16)], axis=1)

    def scores(j):
        start = pl.multiple_of(j * tq, tq)
        k = jnp.concatenate([k_ref[0, pl.ds(start, tq), :], kx_ref[pl.ds(start, tq), :]], axis=1)
        return lax.dot_general(q, k, (((1,), (1,)), ((), ())), preferred_element_type=F32), start

    def update(carry, sc, start):
        m, l, acc = carry
        m_new = jnp.maximum(m, jnp.max(sc, axis=1, keepdims=True))
        corr = jnp.exp2(m - m_new)
        p = jnp.exp2(sc - m_new)
        l = corr * l + jnp.sum(p, axis=1, keepdims=True)
        v = v_ref[0, pl.ds(start, tq), :]
        acc = corr * acc + jnp.dot(p.astype(BF16), v, preferred_element_type=F32)
        return m_new, l, acc

    def diagonal(carry):
        sc, start = scores(qi)
        rq = lax.broadcasted_iota(jnp.int32, sc.shape, 0)
        ck = lax.broadcasted_iota(jnp.int32, sc.shape, 1)
        return update(carry, jnp.where(ck <= rq, sc, -1e30), start)

    def two_blocks(jj, carry):
        sc_a, start_a = scores(2 * jj)
        sc_b, start_b = scores(2 * jj + 1)
        return update(update(carry, sc_a, start_a), sc_b, start_b)

    def odd_tail(carry):
        sc_a, start_a = scores(qi - 1)
        return diagonal(update(carry, sc_a, start_a))

    init = (jnp.full((tq, 1), -1e30, F32), jnp.zeros((tq, 1), F32), jnp.zeros((tq, dh), F32))
    carry = lax.fori_loop(0, qi // 2, two_blocks, init)
    _, l, acc = lax.cond(qi % 2 == 1, odd_tail, diagonal, carry)
    o_ref[0] = (acc / l).astype(o_ref.dtype)


def _fox_attention(qkv, f2, heads, dh, tq):
    bn, s, _ = qkv.shape
    tq = min(tq, s)
    return pl.pallas_call(
        functools.partial(_fox_attn_body, tq=tq),
        grid=(bn, heads, s // tq),
        in_specs=[pl.BlockSpec((1, tq, dh), lambda b, h, i: (b, i, h)),
                  pl.BlockSpec((1, s, dh), lambda b, h, i: (b, 0, heads + h)),
                  pl.BlockSpec((1, s, dh), lambda b, h, i: (b, 0, 2 * heads + h)),
                  pl.BlockSpec((1, 1, s, 1), lambda b, h, i: (b, h, 0, 0))],
        out_specs=pl.BlockSpec((1, tq, dh), lambda b, h, i: (b, i, h)),
        out_shape=jax.ShapeDtypeStruct((bn, s, heads * dh), BF16),
        scratch_shapes=[pltpu.VMEM((s, _LANES), BF16)],
        compiler_params=_params("parallel", "parallel", "arbitrary"),
        name="fox_attention",
    )(qkv, qkv, qkv, f2[..., None])


def _ssd_body(xbc_ref, z_ref, dt_ref, dtb_ref, alog_ref, dexp_ref, ng_ref, o_ref,
              state_ref, acst_ref, dtt_ref, wt_ref, *, groups, inner):
    L = SSD_CHUNK
    gw = inner // groups
    pairs = gw // _LANES
    hpg = gw // SSD_HEAD_DIM

    @pl.when(pl.program_id(1) == 0)
    def _():
        state_ref[...] = jnp.zeros_like(state_ref)

    dt = _softplus(dt_ref[0] + dtb_ref[...])
    acs = _cumsum_rows(dt * (-jnp.exp(alog_ref[...])))
    acs_t = acs.T
    dt_t = dt.T
    acst_ref[...] = acs_t
    dtt_ref[...] = dt_t
    wt_ref[...] = dt_t * jnp.exp(acs_t[:, L - 1:L] - acs_t)
    dec_last = jnp.exp(acs[L - 1:L, :])

    rq = lax.broadcasted_iota(jnp.int32, (L, L), 0)
    ck = lax.broadcasted_iota(jnp.int32, (L, L), 1)
    causal = ck <= rq
    lo = lax.broadcasted_iota(jnp.int32, (L, _LANES), 1) < SSD_HEAD_DIM
    lo_row = lo[:1]

    for g in range(groups):
        bmat = xbc_ref[0, :, inner + g * SSD_STATE:inner + (g + 1) * SSD_STATE]
        cmat = xbc_ref[0, :, inner + (groups + g) * SSD_STATE:inner + (groups + g + 1) * SSD_STATE]
        cbm = lax.dot_general(cmat, bmat, (((1,), (1,)), ((), ())), preferred_element_type=F32)
        cmat = cmat.astype(F32)
        b_t = bmat.astype(F32).T
        ys = []
        for j in range(pairs):
            col0 = g * gw + j * _LANES
            xs_p = xbc_ref[0, :, col0:col0 + _LANES].astype(F32)
            m_parts, c_parts, bw_parts, dl = [], [], [], []
            for e in range(2):
                hd = g * hpg + 2 * j + e
                col = jnp.broadcast_to(acs[:, hd:hd + 1], (L, L))
                row = acst_ref[hd:hd + 1, :]
                dec = jnp.exp(jnp.where(causal, col - row, -jnp.inf))
                m_parts.append((cbm * dec * dtt_ref[hd:hd + 1, :]).astype(BF16))
                c_parts.append((cmat * jnp.exp(col)).astype(BF16))
                bw_parts.append((b_t * wt_ref[hd:hd + 1, :]).astype(BF16))
                dl.append(dec_last[:, hd:hd + 1])
            prev = state_ref[:, col0:col0 + _LANES]
            x_top = jnp.where(lo, xs_p, 0.0).astype(BF16)
            x_bot = jnp.where(lo, 0.0, xs_p).astype(BF16)
            p_top = jnp.where(lo, prev, 0.0).astype(BF16)
            p_bot = jnp.where(lo, 0.0, prev).astype(BF16)
            lhs = jnp.concatenate(m_parts + c_parts, axis=1)
            rhs = jnp.concatenate([x_top, x_bot, p_top, p_bot], axis=0)
            y = jnp.dot(lhs, rhs, preferred_element_type=F32)
            st = jnp.dot(jnp.concatenate(bw_parts, axis=1), jnp.concatenate([x_top, x_bot], axis=0),
                         preferred_element_type=F32)
            state_ref[:, col0:col0 + _LANES] = prev * jnp.where(lo_row, dl[0], dl[1]) + st
            y = y + dexp_ref[:, col0:col0 + _LANES] * xs_p
            ys.append(y * z_ref[0, :, col0:col0 + _LANES].astype(F32))
        yg = jnp.concatenate(ys, axis=1)
        ms = jnp.mean(yg * yg, axis=1, keepdims=True)
        gcols = slice(g * gw, (g + 1) * gw)
        o_ref[0, :, gcols] = (yg * lax.rsqrt(ms + RMS_EPS) * ng_ref[:, gcols]).astype(o_ref.dtype)


def _ssd_core(xbc, zs, dt, dt_bias, a_log, d_skip, norm_g, groups):
    bn, s, conv_dim = xbc.shape
    inner = zs.shape[2]
    heads = a_log.shape[0]
    L = SSD_CHUNK
    pad = lambda v: jnp.pad(v, (0, _LANES - heads)).reshape(1, _LANES)
    const2 = lambda b, c: (0, 0)
    return pl.pallas_call(
        functools.partial(_ssd_body, groups=groups, inner=inner),
        grid=(bn, s // L),
        in_specs=[pl.BlockSpec((1, L, conv_dim), lambda b, c: (b, c, 0)),
                  pl.BlockSpec((1, L, inner), lambda b, c: (b, c, 0)),
                  pl.BlockSpec((1, L, _LANES), lambda b, c: (b, c, 0)),
                  pl.BlockSpec((1, _LANES), const2),
                  pl.BlockSpec((1, _LANES), const2),
                  pl.BlockSpec((1, inner), const2),
                  pl.BlockSpec((1, inner), const2)],
        out_specs=pl.BlockSpec((1, L, inner), lambda b, c: (b, c, 0)),
        out_shape=jax.ShapeDtypeStruct((bn, s, inner), BF16),
        scratch_shapes=[pltpu.VMEM((SSD_STATE, inner), F32),
                        pltpu.VMEM((_LANES, L), F32),
                        pltpu.VMEM((_LANES, L), F32),
                        pltpu.VMEM((_LANES, L), F32)],
        compiler_params=_params("parallel", "arbitrary"),
        name="ssd_core",
    )(xbc, zs, dt, pad(dt_bias), pad(a_log),
      jnp.repeat(d_skip, SSD_HEAD_DIM).reshape(1, inner), norm_g.reshape(1, inner))


def _router_body(x_ref, whl_ref, wh_ref, b_ref, oi_ref, of_ref, cnt_ref, base_ref):
    i = pl.program_id(0)

    @pl.when(i == 0)
    def _():
        base_ref[...] = jnp.zeros_like(base_ref)

    x = x_ref[...]
    tm = x.shape[0]
    xh = x.astype(BF16)
    xl = (x - xh.astype(F32)).astype(BF16)
    r = jnp.dot(xh, whl_ref[...], preferred_element_type=F32)
    logits = (r[:, :ROUTER_COLS] + r[:, ROUTER_COLS:]
              + jnp.dot(xl, wh_ref[...], preferred_element_type=F32) + b_ref[...])
    lt = logits.T

    row = lax.broadcasted_iota(jnp.int32, (_SUBLANES, tm), 0)
    neg = -jnp.inf
    l1 = jnp.where(row < MOE_GROUPS, lt[:_SUBLANES], neg)
    e1 = jnp.exp(l1 - jnp.max(l1, axis=0, keepdims=True))
    p1 = e1 / jnp.sum(e1, axis=0, keepdims=True)
    pg = jnp.max(p1, axis=0, keepdims=True)
    gidx = jnp.min(jnp.where(p1 == pg, row, _SUBLANES), axis=0, keepdims=True)

    sel = jnp.full((_SUBLANES, tm), neg, F32)
    for gi in range(MOE_GROUPS):
        r0 = ROUTER_L2_ROW0 + _SUBLANES * gi
        sel = jnp.where((gidx == gi) & (row < MOE_PER_GROUP), lt[r0:r0 + _SUBLANES], sel)
    va = jnp.max(sel, axis=0, keepdims=True)
    ia = jnp.min(jnp.where(sel == va, row, _SUBLANES), axis=0, keepdims=True)
    sel_b = jnp.where(row == ia, neg, sel)
    vb = jnp.max(sel_b, axis=0, keepdims=True)
    ib = jnp.min(jnp.where(sel_b == vb, row, _SUBLANES), axis=0, keepdims=True)
    t = jnp.exp(vb - va)
    ca = pg * (1.0 / (1.0 + t))
    cb = pg * (t / (1.0 + t))
    ea = gidx * MOE_PER_GROUP + ia
    eb = gidx * MOE_PER_GROUP + ib

    erow = lax.broadcasted_iota(jnp.int32, (N_EXPERTS, tm), 0)
    hit_a = erow == ea
    hit_b = erow == eb
    oh = jnp.where(hit_a, 1.0, 0.0) + jnp.where(hit_b, 1.0, 0.0)
    before = (lax.broadcasted_iota(jnp.int32, (tm, tm), 0)
              < lax.broadcasted_iota(jnp.int32, (tm, tm), 1))
    prefix = jnp.dot(oh.astype(BF16), jnp.where(before, 1.0, 0.0).astype(BF16),
                     preferred_element_type=F32) + base_ref[...]
    rank_a = jnp.sum(jnp.where(hit_a, prefix, 0.0), axis=0, keepdims=True)
    rank_b = jnp.sum(jnp.where(hit_b, prefix, 0.0), axis=0, keepdims=True)
    base_ref[...] += jnp.sum(oh, axis=1, keepdims=True)
    cnt_ref[...] = jnp.broadcast_to(base_ref[...], cnt_ref.shape)

    zi = jnp.zeros((1, tm), jnp.int32)
    oi_ref[...] = jnp.concatenate(
        [ea, eb, rank_a.astype(jnp.int32), rank_b.astype(jnp.int32), zi, zi, zi, zi], axis=0)
    zf = jnp.zeros((1, tm), F32)
    of_ref[...] = jnp.concatenate([ca, cb, zf, zf, zf, zf, zf, zf], axis=0)


def _router(x, w_r1, b_r1, w_r2, b_r2, tm):
    t, d = x.shape
    tm = min(tm, t)
    w = jnp.zeros((d, ROUTER_COLS), F32).at[:, :MOE_GROUPS].set(w_r1)
    b = jnp.zeros((ROUTER_COLS,), F32).at[:MOE_GROUPS].set(b_r1)
    for gi in range(MOE_GROUPS):
        c0 = ROUTER_L2_ROW0 + _SUBLANES * gi
        w = w.at[:, c0:c0 + MOE_PER_GROUP].set(w_r2[gi])
        b = b.at[c0:c0 + MOE_PER_GROUP].set(b_r2[gi])
    wh = w.astype(BF16)
    wl = (w - wh.astype(F32)).astype(BF16)
    const = lambda i: (0, 0)
    return pl.pallas_call(
        _router_body,
        grid=(t // tm,),
        in_specs=[pl.BlockSpec((tm, d), lambda i: (i, 0)),
                  pl.BlockSpec((d, 2 * ROUTER_COLS), const),
                  pl.BlockSpec((d, ROUTER_COLS), const),
                  pl.BlockSpec((1, ROUTER_COLS), const)],
        out_specs=[pl.BlockSpec((_SUBLANES, tm), lambda i: (0, i)),
                   pl.BlockSpec((_SUBLANES, tm), lambda i: (0, i)),
                   pl.BlockSpec((N_EXPERTS, _LANES), const)],
        out_shape=[jax.ShapeDtypeStruct((_SUBLANES, t), jnp.int32),
                   jax.ShapeDtypeStruct((_SUBLANES, t), F32),
                   jax.ShapeDtypeStruct((N_EXPERTS, _LANES), F32)],
        scratch_shapes=[pltpu.VMEM((N_EXPERTS, 1), F32)],
        compiler_params=_params("arbitrary"),
        name="moe_router",
    )(x, jnp.concatenate([wh, wl], axis=1), wh, b.reshape(1, ROUTER_COLS))


def _row_copy(src, src_row, dst, dst_row, sem):
    return pltpu.make_async_copy(src.at[pl.ds(src_row, 1)], dst.at[pl.ds(dst_row, 1)], sem)


def _dispatch_body(pos_ref, zt_ref, x_ref, xs_ref, zeros_ref, sem, zsem, *, td, tz, t_total):
    step = pl.program_id(0)
    base = step * td

    def zero_tile(i):
        return pltpu.make_async_copy(zeros_ref, xs_ref.at[pl.ds(pl.multiple_of(i * tz, tz), tz)], zsem)

    @pl.when(step == 0)
    def _():
        zeros_ref[...] = jnp.zeros_like(zeros_ref)

        def start(i, carry):
            @pl.when(zt_ref[i] == 1)
            def _():
                zero_tile(i).start()
            return carry

        def wait(i, carry):
            @pl.when(zt_ref[i] == 1)
            def _():
                zero_tile(i).wait()
            return carry

        lax.fori_loop(0, zt_ref.shape[0], start, 0)
        lax.fori_loop(0, zt_ref.shape[0], wait, 0)

    def issue(r, carry):
        _row_copy(x_ref, r, xs_ref, pos_ref[base + r], sem).start()
        _row_copy(x_ref, r, xs_ref, pos_ref[t_total + base + r], sem).start(priority=1)
        return carry

    lax.fori_loop(0, td, issue, 0, unroll=4)
    tile_rows = xs_ref.at[pl.ds(0, td)]
    pltpu.make_async_copy(x_ref, tile_rows, sem).wait()
    pltpu.make_async_copy(x_ref, tile_rows, sem).wait()


def _dispatch(x, pos, zero_tiles, n_rows, td, tz):
    t, d = x.shape
    td = min(td, t)
    return pl.pallas_call(
        functools.partial(_dispatch_body, td=td, tz=tz, t_total=t),
        grid_spec=pltpu.PrefetchScalarGridSpec(
            num_scalar_prefetch=2,
            grid=(t // td,),
            in_specs=[pl.BlockSpec((td, d), lambda i, pos, zt: (i, 0))],
            out_specs=pl.BlockSpec(memory_space=pl.ANY),
            scratch_shapes=[pltpu.VMEM((tz, d), x.dtype),
                            pltpu.SemaphoreType.DMA(()), pltpu.SemaphoreType.DMA(())]),
        out_shape=jax.ShapeDtypeStruct((n_rows, d), x.dtype),
        compiler_params=_params("arbitrary"),
        name="moe_dispatch",
    )(pos, zero_tiles, x)


def _expert_body(blk_ref, te_ref, nv_ref, xs_ref, wg_ref, wu_ref, wd_ref, o_ref):
    del blk_ref, te_ref
    valid = pl.program_id(0) < nv_ref[0]

    @pl.when(jnp.logical_not(valid))
    def _():
        o_ref[...] = jnp.zeros_like(o_ref)

    @pl.when(valid)
    def _():
        x = xs_ref[...].astype(BF16)
        g = jnp.dot(x, wg_ref[0], preferred_element_type=F32)
        u = jnp.dot(x, wu_ref[0], preferred_element_type=F32)
        hid = (g * _sigmoid(g)) * u
        o_ref[...] = jnp.dot(hid.astype(BF16), wd_ref[0], preferred_element_type=F32)


def _experts(xs, w_gate, w_up, w_down, blk, tile_expert, n_valid, tm):
    p, d = xs.shape
    ff = w_gate.shape[2]
    n_tiles = p // tm
    rows = lambda i, blk, te, nv: (blk[i], 0)
    wsel = lambda i, blk, te, nv: (te[i], 0, 0)
    return pl.pallas_call(
        _expert_body,
        grid_spec=pltpu.PrefetchScalarGridSpec(
            num_scalar_prefetch=3,
            grid=(n_tiles,),
            in_specs=[pl.BlockSpec((tm, d), rows),
                      pl.BlockSpec((1, d, ff), wsel),
                      pl.BlockSpec((1, d, ff), wsel),
                      pl.BlockSpec((1, ff, d), wsel)],
            out_specs=pl.BlockSpec((tm, d), lambda i, blk, te, nv: (i, 0))),
        out_shape=jax.ShapeDtypeStruct((p, d), F32),
        compiler_params=_params("arbitrary"),
        name="moe_experts",
    )(blk, tile_expert, n_valid, xs, w_gate, w_up, w_down)


def _combine_body(pos_ref, ys_ref, x_ref, c_ref, g_ref, b_ref, of_ref, ob_ref, buf_a, buf_b, sem,
                  *, tc, t_total, alpha):
    base = pl.program_id(0) * tc

    def issue(r, carry):
        _row_copy(ys_ref, pos_ref[base + r], buf_a, r, sem).start()
        _row_copy(ys_ref, pos_ref[t_total + base + r], buf_b, r, sem).start(priority=1)
        return carry

    lax.fori_loop(0, tc, issue, 0, unroll=4)
    tile_rows = ys_ref.at[pl.ds(0, tc)]
    pltpu.make_async_copy(tile_rows, buf_a, sem).wait()
    pltpu.make_async_copy(tile_rows, buf_b, sem).wait()
    c = c_ref[...]
    f = c[:, 0:1] * buf_a[...] + c[:, 1:2] * buf_b[...]
    y = _layer_norm(alpha * x_ref[...] + f, g_ref[...], b_ref[...])
    of_ref[...] = y
    ob_ref[...] = y.astype(BF16)


def _combine_ln(ys, pos, x, c, g, b, alpha, tc):
    t, d = x.shape
    tc = min(tc, t)
    const = lambda i, pos: (0, 0)
    tile = lambda i, pos: (i, 0)
    return pl.pallas_call(
        functools.partial(_combine_body, tc=tc, t_total=t, alpha=alpha),
        grid_spec=pltpu.PrefetchScalarGridSpec(
            num_scalar_prefetch=1,
            grid=(t // tc,),
            in_specs=[pl.BlockSpec(memory_space=pl.ANY),
                      pl.BlockSpec((tc, d), tile),
                      pl.BlockSpec((tc, 2), tile),
                      pl.BlockSpec((1, d), const),
                      pl.BlockSpec((1, d), const)],
            out_specs=[pl.BlockSpec((tc, d), tile), pl.BlockSpec((tc, d), tile)],
            scratch_shapes=[pltpu.VMEM((tc, d), F32), pltpu.VMEM((tc, d), F32),
                            pltpu.SemaphoreType.DMA(())]),
        out_shape=[jax.ShapeDtypeStruct((t, d), F32), jax.ShapeDtypeStruct((t, d), BF16)],
        compiler_params=_params("arbitrary"),
        name="moe_combine_ln",
    )(pos, ys, x, c, g.reshape(1, d), b.reshape(1, d))


def _moe_ln(x, w_r1, b_r1, w_r2, b_r2, w_gate, w_up, w_down, g, b, alpha, tiles):
    t, d = x.shape
    tm_e = min(tiles["expert_rows"], t)
    oi, of, cnt = _router(x, w_r1, b_r1, w_r2, b_r2, tiles["router"])
    counts = cnt[:, 0].astype(jnp.int32)
    seg_tiles = (counts + tm_e - 1) // tm_e
    tile_end = jnp.cumsum(seg_tiles)
    seg_start = (tile_end - seg_tiles) * tm_e
    start_of = lambda e: jnp.sum(jnp.where(e[:, None] == jnp.arange(N_EXPERTS, dtype=jnp.int32)[None, :],
                                           seg_start[None, :], 0), axis=1)
    pos = jnp.concatenate([start_of(oi[0]) + oi[2], start_of(oi[1]) + oi[3]])
    n_tiles = 2 * t // tm_e + N_EXPERTS
    n_valid = tile_end[-1:]
    tile_id = jnp.arange(n_tiles, dtype=jnp.int32)
    blk = jnp.minimum(tile_id, n_valid - 1)
    tile_expert = jnp.minimum(jnp.sum((blk[:, None] >= tile_end[None, :]).astype(jnp.int32), axis=1), N_EXPERTS - 1)
    zero_tiles = (jnp.any(tile_id[:, None] == (tile_end - 1)[None, :], axis=1) | (tile_id >= n_valid)).astype(jnp.int32)

    xs = _dispatch(x, pos, zero_tiles, n_tiles * tm_e, tiles["dispatch"], tm_e)
    ff = w_gate.shape[-1]
    ys = _experts(xs, w_gate.reshape(N_EXPERTS, d, ff).astype(BF16),
                  w_up.reshape(N_EXPERTS, d, ff).astype(BF16),
                  w_down.reshape(N_EXPERTS, ff, d).astype(BF16), blk, tile_expert, n_valid, tm_e)
    return _combine_ln(ys, pos, x, of[:2].T, g, b, alpha, tiles["combine"])


def _hybrid_mixer_ln(xf, xb, bn, w_in, conv_w, conv_b, w_r, b_r, w_i, b_i, lam, b_f, w_out, g, b, alpha, tiles):
    t, d = xf.shape
    s = t // bn
    heads, block, _ = w_r.shape
    lru_w = heads * block
    fox_heads = b_f.shape[0]
    main = w_in.shape[1] - fox_heads
    fox_w = (main - 2 * lru_w) // 3
    dh = fox_w // fox_heads
    tm, tn = tiles["mm_m"], tiles["mm_n"]
    xc = _proj(xb, w_in[:, :lru_w].astype(BF16), BF16, tm, tiles["conv_n"], "hy_lru_x_proj_conv",
               "conv", (conv_w, conv_b), s)
    gg = _proj(xb, w_in[:, lru_w:2 * lru_w].astype(BF16), BF16, tm, tn, "hy_lru_gate_proj_gelu", "gelu")
    w_qkv = jnp.concatenate([w_in[:, 2 * lru_w:2 * lru_w + fox_w] * (dh ** -0.5 * LOG2E),
                             w_in[:, 2 * lru_w + fox_w:main]], axis=1).astype(BF16)
    qkv = _proj(xb, w_qkv, BF16, tm, tn, "hy_qkv_proj")
    w_fl = jnp.pad(w_in[:, main:], ((0, 0), (0, _LANES - fox_heads))).astype(BF16)
    fl = _proj(xb, w_fl, F32, tm, _LANES, "hy_forget_proj")[:, :fox_heads]
    f2 = _fox_prep(fl.reshape(bn, s, fox_heads).transpose(0, 2, 1), b_f)
    y_lru = _lru(xc.reshape(bn, s, lru_w), gg.reshape(bn, s, lru_w), w_r, b_r, w_i, b_i, lam, tiles["lru"])
    y_att = _fox_attention(qkv.reshape(bn, s, 3 * fox_w), f2, fox_heads, dh, tiles["attn"])
    y = jnp.concatenate([y_lru, y_att], axis=-1).reshape(t, lru_w + fox_w)
    return _matmul_residual_ln(y, w_out.astype(BF16), xf, g, b, alpha, tiles["ln_m"], tiles["ln_k"], "hy_out_proj_ln")


def _ssd_mixer_ln(xf, xb, bn, w_in, conv_w, conv_b, dt_bias, a_log, d_skip, norm_g, w_out, g, b, alpha, tiles):
    t, d = xf.shape
    s = t // bn
    heads = a_log.shape[0]
    inner = norm_g.shape[0]
    conv_dim = conv_w.shape[1]
    groups = (conv_dim - inner) // (2 * SSD_STATE)
    tm, tn = tiles["mm_m"], tiles["mm_n"]
    zs = _proj(xb, w_in[:, :inner].astype(BF16), BF16, tm, tn, "ssd_z_proj_silu", "silu")
    xbc = _proj(xb, w_in[:, inner:inner + conv_dim].astype(BF16), BF16, tm, tiles["conv_n"],
                "ssd_xbc_proj_conv_silu", "conv_silu", (conv_w, conv_b), s)
    w_dt = jnp.pad(w_in[:, inner + conv_dim:], ((0, 0), (0, _LANES - heads))).astype(BF16)
    dt = _proj(xb, w_dt, F32, tm, _LANES, "ssd_dt_proj")
    y = _ssd_core(xbc.reshape(bn, s, conv_dim), zs.reshape(bn, s, inner), dt.reshape(bn, s, _LANES),
                  dt_bias, a_log, d_skip, norm_g, groups)
    return _matmul_residual_ln(y.reshape(t, inner), w_out.astype(BF16), xf, g, b, alpha,
                               tiles["ln_m"], tiles["ln_k"], "ssd_out_proj_ln")


_TILES = dict(mm_m=1024, mm_n=1024, conv_n=512, ln_m=512, ln_k=2048, lru=256, attn=512,
              router=512, dispatch=512, expert_rows=512, combine=256)


def kernel(x, ln_g, ln_b, hy_w_in, hy_conv_w, hy_conv_b, hy_w_r, hy_b_r, hy_w_i, hy_b_i, hy_lambda, hy_b_f, hy_w_out, ssd_w_in, ssd_conv_w, ssd_conv_b, ssd_dt_bias, ssd_a_log, ssd_d, ssd_norm_g, ssd_w_out, moe_w_r1, moe_b_r1, moe_w_r2, moe_b_r2, moe_w_gate, moe_w_up, moe_w_down):
    bn, s, d = x.shape
    depth = ln_g.shape[0]
    alpha = (2 * depth) ** 0.25
    xf = x.reshape(bn * s, d)
    xb = xf.astype(BF16)
    for layer in range(depth):
        j = layer // 2
        if layer % 2 == 0:
            xf, xb = _hybrid_mixer_ln(xf, xb, bn, hy_w_in[j], hy_conv_w[j], hy_conv_b[j], hy_w_r[j], hy_b_r[j],
                                      hy_w_i[j], hy_b_i[j], hy_lambda[j], hy_b_f[j], hy_w_out[j],
                                      ln_g[layer, 0], ln_b[layer, 0], alpha, _TILES)
        else:
            xf, xb = _ssd_mixer_ln(xf, xb, bn, ssd_w_in[j], ssd_conv_w[j], ssd_conv_b[j], ssd_dt_bias[j],
                                   ssd_a_log[j], ssd_d[j], ssd_norm_g[j], ssd_w_out[j],
                                   ln_g[layer, 0], ln_b[layer, 0], alpha, _TILES)
        xf, xb = _moe_ln(xf, moe_w_r1[layer], moe_b_r1[layer], moe_w_r2[layer], moe_b_r2[layer],
                         moe_w_gate[layer], moe_w_up[layer], moe_w_down[layer],
                         ln_g[layer, 1], ln_b[layer, 1], alpha, _TILES)
    return xf.reshape(bn, s, d)
```

```python
import functools
import math

import jax
import jax.numpy as jnp
from jax import lax
from jax.experimental import pallas as pl
from jax.experimental.pallas import tpu as pltpu

F32 = jnp.float32
BF16 = jnp.bfloat16

_VMEM_LIMIT_BYTES = 56 * 1024 * 1024
_LANES = 128
_SUBLANES = 8

LN_EPS = 1e-5
RMS_EPS = 1e-5
LRU_C = 8.0
CONV_K = 4
SSD_CHUNK = 128
SSD_HEAD_DIM = 64
SSD_STATE = 128
MOE_GROUPS = 4
MOE_PER_GROUP = 4
N_EXPERTS = MOE_GROUPS * MOE_PER_GROUP
ROUTER_COLS = 128
ROUTER_L2_ROW0 = 8
LOG2E = math.log2(math.e)


def _params(*sem):
    return pltpu.CompilerParams(dimension_semantics=sem, vmem_limit_bytes=_VMEM_LIMIT_BYTES)


def _sigmoid(x):
    return 1.0 / (1.0 + jnp.exp(-x))


def _softplus(x):
    e = jnp.exp(-jnp.abs(x))
    u = 1.0 + e
    return jnp.maximum(x, 0.0) + jnp.where(u == 1.0, e, jnp.log(u) * (e / (u - 1.0)))


def _shift_rows(x, tail, d):
    r = pltpu.roll(x, d, axis=0)
    t = pltpu.roll(tail, d, axis=0)
    row = lax.broadcasted_iota(jnp.int32, tail.shape, 0)
    first = jnp.where(row < d, t, r[:_SUBLANES])
    return jnp.concatenate([first, r[_SUBLANES:]], axis=0)


def _causal_conv(x, tail, w_ref, b_ref):
    acc = b_ref[...] + w_ref[CONV_K - 1:CONV_K, :] * x
    for d in range(1, CONV_K):
        acc = acc + w_ref[CONV_K - 1 - d:CONV_K - d, :] * _shift_rows(x, tail, d)
    return acc


def _cumsum_rows(x):
    n = x.shape[0]
    row = lax.broadcasted_iota(jnp.int32, x.shape, 0)
    d = 1
    while d < n:
        x = x + jnp.where(row >= d, pltpu.roll(x, d, axis=0), 0.0)
        d *= 2
    return x


def _cumsum_lanes(x):
    n = x.shape[1]
    lane = lax.broadcasted_iota(jnp.int32, x.shape, 1)
    d = 1
    while d < n:
        x = x + jnp.where(lane >= d, pltpu.roll(x, d, axis=1), 0.0)
        d *= 2
    return x


def _layer_norm(s, g, b):
    mu = jnp.mean(s, axis=-1, keepdims=True)
    d = s - mu
    var = jnp.mean(d * d, axis=-1, keepdims=True)
    return d * lax.rsqrt(var + LN_EPS) * g + b


def _proj_body(x_ref, w_ref, *rest, epilogue, tiles_per_seq):
    if epilogue.startswith("conv"):
        cw_ref, cb_ref, o_ref, carry_ref = rest
    else:
        (o_ref,) = rest
    acc = jnp.dot(x_ref[...], w_ref[...], preferred_element_type=F32)
    if epilogue.startswith("conv"):
        @pl.when(pl.program_id(1) % tiles_per_seq == 0)
        def _():
            carry_ref[...] = jnp.zeros_like(carry_ref)

        tail = carry_ref[...]
        carry_ref[...] = acc[acc.shape[0] - _SUBLANES:, :]
        acc = _causal_conv(acc, tail, cw_ref, cb_ref)
    if epilogue.endswith("silu"):
        acc = acc * _sigmoid(acc)
    elif epilogue == "gelu":
        acc = jax.nn.gelu(acc)
    o_ref[...] = acc.astype(o_ref.dtype)


def _proj(x, w, out_dtype, tm, tn, name, epilogue="none", conv=None, seq_len=None):
    m, k = x.shape
    n = w.shape[1]
    tm, tn = min(tm, m), min(tn, n)
    in_specs = [pl.BlockSpec((tm, k), lambda j, i: (i, 0)),
                pl.BlockSpec((k, tn), lambda j, i: (0, j))]
    args = [x, w]
    scratch = []
    tiles_per_seq = 1
    if conv is not None:
        assert seq_len % tm == 0
        tiles_per_seq = seq_len // tm
        in_specs += [pl.BlockSpec((CONV_K, tn), lambda j, i: (0, j)),
                     pl.BlockSpec((1, tn), lambda j, i: (0, j))]
        args += [conv[0], conv[1].reshape(1, n)]
        scratch = [pltpu.VMEM((_SUBLANES, tn), F32)]
    return pl.pallas_call(
        functools.partial(_proj_body, epilogue=epilogue, tiles_per_seq=tiles_per_seq),
        grid=(n // tn, m // tm),
        in_specs=in_specs,
        out_specs=pl.BlockSpec((tm, tn), lambda j, i: (i, j)),
        out_shape=jax.ShapeDtypeStruct((m, n), out_dtype),
        scratch_shapes=scratch,
        compiler_params=_params("parallel", "arbitrary"),
        name=name,
    )(*args)


def _proj_w32_body(x_ref, w_ref, *rest, epilogue, tiles_per_seq, chunk, scale_cols, scale):
    conv = epilogue.startswith("conv")
    if conv:
        cw_ref, cb_ref, o_ref, wb_ref, carry_ref = rest
    else:
        o_ref, wb_ref = rest
    j = pl.program_id(0)
    t = pl.program_id(1)
    tm = x_ref.shape[0]

    @pl.when(t == 0)
    def _():
        wb_ref[...] = w_ref[...].astype(BF16)

    if conv:
        @pl.when(t % tiles_per_seq == 0)
        def _():
            carry_ref[...] = jnp.zeros_like(carry_ref)

    x = x_ref[...]
    for c0 in range(0, o_ref.shape[1], chunk):
        cols = slice(c0, c0 + chunk)
        acc = jnp.dot(x, wb_ref[:, cols], preferred_element_type=F32)
        if conv:
            tail = carry_ref[:, cols]
            carry_ref[:, cols] = acc[tm - _SUBLANES:, :]
            acc = _causal_conv(acc, tail, cw_ref.at[:, cols], cb_ref.at[:, cols])
        if epilogue.endswith("silu"):
            acc = acc * _sigmoid(acc)
        elif epilogue == "gelu":
            acc = jax.nn.gelu(acc)
        elif epilogue == "scale":
            acc = acc * jnp.where(j < scale_cols, scale, 1.0)
        o_ref[:, cols] = acc.astype(o_ref.dtype)


def _proj_w32(x, w_all, layer, col0, n, out_dtype, tm, tn, name, epilogue="none", conv=None, seq_len=None,
              scale_cols=0, scale=1.0, chunk=256):
    m, k = x.shape
    tm, tn = min(tm, m), min(tn, n)
    assert col0 % tn == 0 and n % tn == 0 and m % tm == 0 and tn % chunk == 0
    jb = col0 // tn
    in_specs = [pl.BlockSpec((tm, k), lambda j, t: (t, 0)),
                pl.BlockSpec((None, k, tn), lambda j, t: (layer, 0, jb + j))]
    args = [x, w_all]
    scratch = [pltpu.VMEM((k, tn), BF16)]
    tiles_per_seq = 1
    if conv is not None:
        assert seq_len % tm == 0
        tiles_per_seq = seq_len // tm
        in_specs += [pl.BlockSpec((CONV_K, tn), lambda j, t: (0, j)),
                     pl.BlockSpec((1, tn), lambda j, t: (0, j))]
        args += [conv[0], conv[1].reshape(1, n)]
        scratch.append(pltpu.VMEM((_SUBLANES, tn), F32))
    return pl.pallas_call(
        functools.partial(_proj_w32_body, epilogue=epilogue, tiles_per_seq=tiles_per_seq, chunk=chunk,
                          scale_cols=scale_cols, scale=scale),
        grid=(n // tn, m // tm),
        in_specs=in_specs,
        out_specs=pl.BlockSpec((tm, tn), lambda j, t: (t, j)),
        out_shape=jax.ShapeDtypeStruct((m, n), out_dtype),
        scratch_shapes=scratch,
        compiler_params=_params("arbitrary", "arbitrary"),
        name=name,
    )(*args)


def _mm_ln_body(*refs, alpha, n_a):
    a_refs = refs[:n_a]
    w_ref, x_ref, g_ref, b_ref, of_ref, ob_ref, acc_ref = refs[n_a:]
    k = pl.program_id(1)
    ka = a_refs[0].shape[1]
    part = jnp.dot(a_refs[0][...], w_ref[:ka, :], preferred_element_type=F32)
    for idx in range(1, n_a):
        part = part + jnp.dot(a_refs[idx][...], w_ref[idx * ka:(idx + 1) * ka, :], preferred_element_type=F32)

    @pl.when(k == 0)
    def _():
        acc_ref[...] = part

    @pl.when(k > 0)
    def _():
        acc_ref[...] += part

    @pl.when(k == pl.num_programs(1) - 1)
    def _():
        y = _layer_norm(alpha * x_ref[...] + acc_ref[...], g_ref[...], b_ref[...])
        of_ref[...] = y
        ob_ref[...] = y.astype(BF16)


def _matmul_residual_ln(a_list, w_all, layer, x, g, b, alpha, tm, tk, name):
    n_a = len(a_list)
    m, ka = a_list[0].shape
    n = w_all.shape[2]
    tm = min(tm, m)
    tka = ka if n_a > 1 else min(tk, ka)
    nk = ka // tka
    return pl.pallas_call(
        functools.partial(_mm_ln_body, alpha=alpha, n_a=n_a),
        grid=(m // tm, nk),
        in_specs=[pl.BlockSpec((tm, tka), lambda i, k: (i, k))] * n_a + [
                  pl.BlockSpec((None, n_a * tka, n), lambda i, k: (layer, k, 0)),
                  pl.BlockSpec((tm, n), lambda i, k: (i, 0)),
                  pl.BlockSpec((1, n), lambda i, k: (0, 0)),
                  pl.BlockSpec((1, n), lambda i, k: (0, 0))],
        out_specs=[pl.BlockSpec((tm, n), lambda i, k: (i, 0)),
                   pl.BlockSpec((tm, n), lambda i, k: (i, 0))],
        out_shape=[jax.ShapeDtypeStruct((m, n), F32), jax.ShapeDtypeStruct((m, n), BF16)],
        scratch_shapes=[pltpu.VMEM((tm, n), F32)],
        compiler_params=_params("parallel", "arbitrary"),
        name=name,
    )(*a_list, w_all, x, g.reshape(1, n), b.reshape(1, n))


def _fox_prep_body(fl_ref, bf_ref, f_ref):
    x = fl_ref[0] + bf_ref[...]
    f_ref[0] = _cumsum_lanes(-_softplus(-x)) * LOG2E


def _fox_prep(fl_t, b_f):
    bn, h, s = fl_t.shape
    return pl.pallas_call(
        _fox_prep_body,
        grid=(bn,),
        in_specs=[pl.BlockSpec((1, h, s), lambda b: (b, 0, 0)),
                  pl.BlockSpec((h, 1), lambda b: (0, 0))],
        out_specs=pl.BlockSpec((1, h, s), lambda b: (b, 0, 0)),
        out_shape=jax.ShapeDtypeStruct((bn, h, s), F32),
        compiler_params=_params("parallel"),
        name="fox_prep",
    )(fl_t, b_f.reshape(h, 1))


def _lru_body(xc_ref, gg_ref, wr_ref, br_ref, wi_ref, bi_ref, lam_ref, y_ref, h_ref, *, heads, block):
    @pl.when(pl.program_id(1) == 0)
    def _():
        h_ref[...] = jnp.zeros_like(h_ref)

    xcb = xc_ref[0]
    xc = xcb.astype(F32)
    ts = xc.shape[0]
    r_parts, i_parts = [], []
    for hd in range(heads):
        xh = xcb[:, hd * block:(hd + 1) * block]
        r_parts.append(jnp.dot(xh, wr_ref[hd], preferred_element_type=F32))
        i_parts.append(jnp.dot(xh, wi_ref[hd], preferred_element_type=F32))
    r = _sigmoid(jnp.concatenate(r_parts, axis=1) + br_ref[...])
    ig = _sigmoid(jnp.concatenate(i_parts, axis=1) + bi_ref[...])
    log_a = -LRU_C * r * _softplus(-lam_ref[...])
    a = jnp.exp(log_a)
    u = jnp.sqrt(-jnp.tanh(log_a) * (a * a + 1.0)) * (ig * xc)

    row = lax.broadcasted_iota(jnp.int32, a.shape, 0)
    d = 1
    while d < ts:
        keep = row >= d
        u = u + a * jnp.where(keep, pltpu.roll(u, d, axis=0), 0.0)
        a = a * jnp.where(keep, pltpu.roll(a, d, axis=0), 1.0)
        d *= 2
    h = u + a * h_ref[...]
    h_ref[...] = h[ts - 1:ts, :]
    y_ref[0] = (h * gg_ref[0].astype(F32)).astype(y_ref.dtype)


def _lru(xc, gg, w_r, b_r, w_i, b_i, lam, ts):
    bn, s, width = xc.shape
    heads, block, _ = w_r.shape
    ts = min(ts, s)
    vec = lambda v: v.reshape(1, width)
    const2 = lambda b, i: (0, 0)
    const3 = lambda b, i: (0, 0, 0)
    tile = pl.BlockSpec((1, ts, width), lambda b, i: (b, i, 0))
    return pl.pallas_call(
        functools.partial(_lru_body, heads=heads, block=block),
        grid=(bn, s // ts),
        in_specs=[tile, tile,
                  pl.BlockSpec((heads, block, block), const3),
                  pl.BlockSpec((1, width), const2),
                  pl.BlockSpec((heads, block, block), const3),
                  pl.BlockSpec((1, width), const2),
                  pl.BlockSpec((1, width), const2)],
        out_specs=tile,
        out_shape=jax.ShapeDtypeStruct((bn, s, width), BF16),
        scratch_shapes=[pltpu.VMEM((1, width), F32)],
        compiler_params=_params("parallel", "arbitrary"),
        name="rg_lru",
    )(xc, gg, w_r.astype(BF16), vec(b_r), w_i.astype(BF16), vec(b_i), vec(lam))


def _split3(f):
    hi = f.astype(BF16).astype(F32)
    r = f - hi
    mid = r.astype(BF16).astype(F32)
    return hi, mid, r - mid


def _fox_attn_body(q_ref, k_ref, v_ref, fc_ref, o_ref, kx_ref, *, tq):
    qi = pl.program_id(2)
    s_len = k_ref.shape[1]
    dh = q_ref.shape[2]

    @pl.when(qi == 0)
    def _():
        for c0 in range(0, s_len, tq):
            hi, mid, lo = _split3(fc_ref[0, 0, c0:c0 + tq, :])
            lane = lax.broadcasted_iota(jnp.int32, (tq, _LANES), 1)
            kx = jnp.where(lane < 3, 1.0, jnp.where(lane == 3, -hi, jnp.where(lane == 4, -mid,
                           jnp.where(lane == 5, -lo, 0.0))))
            kx_ref[c0:c0 + tq, :] = kx.astype(BF16)

    hi, mid, lo = _split3(fc_ref[0, 0, pl.ds(pl.multiple_of(qi * tq, tq), tq), :])
    lane = lax.broadcasted_iota(jnp.int32, (tq, _LANES), 1)
    qx = jnp.where(lane == 0, hi, jnp.where(lane == 1, mid, jnp.where(lane == 2, lo,
                   jnp.where(lane < 6, 1.0, 0.0))))
    q = jnp.concatenate([q_ref[0], qx.astype(BF16)], axis=1)

    def scores(j):
        start = pl.multiple_of(j * tq, tq)
        k = jnp.concatenate([k_ref[0, pl.ds(start, tq), :], kx_ref[pl.ds(start, tq), :]], axis=1)
        return lax.dot_general(q, k, (((1,), (1,)), ((), ())), preferred_element_type=F32), start

    def update(carry, sc, start):
        m, l, acc = carry
        m_new = jnp.maximum(m, jnp.max(sc, axis=1, keepdims=True))
        corr = jnp.exp2(m - m_new)
        p = jnp.exp2(sc - m_new)
        l = corr * l + jnp.sum(p, axis=1, keepdims=True)
        v = v_ref[0, pl.ds(start, tq), :]
        acc = corr * acc + jnp.dot(p.astype(BF16), v, preferred_element_type=F32)
        return m_new, l, acc

    def diagonal(carry):
        sc, start = scores(qi)
        rq = lax.broadcasted_iota(jnp.int32, sc.shape, 0)
        ck = lax.broadcasted_iota(jnp.int32, sc.shape, 1)
        return update(carry, jnp.where(ck <= rq, sc, -1e30), start)

    def two_blocks(jj, carry):
        sc_a, start_a = scores(2 * jj)
        sc_b, start_b = scores(2 * jj + 1)
        return update(update(carry, sc_a, start_a), sc_b, start_b)

    def odd_tail(carry):
        sc_a, start_a = scores(qi - 1)
        return diagonal(update(carry, sc_a, start_a))

    init = (jnp.full((tq, 1), -1e30, F32), jnp.zeros((tq, 1), F32), jnp.zeros((tq, dh), F32))
    carry = lax.fori_loop(0, qi // 2, two_blocks, init)
    _, l, acc = lax.cond(qi % 2 == 1, odd_tail, diagonal, carry)
    o_ref[0] = (acc / l).astype(o_ref.dtype)


def _fox_attention(qkv, f2, heads, dh, tq):
    bn, s, _ = qkv.shape
    tq = min(tq, s)
    return pl.pallas_call(
        functools.partial(_fox_attn_body, tq=tq),
        grid=(bn, heads, s // tq),
        in_specs=[pl.BlockSpec((1, tq, dh), lambda b, h, i: (b, i, h)),
                  pl.BlockSpec((1, s, dh), lambda b, h, i: (b, 0, heads + h)),
                  pl.BlockSpec((1, s, dh), lambda b, h, i: (b, 0, 2 * heads + h)),
                  pl.BlockSpec((1, 1, s, 1), lambda b, h, i: (b, h, 0, 0))],
        out_specs=pl.BlockSpec((1, tq, dh), lambda b, h, i: (b, i, h)),
        out_shape=jax.ShapeDtypeStruct((bn, s, heads * dh), BF16),
        scratch_shapes=[pltpu.VMEM((s, _LANES), BF16)],
        compiler_params=_params("parallel", "parallel", "arbitrary"),
        name="fox_attention",
    )(qkv, qkv, qkv, f2[..., None])


def _ssd_body(xbc_ref, z_ref, dt_ref, dtb_ref, alog_ref, dexp_ref, ng_ref, o_ref,
              state_ref, acst_ref, dtt_ref, wt_ref, *, groups, inner):
    L = SSD_CHUNK
    gw = inner // groups
    pairs = gw // _LANES
    hpg = gw // SSD_HEAD_DIM

    @pl.when(pl.program_id(1) == 0)
    def _():
        state_ref[...] = jnp.zeros_like(state_ref)

    dt = _softplus(dt_ref[0] + dtb_ref[...])
    acs = _cumsum_rows(dt * (-jnp.exp(alog_ref[...])))
    acs_t = acs.T
    dt_t = dt.T
    acst_ref[...] = acs_t
    dtt_ref[...] = dt_t
    wt_ref[...] = dt_t * jnp.exp(acs_t[:, L - 1:L] - acs_t)
    dec_last = jnp.exp(acs[L - 1:L, :])

    rq = lax.broadcasted_iota(jnp.int32, (L, L), 0)
    ck = lax.broadcasted_iota(jnp.int32, (L, L), 1)
    causal = ck <= rq
    lo = lax.broadcasted_iota(jnp.int32, (L, _LANES), 1) < SSD_HEAD_DIM
    lo_row = lo[:1]

    for g in range(groups):
        bmat = xbc_ref[0, :, inner + g * SSD_STATE:inner + (g + 1) * SSD_STATE]
        cmat = xbc_ref[0, :, inner + (groups + g) * SSD_STATE:inner + (groups + g + 1) * SSD_STATE]
        cbm = lax.dot_general(cmat, bmat, (((1,), (1,)), ((), ())), preferred_element_type=F32)
        cmat = cmat.astype(F32)
        b_t = bmat.astype(F32).T
        ys = []
        for j in range(pairs):
            col0 = g * gw + j * _LANES
            xs_p = xbc_ref[0, :, col0:col0 + _LANES].astype(F32)
            m_parts, c_parts, bw_parts, dl = [], [], [], []
            for e in range(2):
                hd = g * hpg + 2 * j + e
                col = jnp.broadcast_to(acs[:, hd:hd + 1], (L, L))
                row = acst_ref[hd:hd + 1, :]
                dec = jnp.exp(jnp.where(causal, col - row, -jnp.inf))
                m_parts.append((cbm * dec * dtt_ref[hd:hd + 1, :]).astype(BF16))
                c_parts.append((cmat * jnp.exp(col)).astype(BF16))
                bw_parts.append((b_t * wt_ref[hd:hd + 1, :]).astype(BF16))
                dl.append(dec_last[:, hd:hd + 1])
            prev = state_ref[:, col0:col0 + _LANES]
            x_top = jnp.where(lo, xs_p, 0.0).astype(BF16)
            x_bot = jnp.where(lo, 0.0, xs_p).astype(BF16)
            p_top = jnp.where(lo, prev, 0.0).astype(BF16)
            p_bot = jnp.where(lo, 0.0, prev).astype(BF16)
            lhs = jnp.concatenate(m_parts + c_parts, axis=1)
            rhs = jnp.concatenate([x_top, x_bot, p_top, p_bot], axis=0)
            y = jnp.dot(lhs, rhs, preferred_element_type=F32)
            st = jnp.dot(jnp.concatenate(bw_parts, axis=1), jnp.concatenate([x_top, x_bot], axis=0),
                         preferred_element_type=F32)
            state_ref[:, col0:col0 + _LANES] = prev * jnp.where(lo_row, dl[0], dl[1]) + st
            y = y + dexp_ref[:, col0:col0 + _LANES] * xs_p
            ys.append(y * z_ref[0, :, col0:col0 + _LANES].astype(F32))
        yg = jnp.concatenate(ys, axis=1)
        ms = jnp.mean(yg * yg, axis=1, keepdims=True)
        gcols = slice(g * gw, (g + 1) * gw)
        o_ref[0, :, gcols] = (yg * lax.rsqrt(ms + RMS_EPS) * ng_ref[:, gcols]).astype(o_ref.dtype)


def _ssd_core(xbc, zs, dt, dt_bias, a_log, d_skip, norm_g, groups):
    bn, s, conv_dim = xbc.shape
    inner = zs.shape[2]
    heads = a_log.shape[0]
    L = SSD_CHUNK
    pad = lambda v: jnp.pad(v, (0, _LANES - heads)).reshape(1, _LANES)
    const2 = lambda b, c: (0, 0)
    return pl.pallas_call(
        functools.partial(_ssd_body, groups=groups, inner=inner),
        grid=(bn, s // L),
        in_specs=[pl.BlockSpec((1, L, conv_dim), lambda b, c: (b, c, 0)),
                  pl.BlockSpec((1, L, inner), lambda b, c: (b, c, 0)),
                  pl.BlockSpec((1, L, _LANES), lambda b, c: (b, c, 0)),
                  pl.BlockSpec((1, _LANES), const2),
                  pl.BlockSpec((1, _LANES), const2),
                  pl.BlockSpec((1, inner), const2),
                  pl.BlockSpec((1, inner), const2)],
        out_specs=pl.BlockSpec((1, L, inner), lambda b, c: (b, c, 0)),
        out_shape=jax.ShapeDtypeStruct((bn, s, inner), BF16),
        scratch_shapes=[pltpu.VMEM((SSD_STATE, inner), F32),
                        pltpu.VMEM((_LANES, L), F32),
                        pltpu.VMEM((_LANES, L), F32),
                        pltpu.VMEM((_LANES, L), F32)],
        compiler_params=_params("parallel", "arbitrary"),
        name="ssd_core",
    )(xbc, zs, dt, pad(dt_bias), pad(a_log),
      jnp.repeat(d_skip, SSD_HEAD_DIM).reshape(1, inner), norm_g.reshape(1, inner))


def _router_body(x_ref, whl_ref, wh_ref, b_ref, oi_ref, of_ref, cnt_ref, base_ref):
    i = pl.program_id(0)

    @pl.when(i == 0)
    def _():
        base_ref[...] = jnp.zeros_like(base_ref)

    x = x_ref[...]
    tm = x.shape[0]
    xh = x.astype(BF16)
    xl = (x - xh.astype(F32)).astype(BF16)
    r = jnp.dot(xh, whl_ref[...], preferred_element_type=F32)
    logits = (r[:, :ROUTER_COLS] + r[:, ROUTER_COLS:]
              + jnp.dot(xl, wh_ref[...], preferred_element_type=F32) + b_ref[...])
    lt = logits.T

    row = lax.broadcasted_iota(jnp.int32, (_SUBLANES, tm), 0)
    neg = -jnp.inf
    l1 = jnp.where(row < MOE_GROUPS, lt[:_SUBLANES], neg)
    e1 = jnp.exp(l1 - jnp.max(l1, axis=0, keepdims=True))
    p1 = e1 / jnp.sum(e1, axis=0, keepdims=True)
    pg = jnp.max(p1, axis=0, keepdims=True)
    gidx = jnp.min(jnp.where(p1 == pg, row, _SUBLANES), axis=0, keepdims=True)

    sel = jnp.full((_SUBLANES, tm), neg, F32)
    for gi in range(MOE_GROUPS):
        r0 = ROUTER_L2_ROW0 + _SUBLANES * gi
        sel = jnp.where((gidx == gi) & (row < MOE_PER_GROUP), lt[r0:r0 + _SUBLANES], sel)
    va = jnp.max(sel, axis=0, keepdims=True)
    ia = jnp.min(jnp.where(sel == va, row, _SUBLANES), axis=0, keepdims=True)
    sel_b = jnp.where(row == ia, neg, sel)
    vb = jnp.max(sel_b, axis=0, keepdims=True)
    ib = jnp.min(jnp.where(sel_b == vb, row, _SUBLANES), axis=0, keepdims=True)
    t = jnp.exp(vb - va)
    ca = pg * (1.0 / (1.0 + t))
    cb = pg * (t / (1.0 + t))
    ea = gidx * MOE_PER_GROUP + ia
    eb = gidx * MOE_PER_GROUP + ib

    erow = lax.broadcasted_iota(jnp.int32, (N_EXPERTS, tm), 0)
    hit_a = erow == ea
    hit_b = erow == eb
    oh = jnp.where(hit_a, 1.0, 0.0) + jnp.where(hit_b, 1.0, 0.0)
    before = (lax.broadcasted_iota(jnp.int32, (tm, tm), 0)
              < lax.broadcasted_iota(jnp.int32, (tm, tm), 1))
    prefix = jnp.dot(oh.astype(BF16), jnp.where(before, 1.0, 0.0).astype(BF16),
                     preferred_element_type=F32) + base_ref[...]
    rank_a = jnp.sum(jnp.where(hit_a, prefix, 0.0), axis=0, keepdims=True)
    rank_b = jnp.sum(jnp.where(hit_b, prefix, 0.0), axis=0, keepdims=True)
    base_ref[...] += jnp.sum(oh, axis=1, keepdims=True)
    cnt_ref[...] = jnp.broadcast_to(base_ref[...], cnt_ref.shape)

    zi = jnp.zeros((1, tm), jnp.int32)
    oi_ref[...] = jnp.concatenate(
        [ea, eb, rank_a.astype(jnp.int32), rank_b.astype(jnp.int32), zi, zi, zi, zi], axis=0)
    zf = jnp.zeros((1, tm), F32)
    of_ref[...] = jnp.concatenate([ca, cb, zf, zf, zf, zf, zf, zf], axis=0)


def _router(x, w_r1, b_r1, w_r2, b_r2, tm):
    t, d = x.shape
    tm = min(tm, t)
    w = jnp.zeros((d, ROUTER_COLS), F32).at[:, :MOE_GROUPS].set(w_r1)
    b = jnp.zeros((ROUTER_COLS,), F32).at[:MOE_GROUPS].set(b_r1)
    for gi in range(MOE_GROUPS):
        c0 = ROUTER_L2_ROW0 + _SUBLANES * gi
        w = w.at[:, c0:c0 + MOE_PER_GROUP].set(w_r2[gi])
        b = b.at[c0:c0 + MOE_PER_GROUP].set(b_r2[gi])
    wh = w.astype(BF16)
    wl = (w - wh.astype(F32)).astype(BF16)
    const = lambda i: (0, 0)
    return pl.pallas_call(
        _router_body,
        grid=(t // tm,),
        in_specs=[pl.BlockSpec((tm, d), lambda i: (i, 0)),
                  pl.BlockSpec((d, 2 * ROUTER_COLS), const),
                  pl.BlockSpec((d, ROUTER_COLS), const),
                  pl.BlockSpec((1, ROUTER_COLS), const)],
        out_specs=[pl.BlockSpec((_SUBLANES, tm), lambda i: (0, i)),
                   pl.BlockSpec((_SUBLANES, tm), lambda i: (0, i)),
                   pl.BlockSpec((N_EXPERTS, _LANES), const)],
        out_shape=[jax.ShapeDtypeStruct((_SUBLANES, t), jnp.int32),
                   jax.ShapeDtypeStruct((_SUBLANES, t), F32),
                   jax.ShapeDtypeStruct((N_EXPERTS, _LANES), F32)],
        scratch_shapes=[pltpu.VMEM((N_EXPERTS, 1), F32)],
        compiler_params=_params("arbitrary"),
        name="moe_router",
    )(x, jnp.concatenate([wh, wl], axis=1), wh, b.reshape(1, ROUTER_COLS))


def _row_copy(src, src_row, dst, dst_row, sem):
    return pltpu.make_async_copy(src.at[pl.ds(src_row, 1)], dst.at[pl.ds(dst_row, 1)], sem)


def _dispatch_body(pos_ref, zt_ref, x_ref, xs_ref, zeros_ref, sem, zsem, *, td, tz, t_total):
    step = pl.program_id(0)
    base = step * td

    def zero_tile(i):
        return pltpu.make_async_copy(zeros_ref, xs_ref.at[pl.ds(pl.multiple_of(i * tz, tz), tz)], zsem)

    @pl.when(step == 0)
    def _():
        zeros_ref[...] = jnp.zeros_like(zeros_ref)

        def start(i, carry):
            @pl.when(zt_ref[i] == 1)
            def _():
                zero_tile(i).start()
            return carry

        def wait(i, carry):
            @pl.when(zt_ref[i] == 1)
            def _():
                zero_tile(i).wait()
            return carry

        lax.fori_loop(0, zt_ref.shape[0], start, 0)
        lax.fori_loop(0, zt_ref.shape[0], wait, 0)

    def issue(r, carry):
        _row_copy(x_ref, r, xs_ref, pos_ref[base + r], sem).start()
        _row_copy(x_ref, r, xs_ref, pos_ref[t_total + base + r], sem).start(priority=1)
        return carry

    lax.fori_loop(0, td, issue, 0, unroll=4)
    tile_rows = xs_ref.at[pl.ds(0, td)]
    pltpu.make_async_copy(x_ref, tile_rows, sem).wait()
    pltpu.make_async_copy(x_ref, tile_rows, sem).wait()


def _dispatch(x, pos, zero_tiles, n_rows, td, tz):
    t, d = x.shape
    td = min(td, t)
    return pl.pallas_call(
        functools.partial(_dispatch_body, td=td, tz=tz, t_total=t),
        grid_spec=pltpu.PrefetchScalarGridSpec(
            num_scalar_prefetch=2,
            grid=(t // td,),
            in_specs=[pl.BlockSpec((td, d), lambda i, pos, zt: (i, 0))],
            out_specs=pl.BlockSpec(memory_space=pl.ANY),
            scratch_shapes=[pltpu.VMEM((tz, d), x.dtype),
                            pltpu.SemaphoreType.DMA(()), pltpu.SemaphoreType.DMA(())]),
        out_shape=jax.ShapeDtypeStruct((n_rows, d), x.dtype),
        compiler_params=_params("arbitrary"),
        name="moe_dispatch",
    )(pos, zero_tiles, x)


def _expert_body(blk_ref, te_ref, nv_ref, xs_ref, wg_ref, wu_ref, wd_ref, o_ref):
    del blk_ref, te_ref
    valid = pl.program_id(0) < nv_ref[0]

    @pl.when(jnp.logical_not(valid))
    def _():
        o_ref[...] = jnp.zeros_like(o_ref)

    @pl.when(valid)
    def _():
        x = xs_ref[...].astype(BF16)
        g = jnp.dot(x, wg_ref[0], preferred_element_type=F32)
        u = jnp.dot(x, wu_ref[0], preferred_element_type=F32)
        hid = (g * _sigmoid(g)) * u
        o_ref[...] = jnp.dot(hid.astype(BF16), wd_ref[0], preferred_element_type=F32)


def _experts(xs, w_gate, w_up, w_down, expert0, blk, tile_expert, n_valid, tm):
    p, d = xs.shape
    ff = w_gate.shape[2]
    n_tiles = p // tm
    rows = lambda i, blk, te, nv: (blk[i], 0)
    wsel = lambda i, blk, te, nv: (expert0 + te[i], 0, 0)
    return pl.pallas_call(
        _expert_body,
        grid_spec=pltpu.PrefetchScalarGridSpec(
            num_scalar_prefetch=3,
            grid=(n_tiles,),
            in_specs=[pl.BlockSpec((tm, d), rows),
                      pl.BlockSpec((1, d, ff), wsel),
                      pl.BlockSpec((1, d, ff), wsel),
                      pl.BlockSpec((1, ff, d), wsel)],
            out_specs=pl.BlockSpec((tm, d), lambda i, blk, te, nv: (i, 0))),
        out_shape=jax.ShapeDtypeStruct((p, d), F32),
        compiler_params=_params("arbitrary"),
        name="moe_experts",
    )(blk, tile_expert, n_valid, xs, w_gate, w_up, w_down)


def _combine_body(pos_ref, ys_ref, x_ref, c_ref, g_ref, b_ref, of_ref, ob_ref, buf_a, buf_b, sem,
                  *, tc, t_total, alpha):
    base = pl.program_id(0) * tc

    def issue(r, carry):
        _row_copy(ys_ref, pos_ref[base + r], buf_a, r, sem).start()
        _row_copy(ys_ref, pos_ref[t_total + base + r], buf_b, r, sem).start(priority=1)
        return carry

    lax.fori_loop(0, tc, issue, 0, unroll=4)
    tile_rows = ys_ref.at[pl.ds(0, tc)]
    pltpu.make_async_copy(tile_rows, buf_a, sem).wait()
    pltpu.make_async_copy(tile_rows, buf_b, sem).wait()
    c = c_ref[...]
    f = c[:, 0:1] * buf_a[...] + c[:, 1:2] * buf_b[...]
    y = _layer_norm(alpha * x_ref[...] + f, g_ref[...], b_ref[...])
    of_ref[...] = y
    ob_ref[...] = y.astype(BF16)


def _combine_ln(ys, pos, x, c, g, b, alpha, tc):
    t, d = x.shape
    tc = min(tc, t)
    const = lambda i, pos: (0, 0)
    tile = lambda i, pos: (i, 0)
    return pl.pallas_call(
        functools.partial(_combine_body, tc=tc, t_total=t, alpha=alpha),
        grid_spec=pltpu.PrefetchScalarGridSpec(
            num_scalar_prefetch=1,
            grid=(t // tc,),
            in_specs=[pl.BlockSpec(memory_space=pl.ANY),
                      pl.BlockSpec((tc, d), tile),
                      pl.BlockSpec((tc, 2), tile),
                      pl.BlockSpec((1, d), const),
                      pl.BlockSpec((1, d), const)],
            out_specs=[pl.BlockSpec((tc, d), tile), pl.BlockSpec((tc, d), tile)],
            scratch_shapes=[pltpu.VMEM((tc, d), F32), pltpu.VMEM((tc, d), F32),
                            pltpu.SemaphoreType.DMA(())]),
        out_shape=[jax.ShapeDtypeStruct((t, d), F32), jax.ShapeDtypeStruct((t, d), BF16)],
        compiler_params=_params("arbitrary"),
        name="moe_combine_ln",
    )(pos, ys, x, c, g.reshape(1, d), b.reshape(1, d))


def _moe_ln(x, w_r1, b_r1, w_r2, b_r2, w_gate, w_up, w_down, layer, g, b, alpha, tiles):
    t, d = x.shape
    tm_e = min(tiles["expert_rows"], t)
    oi, of, cnt = _router(x, w_r1, b_r1, w_r2, b_r2, tiles["router"])
    counts = cnt[:, 0].astype(jnp.int32)
    seg_tiles = (counts + tm_e - 1) // tm_e
    tile_end = jnp.cumsum(seg_tiles)
    seg_start = (tile_end - seg_tiles) * tm_e
    start_of = lambda e: jnp.sum(jnp.where(e[:, None] == jnp.arange(N_EXPERTS, dtype=jnp.int32)[None, :],
                                           seg_start[None, :], 0), axis=1)
    pos = jnp.concatenate([start_of(oi[0]) + oi[2], start_of(oi[1]) + oi[3]])
    n_tiles = 2 * t // tm_e + N_EXPERTS
    n_valid = tile_end[-1:]
    tile_id = jnp.arange(n_tiles, dtype=jnp.int32)
    blk = jnp.minimum(tile_id, n_valid - 1)
    tile_expert = jnp.minimum(jnp.sum((blk[:, None] >= tile_end[None, :]).astype(jnp.int32), axis=1), N_EXPERTS - 1)
    zero_tiles = (jnp.any(tile_id[:, None] == (tile_end - 1)[None, :], axis=1) | (tile_id >= n_valid)).astype(jnp.int32)

    xs = _dispatch(x, pos, zero_tiles, n_tiles * tm_e, tiles["dispatch"], tm_e)
    ys = _experts(xs, w_gate, w_up, w_down, layer * N_EXPERTS, blk, tile_expert, n_valid, tm_e)
    return _combine_ln(ys, pos, x, of[:2].T, g, b, alpha, tiles["combine"])


def _hybrid_mixer_ln(xf, xb, bn, j, w_in_all, conv_w, conv_b, w_r, b_r, w_i, b_i, lam, b_f, w_out_all, g, b, alpha, tiles):
    t, d = xf.shape
    s = t // bn
    heads, block, _ = w_r.shape
    lru_w = heads * block
    fox_heads = b_f.shape[0]
    main = w_in_all.shape[2] - fox_heads
    fox_w = (main - 2 * lru_w) // 3
    dh = fox_w // fox_heads
    tm, tn = tiles["mm_m"], tiles["mm_n"]
    xc = _proj_w32(xb, w_in_all, j, 0, lru_w, BF16, tm, tn, "hy_lru_x_proj_conv", "conv", (conv_w, conv_b), s)
    gg = _proj_w32(xb, w_in_all, j, lru_w, lru_w, BF16, tm, tn, "hy_lru_gate_proj_gelu", "gelu")
    qkv = _proj_w32(xb, w_in_all, j, 2 * lru_w, 3 * fox_w, BF16, tm, tn, "hy_qkv_proj", "scale",
                     scale_cols=fox_w // min(tn, fox_w), scale=dh ** -0.5 * LOG2E)
    w_fl = jnp.pad(w_in_all[j, :, main:], ((0, 0), (0, _LANES - fox_heads))).astype(BF16)
    fl = _proj(xb, w_fl, F32, tiles["small_m"], _LANES, "hy_forget_proj")[:, :fox_heads]
    f2 = _fox_prep(fl.reshape(bn, s, fox_heads).transpose(0, 2, 1), b_f)
    y_lru = _lru(xc.reshape(bn, s, lru_w), gg.reshape(bn, s, lru_w), w_r, b_r, w_i, b_i, lam, tiles["lru"])
    y_att = _fox_attention(qkv.reshape(bn, s, 3 * fox_w), f2, fox_heads, dh, tiles["attn"])
    return _matmul_residual_ln([y_lru.reshape(t, lru_w), y_att.reshape(t, fox_w)], w_out_all, j, xf, g, b, alpha,
                               tiles["ln_m"], tiles["ln_k"], "hy_out_proj_ln")


def _ssd_mixer_ln(xf, xb, bn, j, w_in_all, conv_w, conv_b, dt_bias, a_log, d_skip, norm_g, w_out_all, g, b, alpha, tiles):
    t, d = xf.shape
    s = t // bn
    heads = a_log.shape[0]
    inner = norm_g.shape[0]
    conv_dim = conv_w.shape[1]
    groups = (conv_dim - inner) // (2 * SSD_STATE)
    tm, tn = tiles["mm_m"], tiles["mm_n"]
    zs = _proj_w32(xb, w_in_all, j, 0, inner, BF16, tm, tn, "ssd_z_proj_silu", "silu")
    xbc = _proj_w32(xb, w_in_all, j, inner, conv_dim, BF16, tm, tn, "ssd_xbc_proj_conv_silu", "conv_silu",
                     (conv_w, conv_b), s)
    w_dt = jnp.pad(w_in_all[j, :, inner + conv_dim:], ((0, 0), (0, _LANES - heads))).astype(BF16)
    dt = _proj(xb, w_dt, F32, tiles["small_m"], _LANES, "ssd_dt_proj")
    y = _ssd_core(xbc.reshape(bn, s, conv_dim), zs.reshape(bn, s, inner), dt.reshape(bn, s, _LANES),
                  dt_bias, a_log, d_skip, norm_g, groups)
    return _matmul_residual_ln([y.reshape(t, inner)], w_out_all, j, xf, g, b, alpha,
                               tiles["ln_m"], tiles["ln_k"], "ssd_out_proj_ln")


_TILES = dict(mm_m=1024, mm_n=1024, small_m=1024, ln_m=512, ln_k=2048, lru=256, attn=512,
              router=512, dispatch=512, expert_rows=512, combine=256)


def kernel(x, ln_g, ln_b, hy_w_in, hy_conv_w, hy_conv_b, hy_w_r, hy_b_r, hy_w_i, hy_b_i, hy_lambda, hy_b_f, hy_w_out, ssd_w_in, ssd_conv_w, ssd_conv_b, ssd_dt_bias, ssd_a_log, ssd_d, ssd_norm_g, ssd_w_out, moe_w_r1, moe_b_r1, moe_w_r2, moe_b_r2, moe_w_gate, moe_w_up, moe_w_down):
    bn, s, d = x.shape
    depth = ln_g.shape[0]
    alpha = (2 * depth) ** 0.25
    ff = moe_w_gate.shape[-1]
    hy_w_out_b = hy_w_out.astype(BF16)
    ssd_w_out_b = ssd_w_out.astype(BF16)
    w_gate_b = moe_w_gate.reshape(depth * N_EXPERTS, d, ff).astype(BF16)
    w_up_b = moe_w_up.reshape(depth * N_EXPERTS, d, ff).astype(BF16)
    w_down_b = moe_w_down.reshape(depth * N_EXPERTS, ff, d).astype(BF16)
    xf = x.reshape(bn * s, d)
    xb = xf.astype(BF16)
    for layer in range(depth):
        j = layer // 2
        if layer % 2 == 0:
            xf, xb = _hybrid_mixer_ln(xf, xb, bn, j, hy_w_in, hy_conv_w[j], hy_conv_b[j], hy_w_r[j], hy_b_r[j],
                                      hy_w_i[j], hy_b_i[j], hy_lambda[j], hy_b_f[j], hy_w_out_b,
                                      ln_g[layer, 0], ln_b[layer, 0], alpha, _TILES)
        else:
            xf, xb = _ssd_mixer_ln(xf, xb, bn, j, ssd_w_in, ssd_conv_w[j], ssd_conv_b[j], ssd_dt_bias[j],
                                   ssd_a_log[j], ssd_d[j], ssd_norm_g[j], ssd_w_out_b,
                                   ln_g[layer, 0], ln_b[layer, 0], alpha, _TILES)
        xf, xb = _moe_ln(xf, moe_w_r1[layer], moe_b_r1[layer], moe_w_r2[layer], moe_b_r2[layer],
                         w_gate_b, w_up_b, w_down_b, layer, ln_g[layer, 1], ln_b[layer, 1], alpha, _TILES)
    return xf.reshape(bn, s, d)
```

```python
import functools
import math

import jax
import jax.numpy as jnp
from jax import lax
from jax.experimental import pallas as pl
from jax.experimental.pallas import tpu as pltpu

F32 = jnp.float32
BF16 = jnp.bfloat16

_VMEM_LIMIT_BYTES = 56 * 1024 * 1024
_LANES = 128
_SUBLANES = 8

LN_EPS = 1e-5
RMS_EPS = 1e-5
LRU_C = 8.0
CONV_K = 4
SSD_CHUNK = 128
SSD_HEAD_DIM = 64
SSD_STATE = 128
MOE_GROUPS = 4
MOE_PER_GROUP = 4
N_EXPERTS = MOE_GROUPS * MOE_PER_GROUP
ROUTER_COLS = 128
ROUTER_L2_ROW0 = 8
LOG2E = math.log2(math.e)


def _params(*sem):
    return pltpu.CompilerParams(dimension_semantics=sem, vmem_limit_bytes=_VMEM_LIMIT_BYTES)


def _sigmoid(x):
    return 1.0 / (1.0 + jnp.exp(-x))


def _softplus(x):
    e = jnp.exp(-jnp.abs(x))
    u = 1.0 + e
    return jnp.maximum(x, 0.0) + jnp.where(u == 1.0, e, jnp.log(u) * (e / (u - 1.0)))


def _shift_rows(x, tail, d):
    r = pltpu.roll(x, d, axis=0)
    t = pltpu.roll(tail, d, axis=0)
    row = lax.broadcasted_iota(jnp.int32, tail.shape, 0)
    first = jnp.where(row < d, t, r[:_SUBLANES])
    return jnp.concatenate([first, r[_SUBLANES:]], axis=0)


def _causal_conv(x, tail, w_ref, b_ref):
    acc = b_ref[...] + w_ref[CONV_K - 1:CONV_K, :] * x
    for d in range(1, CONV_K):
        acc = acc + w_ref[CONV_K - 1 - d:CONV_K - d, :] * _shift_rows(x, tail, d)
    return acc


def _cumsum_rows(x):
    n = x.shape[0]
    row = lax.broadcasted_iota(jnp.int32, x.shape, 0)
    d = 1
    while d < n:
        x = x + jnp.where(row >= d, pltpu.roll(x, d, axis=0), 0.0)
        d *= 2
    return x


def _cumsum_lanes(x):
    n = x.shape[1]
    lane = lax.broadcasted_iota(jnp.int32, x.shape, 1)
    d = 1
    while d < n:
        x = x + jnp.where(lane >= d, pltpu.roll(x, d, axis=1), 0.0)
        d *= 2
    return x


def _layer_norm(s, g, b):
    mu = jnp.mean(s, axis=-1, keepdims=True)
    d = s - mu
    var = jnp.mean(d * d, axis=-1, keepdims=True)
    return d * lax.rsqrt(var + LN_EPS) * g + b


def _proj_small_body(x_ref, wt_ref, o_ref):
    o_ref[...] = lax.dot_general(x_ref[...], wt_ref[...].astype(BF16), (((1,), (1,)), ((), ())),
                                 preferred_element_type=F32)


def _proj_small(x, w_t, tm, name):
    m, k = x.shape
    tm = min(tm, m)
    w_t = jnp.pad(w_t, ((0, _LANES - w_t.shape[0]), (0, 0)))
    return pl.pallas_call(
        _proj_small_body,
        grid=(m // tm,),
        in_specs=[pl.BlockSpec((tm, k), lambda i: (i, 0)),
                  pl.BlockSpec((_LANES, k), lambda i: (0, 0))],
        out_specs=pl.BlockSpec((tm, _LANES), lambda i: (i, 0)),
        out_shape=jax.ShapeDtypeStruct((m, _LANES), F32),
        compiler_params=_params("parallel"),
        name=name,
    )(x, w_t)


def _proj_w32_body(x_ref, w_ref, *rest, epilogue, tiles_per_seq, chunk, scale_cols, scale):
    conv = epilogue.startswith("conv")
    if conv:
        cw_ref, cb_ref, o_ref, wb_ref, carry_ref = rest
    else:
        o_ref, wb_ref = rest
    j = pl.program_id(0)
    t = pl.program_id(1)
    tm = x_ref.shape[0]

    @pl.when(t == 0)
    def _():
        wb_ref[...] = w_ref[...].T.astype(BF16)

    if conv:
        @pl.when(t % tiles_per_seq == 0)
        def _():
            carry_ref[...] = jnp.zeros_like(carry_ref)

    x = x_ref[...]
    for c0 in range(0, o_ref.shape[1], chunk):
        cols = slice(c0, c0 + chunk)
        acc = jnp.dot(x, wb_ref[:, cols], preferred_element_type=F32)
        if conv:
            tail = carry_ref[:, cols]
            carry_ref[:, cols] = acc[tm - _SUBLANES:, :]
            acc = _causal_conv(acc, tail, cw_ref.at[:, cols], cb_ref.at[:, cols])
        if epilogue.endswith("silu"):
            acc = acc * _sigmoid(acc)
        elif epilogue == "gelu":
            acc = jax.nn.gelu(acc)
        elif epilogue == "scale":
            acc = acc * jnp.where(j < scale_cols, scale, 1.0)
        o_ref[:, cols] = acc.astype(o_ref.dtype)


def _proj_w32(x, w_all, layer, col0, n, out_dtype, tm, tn, name, epilogue="none", conv=None, seq_len=None,
              scale_cols=0, scale=1.0, chunk=256):
    m, k = x.shape
    tm, tn = min(tm, m), min(tn, n)
    assert col0 % tn == 0 and n % tn == 0 and m % tm == 0 and tn % chunk == 0
    jb = col0 // tn
    in_specs = [pl.BlockSpec((tm, k), lambda j, t: (t, 0)),
                pl.BlockSpec((None, tn, k), lambda j, t: (layer, jb + j, 0))]
    args = [x, w_all]
    scratch = [pltpu.VMEM((k, tn), BF16)]
    tiles_per_seq = 1
    if conv is not None:
        assert seq_len % tm == 0
        tiles_per_seq = seq_len // tm
        in_specs += [pl.BlockSpec((CONV_K, tn), lambda j, t: (0, j)),
                     pl.BlockSpec((1, tn), lambda j, t: (0, j))]
        args += [conv[0], conv[1].reshape(1, n)]
        scratch.append(pltpu.VMEM((_SUBLANES, tn), F32))
    return pl.pallas_call(
        functools.partial(_proj_w32_body, epilogue=epilogue, tiles_per_seq=tiles_per_seq, chunk=chunk,
                          scale_cols=scale_cols, scale=scale),
        grid=(n // tn, m // tm),
        in_specs=in_specs,
        out_specs=pl.BlockSpec((tm, tn), lambda j, t: (t, j)),
        out_shape=jax.ShapeDtypeStruct((m, n), out_dtype),
        scratch_shapes=scratch,
        compiler_params=_params("arbitrary", "arbitrary"),
        name=name,
    )(*args)


def _mm_ln_body(*refs, alpha, n_a):
    a_refs = refs[:n_a]
    w_ref, x_ref, g_ref, b_ref, of_ref, ob_ref, acc_ref = refs[n_a:]
    k = pl.program_id(1)
    ka = a_refs[0].shape[1]
    part = jnp.dot(a_refs[0][...], w_ref[:ka, :], preferred_element_type=F32)
    for idx in range(1, n_a):
        part = part + jnp.dot(a_refs[idx][...], w_ref[idx * ka:(idx + 1) * ka, :], preferred_element_type=F32)

    @pl.when(k == 0)
    def _():
        acc_ref[...] = part

    @pl.when(k > 0)
    def _():
        acc_ref[...] += part

    @pl.when(k == pl.num_programs(1) - 1)
    def _():
        y = _layer_norm(alpha * x_ref[...] + acc_ref[...], g_ref[...], b_ref[...])
        of_ref[...] = y
        ob_ref[...] = y.astype(BF16)


def _matmul_residual_ln(a_list, w_all, layer, x, g, b, alpha, tm, tk, name):
    n_a = len(a_list)
    m, ka = a_list[0].shape
    n = w_all.shape[2]
    tm = min(tm, m)
    tka = ka if n_a > 1 else min(tk, ka)
    nk = ka // tka
    return pl.pallas_call(
        functools.partial(_mm_ln_body, alpha=alpha, n_a=n_a),
        grid=(m // tm, nk),
        in_specs=[pl.BlockSpec((tm, tka), lambda i, k: (i, k))] * n_a + [
                  pl.BlockSpec((None, n_a * tka, n), lambda i, k: (layer, k, 0)),
                  pl.BlockSpec((tm, n), lambda i, k: (i, 0)),
                  pl.BlockSpec((1, n), lambda i, k: (0, 0)),
                  pl.BlockSpec((1, n), lambda i, k: (0, 0))],
        out_specs=[pl.BlockSpec((tm, n), lambda i, k: (i, 0)),
                   pl.BlockSpec((tm, n), lambda i, k: (i, 0))],
        out_shape=[jax.ShapeDtypeStruct((m, n), F32), jax.ShapeDtypeStruct((m, n), BF16)],
        scratch_shapes=[pltpu.VMEM((tm, n), F32)],
        compiler_params=_params("parallel", "arbitrary"),
        name=name,
    )(*a_list, w_all, x, g.reshape(1, n), b.reshape(1, n))


def _fox_prep_body(fl_ref, bf_ref, f_ref):
    x = fl_ref[0] + bf_ref[...]
    f_ref[0] = _cumsum_lanes(-_softplus(-x)) * LOG2E


def _fox_prep(fl_t, b_f):
    bn, h, s = fl_t.shape
    return pl.pallas_call(
        _fox_prep_body,
        grid=(bn,),
        in_specs=[pl.BlockSpec((1, h, s), lambda b: (b, 0, 0)),
                  pl.BlockSpec((h, 1), lambda b: (0, 0))],
        out_specs=pl.BlockSpec((1, h, s), lambda b: (b, 0, 0)),
        out_shape=jax.ShapeDtypeStruct((bn, h, s), F32),
        compiler_params=_params("parallel"),
        name="fox_prep",
    )(fl_t, b_f.reshape(h, 1))


def _lru_body(xc_ref, gg_ref, wr_ref, br_ref, wi_ref, bi_ref, lam_ref, y_ref, h_ref, *, heads, block):
    @pl.when(pl.program_id(1) == 0)
    def _():
        h_ref[...] = jnp.zeros_like(h_ref)

    xcb = xc_ref[0]
    xc = xcb.astype(F32)
    ts = xc.shape[0]
    r_parts, i_parts = [], []
    for hd in range(heads):
        xh = xcb[:, hd * block:(hd + 1) * block]
        r_parts.append(jnp.dot(xh, wr_ref[hd], preferred_element_type=F32))
        i_parts.append(jnp.dot(xh, wi_ref[hd], preferred_element_type=F32))
    r = _sigmoid(jnp.concatenate(r_parts, axis=1) + br_ref[...])
    ig = _sigmoid(jnp.concatenate(i_parts, axis=1) + bi_ref[...])
    log_a = -LRU_C * r * _softplus(-lam_ref[...])
    a = jnp.exp(log_a)
    u = jnp.sqrt(-jnp.tanh(log_a) * (a * a + 1.0)) * (ig * xc)

    row = lax.broadcasted_iota(jnp.int32, a.shape, 0)
    d = 1
    while d < ts:
        keep = row >= d
        u = u + a * jnp.where(keep, pltpu.roll(u, d, axis=0), 0.0)
        a = a * jnp.where(keep, pltpu.roll(a, d, axis=0), 1.0)
        d *= 2
    h = u + a * h_ref[...]
    h_ref[...] = h[ts - 1:ts, :]
    y_ref[0] = (h * gg_ref[0].astype(F32)).astype(y_ref.dtype)


def _lru(xc, gg, w_r, b_r, w_i, b_i, lam, ts):
    bn, s, width = xc.shape
    heads, block, _ = w_r.shape
    ts = min(ts, s)
    vec = lambda v: v.reshape(1, width)
    const2 = lambda b, i: (0, 0)
    const3 = lambda b, i: (0, 0, 0)
    tile = pl.BlockSpec((1, ts, width), lambda b, i: (b, i, 0))
    return pl.pallas_call(
        functools.partial(_lru_body, heads=heads, block=block),
        grid=(bn, s // ts),
        in_specs=[tile, tile,
                  pl.BlockSpec((heads, block, block), const3),
                  pl.BlockSpec((1, width), const2),
                  pl.BlockSpec((heads, block, block), const3),
                  pl.BlockSpec((1, width), const2),
                  pl.BlockSpec((1, width), const2)],
        out_specs=tile,
        out_shape=jax.ShapeDtypeStruct((bn, s, width), BF16),
        scratch_shapes=[pltpu.VMEM((1, width), F32)],
        compiler_params=_params("parallel", "arbitrary"),
        name="rg_lru",
    )(xc, gg, w_r.astype(BF16), vec(b_r), w_i.astype(BF16), vec(b_i), vec(lam))


def _split3(f):
    hi = f.astype(BF16).astype(F32)
    r = f - hi
    mid = r.astype(BF16).astype(F32)
    return hi, mid, r - mid


def _fox_attn_body(q_ref, k_ref, v_ref, fc_ref, o_ref, kx_ref, *, tq):
    qi = pl.program_id(2)
    s_len = k_ref.shape[1]
    dh = q_ref.shape[2]

    @pl.when(qi == 0)
    def _():
        for c0 in range(0, s_len, tq):
            hi, mid, lo = _split3(fc_ref[0, 0, c0:c0 + tq, :])
            lane = lax.broadcasted_iota(jnp.int32, (tq, _LANES), 1)
            kx = jnp.where(lane < 3, 1.0, jnp.where(lane == 3, -hi, jnp.where(lane == 4, -mid,
                           jnp.where(lane == 5, -lo, 0.0))))
            kx_ref[c0:c0 + tq, :] = kx.astype(BF16)

    hi, mid, lo = _split3(fc_ref[0, 0, pl.ds(pl.multiple_of(qi * tq, tq), tq), :])
    lane = lax.broadcasted_iota(jnp.int32, (tq, _LANES), 1)
    qx = jnp.where(lane == 0, hi, jnp.where(lane == 1, mid, jnp.where(lane == 2, lo,
                   jnp.where(lane < 6, 1.0, 0.0))))
    q = jnp.concatenate([q_ref[0], qx.astype(BF16)], axis=1)

    def scores(j):
        start = pl.multiple_of(j * tq, tq)
        k = jnp.concatenate([k_ref[0, pl.ds(start, tq), :], kx_ref[pl.ds(start, tq), :]], axis=1)
        return lax.dot_general(q, k, (((1,), (1,)), ((), ())), preferred_element_type=F32), start

    def update(carry, sc, start):
        m, l, acc = carry
        m_new = jnp.maximum(m, jnp.max(sc, axis=1, keepdims=True))
        corr = jnp.exp2(m - m_new)
        p = jnp.exp2(sc - m_new)
        l = corr * l + jnp.sum(p, axis=1, keepdims=True)
        v = v_ref[0, pl.ds(start, tq), :]
        acc = corr * acc + jnp.dot(p.astype(BF16), v, preferred_element_type=F32)
        return m_new, l, acc

    def diagonal(carry):
        sc, start = scores(qi)
        rq = lax.broadcasted_iota(jnp.int32, sc.shape, 0)
        ck = lax.broadcasted_iota(jnp.int32, sc.shape, 1)
        return update(carry, jnp.where(ck <= rq, sc, -1e30), start)

    def two_blocks(jj, carry):
        sc_a, start_a = scores(2 * jj)
        sc_b, start_b = scores(2 * jj + 1)
        return update(update(carry, sc_a, start_a), sc_b, start_b)

    def odd_tail(carry):
        sc_a, start_a = scores(qi - 1)
        return diagonal(update(carry, sc_a, start_a))

    init = (jnp.full((tq, 1), -1e30, F32), jnp.zeros((tq, 1), F32), jnp.zeros((tq, dh), F32))
    carry = lax.fori_loop(0, qi // 2, two_blocks, init)
    _, l, acc = lax.cond(qi % 2 == 1, odd_tail, diagonal, carry)
    o_ref[0] = (acc / l).astype(o_ref.dtype)


def _fox_attention(qkv, f2, heads, dh, tq):
    bn, s, _ = qkv.shape
    tq = min(tq, s)
    return pl.pallas_call(
        functools.partial(_fox_attn_body, tq=tq),
        grid=(bn, heads, s // tq),
        in_specs=[pl.BlockSpec((1, tq, dh), lambda b, h, i: (b, i, h)),
                  pl.BlockSpec((1, s, dh), lambda b, h, i: (b, 0, heads + h)),
                  pl.BlockSpec((1, s, dh), lambda b, h, i: (b, 0, 2 * heads + h)),
                  pl.BlockSpec((1, 1, s, 1), lambda b, h, i: (b, h, 0, 0))],
        out_specs=pl.BlockSpec((1, tq, dh), lambda b, h, i: (b, i, h)),
        out_shape=jax.ShapeDtypeStruct((bn, s, heads * dh), BF16),
        scratch_shapes=[pltpu.VMEM((s, _LANES), BF16)],
        compiler_params=_params("parallel", "parallel", "arbitrary"),
        name="fox_attention",
    )(qkv, qkv, qkv, f2[..., None])


def _ssd_body(xbc_ref, z_ref, dt_ref, dtb_ref, alog_ref, dexp_ref, ng_ref, o_ref,
              state_ref, acst_ref, dtt_ref, wt_ref, *, groups, inner):
    L = SSD_CHUNK
    gw = inner // groups
    pairs = gw // _LANES
    hpg = gw // SSD_HEAD_DIM

    @pl.when(pl.program_id(1) == 0)
    def _():
        state_ref[...] = jnp.zeros_like(state_ref)

    dt = _softplus(dt_ref[0] + dtb_ref[...])
    acs = _cumsum_rows(dt * (-jnp.exp(alog_ref[...])))
    acs_t = acs.T
    dt_t = dt.T
    acst_ref[...] = acs_t
    dtt_ref[...] = dt_t
    wt_ref[...] = dt_t * jnp.exp(acs_t[:, L - 1:L] - acs_t)
    dec_last = jnp.exp(acs[L - 1:L, :])

    rq = lax.broadcasted_iota(jnp.int32, (L, L), 0)
    ck = lax.broadcasted_iota(jnp.int32, (L, L), 1)
    causal = ck <= rq
    lo = lax.broadcasted_iota(jnp.int32, (L, _LANES), 1) < SSD_HEAD_DIM
    lo_row = lo[:1]

    for g in range(groups):
        bmat = xbc_ref[0, :, inner + g * SSD_STATE:inner + (g + 1) * SSD_STATE]
        cmat = xbc_ref[0, :, inner + (groups + g) * SSD_STATE:inner + (groups + g + 1) * SSD_STATE]
        cbm = lax.dot_general(cmat, bmat, (((1,), (1,)), ((), ())), preferred_element_type=F32)
        cmat = cmat.astype(F32)
        b_t = bmat.astype(F32).T
        ys = []
        for j in range(pairs):
            col0 = g * gw + j * _LANES
            xs_p = xbc_ref[0, :, col0:col0 + _LANES].astype(F32)
            m_parts, c_parts, bw_parts, dl = [], [], [], []
            for e in range(2):
                hd = g * hpg + 2 * j + e
                col = jnp.broadcast_to(acs[:, hd:hd + 1], (L, L))
                row = acst_ref[hd:hd + 1, :]
                dec = jnp.exp(jnp.where(causal, col - row, -jnp.inf))
                m_parts.append((cbm * dec * dtt_ref[hd:hd + 1, :]).astype(BF16))
                c_parts.append((cmat * jnp.exp(col)).astype(BF16))
                bw_parts.append((b_t * wt_ref[hd:hd + 1, :]).astype(BF16))
                dl.append(dec_last[:, hd:hd + 1])
            prev = state_ref[:, col0:col0 + _LANES]
            x_top = jnp.where(lo, xs_p, 0.0).astype(BF16)
            x_bot = jnp.where(lo, 0.0, xs_p).astype(BF16)
            p_top = jnp.where(lo, prev, 0.0).astype(BF16)
            p_bot = jnp.where(lo, 0.0, prev).astype(BF16)
            lhs = jnp.concatenate(m_parts + c_parts, axis=1)
            rhs = jnp.concatenate([x_top, x_bot, p_top, p_bot], axis=0)
            y = jnp.dot(lhs, rhs, preferred_element_type=F32)
            st = jnp.dot(jnp.concatenate(bw_parts, axis=1), jnp.concatenate([x_top, x_bot], axis=0),
                         preferred_element_type=F32)
            state_ref[:, col0:col0 + _LANES] = prev * jnp.where(lo_row, dl[0], dl[1]) + st
            y = y + dexp_ref[:, col0:col0 + _LANES] * xs_p
            ys.append(y * z_ref[0, :, col0:col0 + _LANES].astype(F32))
        yg = jnp.concatenate(ys, axis=1)
        ms = jnp.mean(yg * yg, axis=1, keepdims=True)
        gcols = slice(g * gw, (g + 1) * gw)
        o_ref[0, :, gcols] = (yg * lax.rsqrt(ms + RMS_EPS) * ng_ref[:, gcols]).astype(o_ref.dtype)


def _ssd_core(xbc, zs, dt, dt_bias, a_log, d_skip, norm_g, groups):
    bn, s, conv_dim = xbc.shape
    inner = zs.shape[2]
    heads = a_log.shape[0]
    L = SSD_CHUNK
    pad = lambda v: jnp.pad(v, (0, _LANES - heads)).reshape(1, _LANES)
    const2 = lambda b, c: (0, 0)
    return pl.pallas_call(
        functools.partial(_ssd_body, groups=groups, inner=inner),
        grid=(bn, s // L),
        in_specs=[pl.BlockSpec((1, L, conv_dim), lambda b, c: (b, c, 0)),
                  pl.BlockSpec((1, L, inner), lambda b, c: (b, c, 0)),
                  pl.BlockSpec((1, L, _LANES), lambda b, c: (b, c, 0)),
                  pl.BlockSpec((1, _LANES), const2),
                  pl.BlockSpec((1, _LANES), const2),
                  pl.BlockSpec((1, inner), const2),
                  pl.BlockSpec((1, inner), const2)],
        out_specs=pl.BlockSpec((1, L, inner), lambda b, c: (b, c, 0)),
        out_shape=jax.ShapeDtypeStruct((bn, s, inner), BF16),
        scratch_shapes=[pltpu.VMEM((SSD_STATE, inner), F32),
                        pltpu.VMEM((_LANES, L), F32),
                        pltpu.VMEM((_LANES, L), F32),
                        pltpu.VMEM((_LANES, L), F32)],
        compiler_params=_params("parallel", "arbitrary"),
        name="ssd_core",
    )(xbc, zs, dt, pad(dt_bias), pad(a_log),
      jnp.repeat(d_skip, SSD_HEAD_DIM).reshape(1, inner), norm_g.reshape(1, inner))


def _router_body(x_ref, whl_ref, wh_ref, b_ref, oi_ref, of_ref, cnt_ref, base_ref):
    i = pl.program_id(0)

    @pl.when(i == 0)
    def _():
        base_ref[...] = jnp.zeros_like(base_ref)

    x = x_ref[...]
    tm = x.shape[0]
    xh = x.astype(BF16)
    xl = (x - xh.astype(F32)).astype(BF16)
    r = jnp.dot(xh, whl_ref[...], preferred_element_type=F32)
    logits = (r[:, :ROUTER_COLS] + r[:, ROUTER_COLS:]
              + jnp.dot(xl, wh_ref[...], preferred_element_type=F32) + b_ref[...])
    lt = logits.T

    row = lax.broadcasted_iota(jnp.int32, (_SUBLANES, tm), 0)
    neg = -jnp.inf
    l1 = jnp.where(row < MOE_GROUPS, lt[:_SUBLANES], neg)
    e1 = jnp.exp(l1 - jnp.max(l1, axis=0, keepdims=True))
    p1 = e1 / jnp.sum(e1, axis=0, keepdims=True)
    pg = jnp.max(p1, axis=0, keepdims=True)
    gidx = jnp.min(jnp.where(p1 == pg, row, _SUBLANES), axis=0, keepdims=True)

    sel = jnp.full((_SUBLANES, tm), neg, F32)
    for gi in range(MOE_GROUPS):
        r0 = ROUTER_L2_ROW0 + _SUBLANES * gi
        sel = jnp.where((gidx == gi) & (row < MOE_PER_GROUP), lt[r0:r0 + _SUBLANES], sel)
    va = jnp.max(sel, axis=0, keepdims=True)
    ia = jnp.min(jnp.where(sel == va, row, _SUBLANES), axis=0, keepdims=True)
    sel_b = jnp.where(row == ia, neg, sel)
    vb = jnp.max(sel_b, axis=0, keepdims=True)
    ib = jnp.min(jnp.where(sel_b == vb, row, _SUBLANES), axis=0, keepdims=True)
    t = jnp.exp(vb - va)
    ca = pg * (1.0 / (1.0 + t))
    cb = pg * (t / (1.0 + t))
    ea = gidx * MOE_PER_GROUP + ia
    eb = gidx * MOE_PER_GROUP + ib

    erow = lax.broadcasted_iota(jnp.int32, (N_EXPERTS, tm), 0)
    hit_a = erow == ea
    hit_b = erow == eb
    oh = jnp.where(hit_a, 1.0, 0.0) + jnp.where(hit_b, 1.0, 0.0)
    before = (lax.broadcasted_iota(jnp.int32, (tm, tm), 0)
              < lax.broadcasted_iota(jnp.int32, (tm, tm), 1))
    prefix = jnp.dot(oh.astype(BF16), jnp.where(before, 1.0, 0.0).astype(BF16),
                     preferred_element_type=F32) + base_ref[...]
    rank_a = jnp.sum(jnp.where(hit_a, prefix, 0.0), axis=0, keepdims=True)
    rank_b = jnp.sum(jnp.where(hit_b, prefix, 0.0), axis=0, keepdims=True)
    base_ref[...] += jnp.sum(oh, axis=1, keepdims=True)
    cnt_ref[...] = jnp.broadcast_to(base_ref[...], cnt_ref.shape)

    zi = jnp.zeros((1, tm), jnp.int32)
    oi_ref[...] = jnp.concatenate(
        [ea, eb, rank_a.astype(jnp.int32), rank_b.astype(jnp.int32), zi, zi, zi, zi], axis=0)
    zf = jnp.zeros((1, tm), F32)
    of_ref[...] = jnp.concatenate([ca, cb, zf, zf, zf, zf, zf, zf], axis=0)


def _router(x, w_r1, b_r1, w_r2, b_r2, tm):
    t, d = x.shape
    tm = min(tm, t)
    w = jnp.zeros((d, ROUTER_COLS), F32).at[:, :MOE_GROUPS].set(w_r1)
    b = jnp.zeros((ROUTER_COLS,), F32).at[:MOE_GROUPS].set(b_r1)
    for gi in range(MOE_GROUPS):
        c0 = ROUTER_L2_ROW0 + _SUBLANES * gi
        w = w.at[:, c0:c0 + MOE_PER_GROUP].set(w_r2[gi])
        b = b.at[c0:c0 + MOE_PER_GROUP].set(b_r2[gi])
    wh = w.astype(BF16)
    wl = (w - wh.astype(F32)).astype(BF16)
    const = lambda i: (0, 0)
    return pl.pallas_call(
        _router_body,
        grid=(t // tm,),
        in_specs=[pl.BlockSpec((tm, d), lambda i: (i, 0)),
                  pl.BlockSpec((d, 2 * ROUTER_COLS), const),
                  pl.BlockSpec((d, ROUTER_COLS), const),
                  pl.BlockSpec((1, ROUTER_COLS), const)],
        out_specs=[pl.BlockSpec((_SUBLANES, tm), lambda i: (0, i)),
                   pl.BlockSpec((_SUBLANES, tm), lambda i: (0, i)),
                   pl.BlockSpec((N_EXPERTS, _LANES), const)],
        out_shape=[jax.ShapeDtypeStruct((_SUBLANES, t), jnp.int32),
                   jax.ShapeDtypeStruct((_SUBLANES, t), F32),
                   jax.ShapeDtypeStruct((N_EXPERTS, _LANES), F32)],
        scratch_shapes=[pltpu.VMEM((N_EXPERTS, 1), F32)],
        compiler_params=_params("arbitrary"),
        name="moe_router",
    )(x, jnp.concatenate([wh, wl], axis=1), wh, b.reshape(1, ROUTER_COLS))


def _row_copy(src, src_row, dst, dst_row, sem):
    return pltpu.make_async_copy(src.at[pl.ds(src_row, 1)], dst.at[pl.ds(dst_row, 1)], sem)


def _dispatch_body(pos_ref, zt_ref, x_ref, xs_ref, zeros_ref, sem, zsem, *, td, tz, t_total):
    step = pl.program_id(0)
    base = step * td

    def zero_tile(i):
        return pltpu.make_async_copy(zeros_ref, xs_ref.at[pl.ds(pl.multiple_of(i * tz, tz), tz)], zsem)

    @pl.when(step == 0)
    def _():
        zeros_ref[...] = jnp.zeros_like(zeros_ref)

        def start(i, carry):
            @pl.when(zt_ref[i] == 1)
            def _():
                zero_tile(i).start()
            return carry

        def wait(i, carry):
            @pl.when(zt_ref[i] == 1)
            def _():
                zero_tile(i).wait()
            return carry

        lax.fori_loop(0, zt_ref.shape[0], start, 0)
        lax.fori_loop(0, zt_ref.shape[0], wait, 0)

    slot = step % 2

    def issue(r, carry):
        _row_copy(x_ref, base + r, xs_ref, pos_ref[base + r], sem.at[slot]).start()
        _row_copy(x_ref, base + r, xs_ref, pos_ref[t_total + base + r], sem.at[slot]).start()
        return carry

    lax.fori_loop(0, td, issue, 0, unroll=4)

    def drain(which):
        rows = xs_ref.at[pl.ds(0, td)]
        pltpu.make_async_copy(x_ref.at[pl.ds(0, td)], rows, sem.at[which]).wait()
        pltpu.make_async_copy(x_ref.at[pl.ds(0, td)], rows, sem.at[which]).wait()

    @pl.when(step > 0)
    def _():
        drain(1 - slot)

    @pl.when(step == pl.num_programs(0) - 1)
    def _():
        drain(slot)


def _dispatch(x, pos, zero_tiles, n_rows, td, tz):
    t, d = x.shape
    td = min(td, t)
    return pl.pallas_call(
        functools.partial(_dispatch_body, td=td, tz=tz, t_total=t),
        grid_spec=pltpu.PrefetchScalarGridSpec(
            num_scalar_prefetch=2,
            grid=(t // td,),
            in_specs=[pl.BlockSpec(memory_space=pl.ANY)],
            out_specs=pl.BlockSpec(memory_space=pl.ANY),
            scratch_shapes=[pltpu.VMEM((tz, d), x.dtype),
                            pltpu.SemaphoreType.DMA((2,)), pltpu.SemaphoreType.DMA(())]),
        out_shape=jax.ShapeDtypeStruct((n_rows, d), x.dtype),
        compiler_params=_params("arbitrary"),
        name="moe_dispatch",
    )(pos, zero_tiles, x)


def _expert_body(blk_ref, te_ref, nv_ref, xs_ref, wg_ref, wu_ref, wd_ref, o_ref, xb_ref):
    del blk_ref, te_ref
    c = pl.program_id(1)
    valid = pl.program_id(0) < nv_ref[0]

    @pl.when(jnp.logical_not(valid) & (c == 0))
    def _():
        o_ref[...] = jnp.zeros_like(o_ref)

    @pl.when(valid & (c == 0))
    def _():
        xb_ref[...] = xs_ref[...].astype(BF16)

    @pl.when(valid)
    def _():
        x = xb_ref[...]
        g = jnp.dot(x, wg_ref[0].astype(BF16), preferred_element_type=F32)
        u = jnp.dot(x, wu_ref[0].astype(BF16), preferred_element_type=F32)
        hid = (g * _sigmoid(g)) * u
        part = jnp.dot(hid.astype(BF16), wd_ref[0].astype(BF16), preferred_element_type=F32)

        @pl.when(c == 0)
        def _():
            o_ref[...] = part

        @pl.when(c > 0)
        def _():
            o_ref[...] += part


def _experts(xs, w_gate, w_up, w_down, expert0, blk, tile_expert, n_valid, tm, tf):
    p, d = xs.shape
    ff = w_gate.shape[2]
    n_tiles = p // tm
    n_c = ff // tf
    chunk = lambda i, c, nv: jnp.where(i < nv[0], c, n_c - 1)
    rows = lambda i, c, blk, te, nv: (blk[i], 0)
    wcol = lambda i, c, blk, te, nv: (expert0 + te[i], 0, chunk(i, c, nv))
    wrow = lambda i, c, blk, te, nv: (expert0 + te[i], chunk(i, c, nv), 0)
    return pl.pallas_call(
        _expert_body,
        grid_spec=pltpu.PrefetchScalarGridSpec(
            num_scalar_prefetch=3,
            grid=(n_tiles, n_c),
            in_specs=[pl.BlockSpec((tm, d), rows),
                      pl.BlockSpec((1, d, tf), wcol),
                      pl.BlockSpec((1, d, tf), wcol),
                      pl.BlockSpec((1, tf, d), wrow)],
            out_specs=pl.BlockSpec((tm, d), lambda i, c, blk, te, nv: (i, 0)),
            scratch_shapes=[pltpu.VMEM((tm, d), BF16)]),
        out_shape=jax.ShapeDtypeStruct((p, d), F32),
        compiler_params=_params("arbitrary", "arbitrary"),
        name="moe_experts",
    )(blk, tile_expert, n_valid, xs, w_gate, w_up, w_down)


def _combine_body(pos_ref, ys_ref, x_ref, c_ref, g_ref, b_ref, of_ref, ob_ref, buf_a, buf_b, sem,
                  *, tc, t_total, alpha):
    step = pl.program_id(0)
    slot = step % 2

    def issue(tile, which):
        base = tile * tc

        def body(r, carry):
            _row_copy(ys_ref, pos_ref[base + r], buf_a.at[which], r, sem.at[which]).start()
            _row_copy(ys_ref, pos_ref[t_total + base + r], buf_b.at[which], r, sem.at[which]).start(priority=1)
            return carry

        lax.fori_loop(0, tc, body, 0, unroll=4)

    @pl.when(step == 0)
    def _():
        issue(0, 0)

    @pl.when(step + 1 < pl.num_programs(0))
    def _():
        issue(step + 1, 1 - slot)

    tile_rows = ys_ref.at[pl.ds(0, tc)]
    pltpu.make_async_copy(tile_rows, buf_a.at[slot], sem.at[slot]).wait()
    pltpu.make_async_copy(tile_rows, buf_b.at[slot], sem.at[slot]).wait()
    c = c_ref[...]
    f = c[:, 0:1] * buf_a[slot] + c[:, 1:2] * buf_b[slot]
    y = _layer_norm(alpha * x_ref[...] + f, g_ref[...], b_ref[...])
    of_ref[...] = y
    ob_ref[...] = y.astype(BF16)


def _combine_ln(ys, pos, x, c, g, b, alpha, tc):
    t, d = x.shape
    tc = min(tc, t)
    const = lambda i, pos: (0, 0)
    tile = lambda i, pos: (i, 0)
    return pl.pallas_call(
        functools.partial(_combine_body, tc=tc, t_total=t, alpha=alpha),
        grid_spec=pltpu.PrefetchScalarGridSpec(
            num_scalar_prefetch=1,
            grid=(t // tc,),
            in_specs=[pl.BlockSpec(memory_space=pl.ANY),
                      pl.BlockSpec((tc, d), tile),
                      pl.BlockSpec((tc, 2), tile),
                      pl.BlockSpec((1, d), const),
                      pl.BlockSpec((1, d), const)],
            out_specs=[pl.BlockSpec((tc, d), tile), pl.BlockSpec((tc, d), tile)],
            scratch_shapes=[pltpu.VMEM((2, tc, d), F32), pltpu.VMEM((2, tc, d), F32),
                            pltpu.SemaphoreType.DMA((2,))]),
        out_shape=[jax.ShapeDtypeStruct((t, d), F32), jax.ShapeDtypeStruct((t, d), BF16)],
        compiler_params=_params("arbitrary"),
        name="moe_combine_ln",
    )(pos, ys, x, c, g.reshape(1, d), b.reshape(1, d))


def _moe_ln(x, w_r1, b_r1, w_r2, b_r2, w_gate, w_up, w_down, layer, g, b, alpha, tiles):
    t, d = x.shape
    tm_e = min(tiles["expert_rows"], t)
    oi, of, cnt = _router(x, w_r1, b_r1, w_r2, b_r2, tiles["router"])
    counts = cnt[:, 0].astype(jnp.int32)
    seg_tiles = (counts + tm_e - 1) // tm_e
    tile_end = jnp.cumsum(seg_tiles)
    seg_start = (tile_end - seg_tiles) * tm_e
    start_of = lambda e: jnp.sum(jnp.where(e[:, None] == jnp.arange(N_EXPERTS, dtype=jnp.int32)[None, :],
                                           seg_start[None, :], 0), axis=1)
    pos = jnp.concatenate([start_of(oi[0]) + oi[2], start_of(oi[1]) + oi[3]])
    n_tiles = 2 * t // tm_e + N_EXPERTS
    n_valid = tile_end[-1:]
    tile_id = jnp.arange(n_tiles, dtype=jnp.int32)
    blk = jnp.minimum(tile_id, n_valid - 1)
    tile_expert = jnp.minimum(jnp.sum((blk[:, None] >= tile_end[None, :]).astype(jnp.int32), axis=1), N_EXPERTS - 1)
    zero_tiles = (jnp.any(tile_id[:, None] == (tile_end - 1)[None, :], axis=1) | (tile_id >= n_valid)).astype(jnp.int32)

    xs = _dispatch(x, pos, zero_tiles, n_tiles * tm_e, tiles["dispatch"], tm_e)
    ys = _experts(xs, w_gate, w_up, w_down, layer * N_EXPERTS, blk, tile_expert, n_valid, tm_e,
                  min(tiles["expert_ff"], w_gate.shape[2]))
    return _combine_ln(ys, pos, x, of[:2].T, g, b, alpha, tiles["combine"])


def _hybrid_mixer_ln(xf, xb, bn, j, w_in_all, conv_w, conv_b, w_r, b_r, w_i, b_i, lam, b_f, w_out_all, g, b, alpha, tiles):
    t, d = xf.shape
    s = t // bn
    heads, block, _ = w_r.shape
    lru_w = heads * block
    fox_heads = b_f.shape[0]
    main = w_in_all.shape[1] - fox_heads
    fox_w = (main - 2 * lru_w) // 3
    dh = fox_w // fox_heads
    tm, tn = tiles["mm_m"], tiles["mm_n"]
    xc = _proj_w32(xb, w_in_all, j, 0, lru_w, BF16, tm, tn, "hy_lru_x_proj_conv", "conv", (conv_w, conv_b), s)
    gg = _proj_w32(xb, w_in_all, j, lru_w, lru_w, BF16, tm, tn, "hy_lru_gate_proj_gelu", "gelu")
    qkv = _proj_w32(xb, w_in_all, j, 2 * lru_w, 3 * fox_w, BF16, tm, tn, "hy_qkv_proj", "scale",
                     scale_cols=fox_w // min(tn, fox_w), scale=dh ** -0.5 * LOG2E)
    fl = _proj_small(xb, w_in_all[j, main:, :], tiles["small_m"], "hy_forget_proj")[:, :fox_heads]
    f2 = _fox_prep(fl.reshape(bn, s, fox_heads).transpose(0, 2, 1), b_f)
    y_lru = _lru(xc.reshape(bn, s, lru_w), gg.reshape(bn, s, lru_w), w_r, b_r, w_i, b_i, lam, tiles["lru"])
    y_att = _fox_attention(qkv.reshape(bn, s, 3 * fox_w), f2, fox_heads, dh, tiles["attn"])
    return _matmul_residual_ln([y_lru.reshape(t, lru_w), y_att.reshape(t, fox_w)], w_out_all, j, xf, g, b, alpha,
                               tiles["ln_m"], tiles["ln_k"], "hy_out_proj_ln")


def _ssd_mixer_ln(xf, xb, bn, j, w_in_all, conv_w, conv_b, dt_bias, a_log, d_skip, norm_g, w_out_all, g, b, alpha, tiles):
    t, d = xf.shape
    s = t // bn
    heads = a_log.shape[0]
    inner = norm_g.shape[0]
    conv_dim = conv_w.shape[1]
    groups = (conv_dim - inner) // (2 * SSD_STATE)
    tm, tn = tiles["mm_m"], tiles["mm_n"]
    zs = _proj_w32(xb, w_in_all, j, 0, inner, BF16, tm, tn, "ssd_z_proj_silu", "silu")
    xbc = _proj_w32(xb, w_in_all, j, inner, conv_dim, BF16, tm, tn, "ssd_xbc_proj_conv_silu", "conv_silu",
                     (conv_w, conv_b), s)
    dt = _proj_small(xb, w_in_all[j, inner + conv_dim:, :], tiles["small_m"], "ssd_dt_proj")
    y = _ssd_core(xbc.reshape(bn, s, conv_dim), zs.reshape(bn, s, inner), dt.reshape(bn, s, _LANES),
                  dt_bias, a_log, d_skip, norm_g, groups)
    return _matmul_residual_ln([y.reshape(t, inner)], w_out_all, j, xf, g, b, alpha,
                               tiles["ln_m"], tiles["ln_k"], "ssd_out_proj_ln")


_TILES = dict(mm_m=1024, mm_n=1024, small_m=1024, ln_m=512, ln_k=2048, lru=256, attn=512,
              router=512, dispatch=512, expert_rows=1024, expert_ff=256, combine=256)


def kernel(x, ln_g, ln_b, hy_w_in, hy_conv_w, hy_conv_b, hy_w_r, hy_b_r, hy_w_i, hy_b_i, hy_lambda, hy_b_f, hy_w_out, ssd_w_in, ssd_conv_w, ssd_conv_b, ssd_dt_bias, ssd_a_log, ssd_d, ssd_norm_g, ssd_w_out, moe_w_r1, moe_b_r1, moe_w_r2, moe_b_r2, moe_w_gate, moe_w_up, moe_w_down):
    bn, s, d = x.shape
    depth = ln_g.shape[0]
    alpha = (2 * depth) ** 0.25
    ff = moe_w_gate.shape[-1]
    hy_w_in_t = jnp.swapaxes(hy_w_in, 1, 2)
    ssd_w_in_t = jnp.swapaxes(ssd_w_in, 1, 2)
    hy_w_out_b = hy_w_out.astype(BF16)
    ssd_w_out_b = ssd_w_out.astype(BF16)
    w_gate_s = moe_w_gate.reshape(depth * N_EXPERTS, d, ff)
    w_up_s = moe_w_up.reshape(depth * N_EXPERTS, d, ff)
    w_down_s = moe_w_down.reshape(depth * N_EXPERTS, ff, d)
    xf = x.reshape(bn * s, d)
    xb = xf.astype(BF16)
    for layer in range(depth):
        j = layer // 2
        if layer % 2 == 0:
            xf, xb = _hybrid_mixer_ln(xf, xb, bn, j, hy_w_in_t, hy_conv_w[j], hy_conv_b[j], hy_w_r[j], hy_b_r[j],
                                      hy_w_i[j], hy_b_i[j], hy_lambda[j], hy_b_f[j], hy_w_out_b,
                                      ln_g[layer, 0], ln_b[layer, 0], alpha, _TILES)
        else:
            xf, xb = _ssd_mixer_ln(xf, xb, bn, j, ssd_w_in_t, ssd_conv_w[j], ssd_conv_b[j], ssd_dt_bias[j],
                                   ssd_a_log[j], ssd_d[j], ssd_norm_g[j], ssd_w_out_b,
                                   ln_g[layer, 0], ln_b[layer, 0], alpha, _TILES)
        xf, xb = _moe_ln(xf, moe_w_r1[layer], moe_b_r1[layer], moe_w_r2[layer], moe_b_r2[layer],
                         w_gate_s, w_up_s, w_down_s, layer, ln_g[layer, 1], ln_b[layer, 1], alpha, _TILES)
    return xf.reshape(bn, s, d)
```

```python
import functools
import math

import jax
import jax.numpy as jnp
from jax import lax
from jax.experimental import pallas as pl
from jax.experimental.pallas import tpu as pltpu

F32 = jnp.float32
BF16 = jnp.bfloat16

_VMEM_LIMIT_BYTES = 56 * 1024 * 1024
_LANES = 128
_SUBLANES = 8

LN_EPS = 1e-5
RMS_EPS = 1e-5
LRU_C = 8.0
CONV_K = 4
SSD_CHUNK = 128
SSD_HEAD_DIM = 64
SSD_STATE = 128
MOE_GROUPS = 4
MOE_PER_GROUP = 4
N_EXPERTS = MOE_GROUPS * MOE_PER_GROUP
ROUTER_COLS = 128
ROUTER_L2_ROW0 = 8
LOG2E = math.log2(math.e)


def _params(*sem):
    return pltpu.CompilerParams(dimension_semantics=sem, vmem_limit_bytes=_VMEM_LIMIT_BYTES)


def _sigmoid(x):
    return 1.0 / (1.0 + jnp.exp(-x))


def _softplus(x):
    e = jnp.exp(-jnp.abs(x))
    u = 1.0 + e
    return jnp.maximum(x, 0.0) + jnp.where(u == 1.0, e, jnp.log(u) * (e / (u - 1.0)))


def _shift_rows(x, tail, d):
    r = pltpu.roll(x, d, axis=0)
    t = pltpu.roll(tail, d, axis=0)
    row = lax.broadcasted_iota(jnp.int32, tail.shape, 0)
    first = jnp.where(row < d, t, r[:_SUBLANES])
    return jnp.concatenate([first, r[_SUBLANES:]], axis=0)


def _causal_conv(x, tail, w_ref, b_ref):
    acc = b_ref[...] + w_ref[CONV_K - 1:CONV_K, :] * x
    for d in range(1, CONV_K):
        acc = acc + w_ref[CONV_K - 1 - d:CONV_K - d, :] * _shift_rows(x, tail, d)
    return acc


def _cumsum_rows(x):
    n = x.shape[0]
    row = lax.broadcasted_iota(jnp.int32, x.shape, 0)
    d = 1
    while d < n:
        x = x + jnp.where(row >= d, pltpu.roll(x, d, axis=0), 0.0)
        d *= 2
    return x


def _cumsum_lanes(x):
    n = x.shape[1]
    lane = lax.broadcasted_iota(jnp.int32, x.shape, 1)
    d = 1
    while d < n:
        x = x + jnp.where(lane >= d, pltpu.roll(x, d, axis=1), 0.0)
        d *= 2
    return x


def _layer_norm(s, g, b):
    mu = jnp.mean(s, axis=-1, keepdims=True)
    d = s - mu
    var = jnp.mean(d * d, axis=-1, keepdims=True)
    return d * lax.rsqrt(var + LN_EPS) * g + b


def _proj_small_body(x_ref, wt_ref, o_ref):
    o_ref[...] = lax.dot_general(x_ref[...], wt_ref[...].astype(BF16), (((1,), (1,)), ((), ())),
                                 preferred_element_type=F32)


def _proj_small(x, w_t, tm, name):
    m, k = x.shape
    tm = min(tm, m)
    w_t = jnp.pad(w_t, ((0, _LANES - w_t.shape[0]), (0, 0)))
    return pl.pallas_call(
        _proj_small_body,
        grid=(m // tm,),
        in_specs=[pl.BlockSpec((tm, k), lambda i: (i, 0)),
                  pl.BlockSpec((_LANES, k), lambda i: (0, 0))],
        out_specs=pl.BlockSpec((tm, _LANES), lambda i: (i, 0)),
        out_shape=jax.ShapeDtypeStruct((m, _LANES), F32),
        compiler_params=_params("parallel"),
        name=name,
    )(x, w_t)


def _proj_w32_body(x_ref, w_ref, *rest, epilogue, tiles_per_seq, chunk, scale_cols, scale):
    conv = epilogue.startswith("conv")
    if conv:
        cw_ref, cb_ref, o_ref, wb_ref, carry_ref = rest
    else:
        o_ref, wb_ref = rest
    j = pl.program_id(0)
    t = pl.program_id(1)
    tm = x_ref.shape[0]

    @pl.when(t == 0)
    def _():
        wb_ref[...] = w_ref[...].T.astype(BF16)

    if conv:
        @pl.when(t % tiles_per_seq == 0)
        def _():
            carry_ref[...] = jnp.zeros_like(carry_ref)

    x = x_ref[...]
    for c0 in range(0, o_ref.shape[1], chunk):
        cols = slice(c0, c0 + chunk)
        acc = jnp.dot(x, wb_ref[:, cols], preferred_element_type=F32)
        if conv:
            tail = carry_ref[:, cols]
            carry_ref[:, cols] = acc[tm - _SUBLANES:, :]
            acc = _causal_conv(acc, tail, cw_ref.at[:, cols], cb_ref.at[:, cols])
        if epilogue.endswith("silu"):
            acc = acc * _sigmoid(acc)
        elif epilogue == "gelu":
            acc = jax.nn.gelu(acc)
        elif epilogue == "scale":
            acc = acc * jnp.where(j < scale_cols, scale, 1.0)
        o_ref[:, cols] = acc.astype(o_ref.dtype)


def _proj_w32(x, w_all, layer, col0, n, out_dtype, tm, tn, name, epilogue="none", conv=None, seq_len=None,
              scale_cols=0, scale=1.0, chunk=256):
    m, k = x.shape
    tm, tn = min(tm, m), min(tn, n)
    assert col0 % tn == 0 and n % tn == 0 and m % tm == 0 and tn % chunk == 0
    jb = col0 // tn
    in_specs = [pl.BlockSpec((tm, k), lambda j, t: (t, 0)),
                pl.BlockSpec((None, tn, k), lambda j, t: (layer, jb + j, 0))]
    args = [x, w_all]
    scratch = [pltpu.VMEM((k, tn), BF16)]
    tiles_per_seq = 1
    if conv is not None:
        assert seq_len % tm == 0
        tiles_per_seq = seq_len // tm
        in_specs += [pl.BlockSpec((CONV_K, tn), lambda j, t: (0, j)),
                     pl.BlockSpec((1, tn), lambda j, t: (0, j))]
        args += [conv[0], conv[1].reshape(1, n)]
        scratch.append(pltpu.VMEM((_SUBLANES, tn), F32))
    return pl.pallas_call(
        functools.partial(_proj_w32_body, epilogue=epilogue, tiles_per_seq=tiles_per_seq, chunk=chunk,
                          scale_cols=scale_cols, scale=scale),
        grid=(n // tn, m // tm),
        in_specs=in_specs,
        out_specs=pl.BlockSpec((tm, tn), lambda j, t: (t, j)),
        out_shape=jax.ShapeDtypeStruct((m, n), out_dtype),
        scratch_shapes=scratch,
        compiler_params=_params("arbitrary", "arbitrary"),
        name=name,
    )(*args)


def _mm_ln_body(*refs, alpha, n_a):
    a_refs = refs[:n_a]
    w_ref, x_ref, g_ref, b_ref, of_ref, ob_ref, acc_ref = refs[n_a:]
    k = pl.program_id(1)
    ka = a_refs[0].shape[1]
    part = jnp.dot(a_refs[0][...], w_ref[:ka, :], preferred_element_type=F32)
    for idx in range(1, n_a):
        part = part + jnp.dot(a_refs[idx][...], w_ref[idx * ka:(idx + 1) * ka, :], preferred_element_type=F32)

    @pl.when(k == 0)
    def _():
        acc_ref[...] = part

    @pl.when(k > 0)
    def _():
        acc_ref[...] += part

    @pl.when(k == pl.num_programs(1) - 1)
    def _():
        y = _layer_norm(alpha * x_ref[...] + acc_ref[...], g_ref[...], b_ref[...])
        of_ref[...] = y
        ob_ref[...] = y.astype(BF16)


def _matmul_residual_ln(a_list, w_all, layer, x, g, b, alpha, tm, tk, name):
    n_a = len(a_list)
    m, ka = a_list[0].shape
    n = w_all.shape[2]
    tm = min(tm, m)
    tka = ka if n_a > 1 else min(tk, ka)
    nk = ka // tka
    return pl.pallas_call(
        functools.partial(_mm_ln_body, alpha=alpha, n_a=n_a),
        grid=(m // tm, nk),
        in_specs=[pl.BlockSpec((tm, tka), lambda i, k: (i, k))] * n_a + [
                  pl.BlockSpec((None, n_a * tka, n), lambda i, k: (layer, k, 0)),
                  pl.BlockSpec((tm, n), lambda i, k: (i, 0)),
                  pl.BlockSpec((1, n), lambda i, k: (0, 0)),
                  pl.BlockSpec((1, n), lambda i, k: (0, 0))],
        out_specs=[pl.BlockSpec((tm, n), lambda i, k: (i, 0)),
                   pl.BlockSpec((tm, n), lambda i, k: (i, 0))],
        out_shape=[jax.ShapeDtypeStruct((m, n), F32), jax.ShapeDtypeStruct((m, n), BF16)],
        scratch_shapes=[pltpu.VMEM((tm, n), F32)],
        compiler_params=_params("parallel", "arbitrary"),
        name=name,
    )(*a_list, w_all, x, g.reshape(1, n), b.reshape(1, n))


def _fox_prep_body(fl_ref, bf_ref, f_ref):
    x = fl_ref[0] + bf_ref[...]
    f_ref[0] = _cumsum_lanes(-_softplus(-x)) * LOG2E


def _fox_prep(fl_t, b_f):
    bn, h, s = fl_t.shape
    return pl.pallas_call(
        _fox_prep_body,
        grid=(bn,),
        in_specs=[pl.BlockSpec((1, h, s), lambda b: (b, 0, 0)),
                  pl.BlockSpec((h, 1), lambda b: (0, 0))],
        out_specs=pl.BlockSpec((1, h, s), lambda b: (b, 0, 0)),
        out_shape=jax.ShapeDtypeStruct((bn, h, s), F32),
        compiler_params=_params("parallel"),
        name="fox_prep",
    )(fl_t, b_f.reshape(h, 1))


def _lru_body(xc_ref, gg_ref, wr_ref, br_ref, wi_ref, bi_ref, lam_ref, y_ref, h_ref, acum_ref, hloc_ref,
              *, heads, block):
    @pl.when(pl.program_id(1) == 0)
    def _():
        h_ref[...] = jnp.zeros_like(h_ref)

    xcb = xc_ref[0]
    xc = xcb.astype(F32)
    ts = xc.shape[0]
    r_parts, i_parts = [], []
    for hd in range(heads):
        xh = xcb[:, hd * block:(hd + 1) * block]
        r_parts.append(jnp.dot(xh, wr_ref[hd], preferred_element_type=F32))
        i_parts.append(jnp.dot(xh, wi_ref[hd], preferred_element_type=F32))
    r = _sigmoid(jnp.concatenate(r_parts, axis=1) + br_ref[...])
    ig = _sigmoid(jnp.concatenate(i_parts, axis=1) + bi_ref[...])
    log_a = -LRU_C * r * _softplus(-lam_ref[...])
    a = jnp.exp(log_a)
    u = jnp.sqrt(-jnp.tanh(log_a) * (a * a + 1.0)) * (ig * xc)

    width = a.shape[1]
    a3 = a.reshape(ts // _SUBLANES, _SUBLANES, width)
    u3 = u.reshape(ts // _SUBLANES, _SUBLANES, width)
    row = lax.broadcasted_iota(jnp.int32, a3.shape, 1)
    d = 1
    while d < _SUBLANES:
        keep = row >= d
        u3 = u3 + a3 * jnp.where(keep, pltpu.roll(u3, d, axis=1), 0.0)
        a3 = a3 * jnp.where(keep, pltpu.roll(a3, d, axis=1), 1.0)
        d *= 2
    acum_ref[...] = a3
    hloc_ref[...] = u3

    def carry_group(g, h_prev):
        hg = hloc_ref[g] + acum_ref[g] * h_prev
        hloc_ref[g] = hg
        return hg[_SUBLANES - 1:_SUBLANES, :]

    h_ref[...] = lax.fori_loop(0, ts // _SUBLANES, carry_group, h_ref[...], unroll=4)
    h = hloc_ref[...].reshape(ts, width)
    y_ref[0] = (h * gg_ref[0].astype(F32)).astype(y_ref.dtype)


def _lru(xc, gg, w_r, b_r, w_i, b_i, lam, ts):
    bn, s, width = xc.shape
    heads, block, _ = w_r.shape
    ts = min(ts, s)
    vec = lambda v: v.reshape(1, width)
    const2 = lambda b, i: (0, 0)
    const3 = lambda b, i: (0, 0, 0)
    tile = pl.BlockSpec((1, ts, width), lambda b, i: (b, i, 0))
    return pl.pallas_call(
        functools.partial(_lru_body, heads=heads, block=block),
        grid=(bn, s // ts),
        in_specs=[tile, tile,
                  pl.BlockSpec((heads, block, block), const3),
                  pl.BlockSpec((1, width), const2),
                  pl.BlockSpec((heads, block, block), const3),
                  pl.BlockSpec((1, width), const2),
                  pl.BlockSpec((1, width), const2)],
        out_specs=tile,
        out_shape=jax.ShapeDtypeStruct((bn, s, width), BF16),
        scratch_shapes=[pltpu.VMEM((1, width), F32),
                        pltpu.VMEM((ts // _SUBLANES, _SUBLANES, width), F32),
                        pltpu.VMEM((ts // _SUBLANES, _SUBLANES, width), F32)],
        compiler_params=_params("parallel", "arbitrary"),
        name="rg_lru",
    )(xc, gg, w_r.astype(BF16), vec(b_r), w_i.astype(BF16), vec(b_i), vec(lam))


def _split3(f):
    hi = f.astype(BF16).astype(F32)
    r = f - hi
    mid = r.astype(BF16).astype(F32)
    return hi, mid, r - mid


def _fox_attn_body(q_ref, k_ref, v_ref, fc_ref, o_ref, kx_ref, *, tq):
    qi = pl.program_id(2)
    s_len = k_ref.shape[1]
    dh = q_ref.shape[2]

    @pl.when(qi == 0)
    def _():
        for c0 in range(0, s_len, tq):
            hi, mid, lo = _split3(fc_ref[0, 0, c0:c0 + tq, :])
            lane = lax.broadcasted_iota(jnp.int32, (tq, _LANES), 1)
            kx = jnp.where(lane < 3, 1.0, jnp.where(lane == 3, -hi, jnp.where(lane == 4, -mid,
                           jnp.where(lane == 5, -lo, 0.0))))
            kx_ref[c0:c0 + tq, :] = kx.astype(BF16)

    hi, mid, lo = _split3(fc_ref[0, 0, pl.ds(pl.multiple_of(qi * tq, tq), tq), :])
    lane = lax.broadcasted_iota(jnp.int32, (tq, _LANES), 1)
    qx = jnp.where(lane == 0, hi, jnp.where(lane == 1, mid, jnp.where(lane == 2, lo,
                   jnp.where(lane < 6, 1.0, 0.0))))
    q = jnp.concatenate([q_ref[0], qx.astype(BF16)], axis=1)

    def scores(j):
        start = pl.multiple_of(j * tq, tq)
        k = jnp.concatenate([k_ref[0, pl.ds(start, tq), :], kx_ref[pl.ds(start, tq), :]], axis=1)
        return lax.dot_general(q, k, (((1,), (1,)), ((), ())), preferred_element_type=F32), start

    def update(carry, sc, start):
        m, l, acc = carry
        m_new = jnp.maximum(m, jnp.max(sc, axis=1, keepdims=True))
        corr = jnp.exp2(m - m_new)
        p = jnp.exp2(sc - m_new)
        l = corr * l + jnp.sum(p, axis=1, keepdims=True)
        v = v_ref[0, pl.ds(start, tq), :]
        acc = corr * acc + jnp.dot(p.astype(BF16), v, preferred_element_type=F32)
        return m_new, l, acc

    def diagonal(carry):
        sc, start = scores(qi)
        rq = lax.broadcasted_iota(jnp.int32, sc.shape, 0)
        ck = lax.broadcasted_iota(jnp.int32, sc.shape, 1)
        return update(carry, jnp.where(ck <= rq, sc, -1e30), start)

    def two_blocks(jj, carry):
        sc_a, start_a = scores(2 * jj)
        sc_b, start_b = scores(2 * jj + 1)
        return update(update(carry, sc_a, start_a), sc_b, start_b)

    def odd_tail(carry):
        sc_a, start_a = scores(qi - 1)
        return diagonal(update(carry, sc_a, start_a))

    init = (jnp.full((tq, 1), -1e30, F32), jnp.zeros((tq, 1), F32), jnp.zeros((tq, dh), F32))
    carry = lax.fori_loop(0, qi // 2, two_blocks, init)
    _, l, acc = lax.cond(qi % 2 == 1, odd_tail, diagonal, carry)
    o_ref[0] = (acc / l).astype(o_ref.dtype)


def _fox_attention(qkv, f2, heads, dh, tq):
    bn, s, _ = qkv.shape
    tq = min(tq, s)
    return pl.pallas_call(
        functools.partial(_fox_attn_body, tq=tq),
        grid=(bn, heads, s // tq),
        in_specs=[pl.BlockSpec((1, tq, dh), lambda b, h, i: (b, i, h)),
                  pl.BlockSpec((1, s, dh), lambda b, h, i: (b, 0, heads + h)),
                  pl.BlockSpec((1, s, dh), lambda b, h, i: (b, 0, 2 * heads + h)),
                  pl.BlockSpec((1, 1, s, 1), lambda b, h, i: (b, h, 0, 0))],
        out_specs=pl.BlockSpec((1, tq, dh), lambda b, h, i: (b, i, h)),
        out_shape=jax.ShapeDtypeStruct((bn, s, heads * dh), BF16),
        scratch_shapes=[pltpu.VMEM((s, _LANES), BF16)],
        compiler_params=_params("parallel", "parallel", "arbitrary"),
        name="fox_attention",
    )(qkv, qkv, qkv, f2[..., None])


def _ssd_body(xbc_ref, z_ref, dt_ref, dtb_ref, alog_ref, dexp_ref, ng_ref, o_ref,
              state_ref, acst_ref, dtt_ref, wt_ref, *, groups, inner):
    L = SSD_CHUNK
    gw = inner // groups
    pairs = gw // _LANES
    hpg = gw // SSD_HEAD_DIM

    @pl.when(pl.program_id(1) == 0)
    def _():
        state_ref[...] = jnp.zeros_like(state_ref)

    dt = _softplus(dt_ref[0] + dtb_ref[...])
    acs = _cumsum_rows(dt * (-jnp.exp(alog_ref[...]))) * LOG2E
    acs_t = acs.T
    dt_t = dt.T
    acst_ref[...] = acs_t
    dtt_ref[...] = dt_t
    wt_ref[...] = dt_t * jnp.exp2(acs_t[:, L - 1:L] - acs_t)
    dec_last = jnp.exp2(acs[L - 1:L, :])

    rq = lax.broadcasted_iota(jnp.int32, (L, L), 0)
    ck = lax.broadcasted_iota(jnp.int32, (L, L), 1)
    causal = ck <= rq
    lo = lax.broadcasted_iota(jnp.int32, (L, _LANES), 1) < SSD_HEAD_DIM
    lo_row = lo[:1]

    for g in range(groups):
        bmat = xbc_ref[0, :, inner + g * SSD_STATE:inner + (g + 1) * SSD_STATE]
        cmat = xbc_ref[0, :, inner + (groups + g) * SSD_STATE:inner + (groups + g + 1) * SSD_STATE]
        cbm = lax.dot_general(cmat, bmat, (((1,), (1,)), ((), ())), preferred_element_type=F32)
        cmat = cmat.astype(F32)
        b_t = bmat.astype(F32).T
        ys = []
        for j in range(pairs):
            col0 = g * gw + j * _LANES
            xs_p = xbc_ref[0, :, col0:col0 + _LANES].astype(F32)
            m_parts, c_parts, bw_parts, dl = [], [], [], []
            for e in range(2):
                hd = g * hpg + 2 * j + e
                col = jnp.broadcast_to(acs[:, hd:hd + 1], (L, L))
                row = acst_ref[hd:hd + 1, :]
                dec = jnp.exp2(jnp.where(causal, col - row, -jnp.inf))
                m_parts.append((cbm * dec * dtt_ref[hd:hd + 1, :]).astype(BF16))
                c_parts.append((cmat * jnp.exp2(col)).astype(BF16))
                bw_parts.append((b_t * wt_ref[hd:hd + 1, :]).astype(BF16))
                dl.append(dec_last[:, hd:hd + 1])
            prev = state_ref[:, col0:col0 + _LANES]
            x_top = jnp.where(lo, xs_p, 0.0).astype(BF16)
            x_bot = jnp.where(lo, 0.0, xs_p).astype(BF16)
            p_top = jnp.where(lo, prev, 0.0).astype(BF16)
            p_bot = jnp.where(lo, 0.0, prev).astype(BF16)
            lhs = jnp.concatenate(m_parts + c_parts, axis=1)
            rhs = jnp.concatenate([x_top, x_bot, p_top, p_bot], axis=0)
            y = jnp.dot(lhs, rhs, preferred_element_type=F32)
            st = jnp.dot(jnp.concatenate(bw_parts, axis=1), jnp.concatenate([x_top, x_bot], axis=0),
                         preferred_element_type=F32)
            state_ref[:, col0:col0 + _LANES] = prev * jnp.where(lo_row, dl[0], dl[1]) + st
            y = y + dexp_ref[:, col0:col0 + _LANES] * xs_p
            ys.append(y * z_ref[0, :, col0:col0 + _LANES].astype(F32))
        yg = jnp.concatenate(ys, axis=1)
        ms = jnp.mean(yg * yg, axis=1, keepdims=True)
        gcols = slice(g * gw, (g + 1) * gw)
        o_ref[0, :, gcols] = (yg * lax.rsqrt(ms + RMS_EPS) * ng_ref[:, gcols]).astype(o_ref.dtype)


def _ssd_core(xbc, zs, dt, dt_bias, a_log, d_skip, norm_g, groups):
    bn, s, conv_dim = xbc.shape
    inner = zs.shape[2]
    heads = a_log.shape[0]
    L = SSD_CHUNK
    pad = lambda v: jnp.pad(v, (0, _LANES - heads)).reshape(1, _LANES)
    const2 = lambda b, c: (0, 0)
    return pl.pallas_call(
        functools.partial(_ssd_body, groups=groups, inner=inner),
        grid=(bn, s // L),
        in_specs=[pl.BlockSpec((1, L, conv_dim), lambda b, c: (b, c, 0)),
                  pl.BlockSpec((1, L, inner), lambda b, c: (b, c, 0)),
                  pl.BlockSpec((1, L, _LANES), lambda b, c: (b, c, 0)),
                  pl.BlockSpec((1, _LANES), const2),
                  pl.BlockSpec((1, _LANES), const2),
                  pl.BlockSpec((1, inner), const2),
                  pl.BlockSpec((1, inner), const2)],
        out_specs=pl.BlockSpec((1, L, inner), lambda b, c: (b, c, 0)),
        out_shape=jax.ShapeDtypeStruct((bn, s, inner), BF16),
        scratch_shapes=[pltpu.VMEM((SSD_STATE, inner), F32),
                        pltpu.VMEM((_LANES, L), F32),
                        pltpu.VMEM((_LANES, L), F32),
                        pltpu.VMEM((_LANES, L), F32)],
        compiler_params=_params("parallel", "arbitrary"),
        name="ssd_core",
    )(xbc, zs, dt, pad(dt_bias), pad(a_log),
      jnp.repeat(d_skip, SSD_HEAD_DIM).reshape(1, inner), norm_g.reshape(1, inner))


def _router_body(x_ref, whl_ref, wh_ref, b_ref, oi_ref, of_ref, cnt_ref, base_ref):
    i = pl.program_id(0)

    @pl.when(i == 0)
    def _():
        base_ref[...] = jnp.zeros_like(base_ref)

    x = x_ref[...]
    tm = x.shape[0]
    xh = x.astype(BF16)
    xl = (x - xh.astype(F32)).astype(BF16)
    r = jnp.dot(xh, whl_ref[...], preferred_element_type=F32)
    logits = (r[:, :ROUTER_COLS] + r[:, ROUTER_COLS:]
              + jnp.dot(xl, wh_ref[...], preferred_element_type=F32) + b_ref[...])
    lt = logits.T

    row = lax.broadcasted_iota(jnp.int32, (_SUBLANES, tm), 0)
    neg = -jnp.inf
    l1 = jnp.where(row < MOE_GROUPS, lt[:_SUBLANES], neg)
    e1 = jnp.exp(l1 - jnp.max(l1, axis=0, keepdims=True))
    p1 = e1 / jnp.sum(e1, axis=0, keepdims=True)
    pg = jnp.max(p1, axis=0, keepdims=True)
    gidx = jnp.min(jnp.where(p1 == pg, row, _SUBLANES), axis=0, keepdims=True)

    sel = jnp.full((_SUBLANES, tm), neg, F32)
    for gi in range(MOE_GROUPS):
        r0 = ROUTER_L2_ROW0 + _SUBLANES * gi
        sel = jnp.where((gidx == gi) & (row < MOE_PER_GROUP), lt[r0:r0 + _SUBLANES], sel)
    va = jnp.max(sel, axis=0, keepdims=True)
    ia = jnp.min(jnp.where(sel == va, row, _SUBLANES), axis=0, keepdims=True)
    sel_b = jnp.where(row == ia, neg, sel)
    vb = jnp.max(sel_b, axis=0, keepdims=True)
    ib = jnp.min(jnp.where(sel_b == vb, row, _SUBLANES), axis=0, keepdims=True)
    t = jnp.exp(vb - va)
    ca = pg * (1.0 / (1.0 + t))
    cb = pg * (t / (1.0 + t))
    ea = gidx * MOE_PER_GROUP + ia
    eb = gidx * MOE_PER_GROUP + ib

    erow = lax.broadcasted_iota(jnp.int32, (N_EXPERTS, tm), 0)
    hit_a = erow == ea
    hit_b = erow == eb
    oh = jnp.where(hit_a, 1.0, 0.0) + jnp.where(hit_b, 1.0, 0.0)
    before = (lax.broadcasted_iota(jnp.int32, (tm, tm), 0)
              < lax.broadcasted_iota(jnp.int32, (tm, tm), 1))
    prefix = jnp.dot(oh.astype(BF16), jnp.where(before, 1.0, 0.0).astype(BF16),
                     preferred_element_type=F32) + base_ref[...]
    rank_a = jnp.sum(jnp.where(hit_a, prefix, 0.0), axis=0, keepdims=True)
    rank_b = jnp.sum(jnp.where(hit_b, prefix, 0.0), axis=0, keepdims=True)
    base_ref[...] += jnp.sum(oh, axis=1, keepdims=True)
    cnt_ref[...] = jnp.broadcast_to(base_ref[...], cnt_ref.shape)

    zi = jnp.zeros((1, tm), jnp.int32)
    oi_ref[...] = jnp.concatenate(
        [ea, eb, rank_a.astype(jnp.int32), rank_b.astype(jnp.int32), zi, zi, zi, zi], axis=0)
    zf = jnp.zeros((1, tm), F32)
    of_ref[...] = jnp.concatenate([ca, cb, zf, zf, zf, zf, zf, zf], axis=0)


def _router(x, w_r1, b_r1, w_r2, b_r2, tm):
    t, d = x.shape
    tm = min(tm, t)
    w = jnp.zeros((d, ROUTER_COLS), F32).at[:, :MOE_GROUPS].set(w_r1)
    b = jnp.zeros((ROUTER_COLS,), F32).at[:MOE_GROUPS].set(b_r1)
    for gi in range(MOE_GROUPS):
        c0 = ROUTER_L2_ROW0 + _SUBLANES * gi
        w = w.at[:, c0:c0 + MOE_PER_GROUP].set(w_r2[gi])
        b = b.at[c0:c0 + MOE_PER_GROUP].set(b_r2[gi])
    wh = w.astype(BF16)
    wl = (w - wh.astype(F32)).astype(BF16)
    const = lambda i: (0, 0)
    return pl.pallas_call(
        _router_body,
        grid=(t // tm,),
        in_specs=[pl.BlockSpec((tm, d), lambda i: (i, 0)),
                  pl.BlockSpec((d, 2 * ROUTER_COLS), const),
                  pl.BlockSpec((d, ROUTER_COLS), const),
                  pl.BlockSpec((1, ROUTER_COLS), const)],
        out_specs=[pl.BlockSpec((_SUBLANES, tm), lambda i: (0, i)),
                   pl.BlockSpec((_SUBLANES, tm), lambda i: (0, i)),
                   pl.BlockSpec((N_EXPERTS, _LANES), const)],
        out_shape=[jax.ShapeDtypeStruct((_SUBLANES, t), jnp.int32),
                   jax.ShapeDtypeStruct((_SUBLANES, t), F32),
                   jax.ShapeDtypeStruct((N_EXPERTS, _LANES), F32)],
        scratch_shapes=[pltpu.VMEM((N_EXPERTS, 1), F32)],
        compiler_params=_params("arbitrary"),
        name="moe_router",
    )(x, jnp.concatenate([wh, wl], axis=1), wh, b.reshape(1, ROUTER_COLS))


def _row_copy(src, src_row, dst, dst_row, sem):
    return pltpu.make_async_copy(src.at[pl.ds(src_row, 1)], dst.at[pl.ds(dst_row, 1)], sem)


def _dispatch_body(pos_ref, zt_ref, x_ref, xs_ref, zeros_ref, sem, zsem, *, td, tz, t_total):
    step = pl.program_id(0)
    base = step * td

    def zero_tile(i):
        return pltpu.make_async_copy(zeros_ref, xs_ref.at[pl.ds(pl.multiple_of(i * tz, tz), tz)], zsem)

    @pl.when(step == 0)
    def _():
        zeros_ref[...] = jnp.zeros_like(zeros_ref)

        def start(i, carry):
            @pl.when(zt_ref[i] == 1)
            def _():
                zero_tile(i).start()
            return carry

        def wait(i, carry):
            @pl.when(zt_ref[i] == 1)
            def _():
                zero_tile(i).wait()
            return carry

        lax.fori_loop(0, zt_ref.shape[0], start, 0)
        lax.fori_loop(0, zt_ref.shape[0], wait, 0)

    def issue(r, carry):
        _row_copy(x_ref, r, xs_ref, pos_ref[base + r], sem).start()
        _row_copy(x_ref, r, xs_ref, pos_ref[t_total + base + r], sem).start(priority=1)
        return carry

    lax.fori_loop(0, td, issue, 0, unroll=4)
    tile_rows = xs_ref.at[pl.ds(0, td)]
    pltpu.make_async_copy(x_ref, tile_rows, sem).wait()
    pltpu.make_async_copy(x_ref, tile_rows, sem).wait()


def _dispatch(x, pos, zero_tiles, n_rows, td, tz):
    t, d = x.shape
    td = min(td, t)
    return pl.pallas_call(
        functools.partial(_dispatch_body, td=td, tz=tz, t_total=t),
        grid_spec=pltpu.PrefetchScalarGridSpec(
            num_scalar_prefetch=2,
            grid=(t // td,),
            in_specs=[pl.BlockSpec((td, d), lambda i, pos, zt: (i, 0))],
            out_specs=pl.BlockSpec(memory_space=pl.ANY),
            scratch_shapes=[pltpu.VMEM((tz, d), x.dtype),
                            pltpu.SemaphoreType.DMA(()), pltpu.SemaphoreType.DMA(())]),
        out_shape=jax.ShapeDtypeStruct((n_rows, d), x.dtype),
        compiler_params=_params("arbitrary"),
        name="moe_dispatch",
    )(pos, zero_tiles, x)


def _expert_body(blk_ref, te_ref, nv_ref, xs_ref, wg_ref, wu_ref, wd_ref, o_ref):
    del blk_ref, te_ref
    valid = pl.program_id(0) < nv_ref[0]

    @pl.when(jnp.logical_not(valid))
    def _():
        o_ref[...] = jnp.zeros_like(o_ref)

    @pl.when(valid)
    def _():
        x = xs_ref[...].astype(BF16)
        g = jnp.dot(x, wg_ref[0], preferred_element_type=F32)
        u = jnp.dot(x, wu_ref[0], preferred_element_type=F32)
        hid = (g * _sigmoid(g)) * u
        o_ref[...] = jnp.dot(hid.astype(BF16), wd_ref[0], preferred_element_type=F32)


def _experts(xs, w_gate, w_up, w_down, expert0, blk, tile_expert, n_valid, tm):
    p, d = xs.shape
    ff = w_gate.shape[2]
    n_tiles = p // tm
    rows = lambda i, blk, te, nv: (blk[i], 0)
    wsel = lambda i, blk, te, nv: (expert0 + te[i], 0, 0)
    return pl.pallas_call(
        _expert_body,
        grid_spec=pltpu.PrefetchScalarGridSpec(
            num_scalar_prefetch=3,
            grid=(n_tiles,),
            in_specs=[pl.BlockSpec((tm, d), rows),
                      pl.BlockSpec((1, d, ff), wsel),
                      pl.BlockSpec((1, d, ff), wsel),
                      pl.BlockSpec((1, ff, d), wsel)],
            out_specs=pl.BlockSpec((tm, d), lambda i, blk, te, nv: (i, 0))),
        out_shape=jax.ShapeDtypeStruct((p, d), F32),
        compiler_params=_params("arbitrary"),
        name="moe_experts",
    )(blk, tile_expert, n_valid, xs, w_gate, w_up, w_down)


def _combine_body(pos_ref, ys_ref, x_ref, c_ref, g_ref, b_ref, of_ref, ob_ref, buf_a, buf_b, sem,
                  *, tc, t_total, alpha):
    step = pl.program_id(0)
    slot = step % 2

    def issue(tile, which):
        base = tile * tc

        def body(r, carry):
            _row_copy(ys_ref, pos_ref[base + r], buf_a.at[which], r, sem.at[which]).start()
            _row_copy(ys_ref, pos_ref[t_total + base + r], buf_b.at[which], r, sem.at[which]).start(priority=1)
            return carry

        lax.fori_loop(0, tc, body, 0, unroll=4)

    @pl.when(step == 0)
    def _():
        issue(0, 0)

    @pl.when(step + 1 < pl.num_programs(0))
    def _():
        issue(step + 1, 1 - slot)

    tile_rows = ys_ref.at[pl.ds(0, tc)]
    pltpu.make_async_copy(tile_rows, buf_a.at[slot], sem.at[slot]).wait()
    pltpu.make_async_copy(tile_rows, buf_b.at[slot], sem.at[slot]).wait()
    c = c_ref[...]
    f = c[:, 0:1] * buf_a[slot] + c[:, 1:2] * buf_b[slot]
    y = _layer_norm(alpha * x_ref[...] + f, g_ref[...], b_ref[...])
    of_ref[...] = y
    ob_ref[...] = y.astype(BF16)


def _combine_ln(ys, pos, x, c, g, b, alpha, tc):
    t, d = x.shape
    tc = min(tc, t)
    const = lambda i, pos: (0, 0)
    tile = lambda i, pos: (i, 0)
    return pl.pallas_call(
        functools.partial(_combine_body, tc=tc, t_total=t, alpha=alpha),
        grid_spec=pltpu.PrefetchScalarGridSpec(
            num_scalar_prefetch=1,
            grid=(t // tc,),
            in_specs=[pl.BlockSpec(memory_space=pl.ANY),
                      pl.BlockSpec((tc, d), tile),
                      pl.BlockSpec((tc, 2), tile),
                      pl.BlockSpec((1, d), const),
                      pl.BlockSpec((1, d), const)],
            out_specs=[pl.BlockSpec((tc, d), tile), pl.BlockSpec((tc, d), tile)],
            scratch_shapes=[pltpu.VMEM((2, tc, d), F32), pltpu.VMEM((2, tc, d), F32),
                            pltpu.SemaphoreType.DMA((2,))]),
        out_shape=[jax.ShapeDtypeStruct((t, d), F32), jax.ShapeDtypeStruct((t, d), BF16)],
        compiler_params=_params("arbitrary"),
        name="moe_combine_ln",
    )(pos, ys, x, c, g.reshape(1, d), b.reshape(1, d))


def _moe_ln(x, w_r1, b_r1, w_r2, b_r2, w_gate, w_up, w_down, layer, g, b, alpha, tiles):
    t, d = x.shape
    tm_e = min(tiles["expert_rows"], t)
    oi, of, cnt = _router(x, w_r1, b_r1, w_r2, b_r2, tiles["router"])
    counts = cnt[:, 0].astype(jnp.int32)
    seg_tiles = (counts + tm_e - 1) // tm_e
    tile_end = jnp.cumsum(seg_tiles)
    seg_start = (tile_end - seg_tiles) * tm_e
    start_of = lambda e: jnp.sum(jnp.where(e[:, None] == jnp.arange(N_EXPERTS, dtype=jnp.int32)[None, :],
                                           seg_start[None, :], 0), axis=1)
    pos = jnp.concatenate([start_of(oi[0]) + oi[2], start_of(oi[1]) + oi[3]])
    n_tiles = 2 * t // tm_e + N_EXPERTS
    n_valid = tile_end[-1:]
    tile_id = jnp.arange(n_tiles, dtype=jnp.int32)
    blk = jnp.minimum(tile_id, n_valid - 1)
    tile_expert = jnp.minimum(jnp.sum((blk[:, None] >= tile_end[None, :]).astype(jnp.int32), axis=1), N_EXPERTS - 1)
    zero_tiles = (jnp.any(tile_id[:, None] == (tile_end - 1)[None, :], axis=1) | (tile_id >= n_valid)).astype(jnp.int32)

    xs = _dispatch(x, pos, zero_tiles, n_tiles * tm_e, tiles["dispatch"], tm_e)
    ys = _experts(xs, w_gate, w_up, w_down, layer * N_EXPERTS, blk, tile_expert, n_valid, tm_e)
    return _combine_ln(ys, pos, x, of[:2].T, g, b, alpha, tiles["combine"])


def _hybrid_mixer_ln(xf, xb, bn, j, w_in_all, conv_w, conv_b, w_r, b_r, w_i, b_i, lam, b_f, w_out_all, g, b, alpha, tiles):
    t, d = xf.shape
    s = t // bn
    heads, block, _ = w_r.shape
    lru_w = heads * block
    fox_heads = b_f.shape[0]
    main = w_in_all.shape[1] - fox_heads
    fox_w = (main - 2 * lru_w) // 3
    dh = fox_w // fox_heads
    tm, tn = tiles["mm_m"], tiles["mm_n"]
    xc = _proj_w32(xb, w_in_all, j, 0, lru_w, BF16, tm, tn, "hy_lru_x_proj_conv", "conv", (conv_w, conv_b), s)
    gg = _proj_w32(xb, w_in_all, j, lru_w, lru_w, BF16, tm, tn, "hy_lru_gate_proj_gelu", "gelu")
    qkv = _proj_w32(xb, w_in_all, j, 2 * lru_w, 3 * fox_w, BF16, tm, tn, "hy_qkv_proj", "scale",
                     scale_cols=fox_w // min(tn, fox_w), scale=dh ** -0.5 * LOG2E)
    fl = _proj_small(xb, w_in_all[j, main:, :], tiles["small_m"], "hy_forget_proj")[:, :fox_heads]
    f2 = _fox_prep(fl.reshape(bn, s, fox_heads).transpose(0, 2, 1), b_f)
    y_lru = _lru(xc.reshape(bn, s, lru_w), gg.reshape(bn, s, lru_w), w_r, b_r, w_i, b_i, lam, tiles["lru"])
    y_att = _fox_attention(qkv.reshape(bn, s, 3 * fox_w), f2, fox_heads, dh, tiles["attn"])
    return _matmul_residual_ln([y_lru.reshape(t, lru_w), y_att.reshape(t, fox_w)], w_out_all, j, xf, g, b, alpha,
                               tiles["ln_m"], tiles["ln_k"], "hy_out_proj_ln")


def _ssd_mixer_ln(xf, xb, bn, j, w_in_all, conv_w, conv_b, dt_bias, a_log, d_skip, norm_g, w_out_all, g, b, alpha, tiles):
    t, d = xf.shape
    s = t // bn
    heads = a_log.shape[0]
    inner = norm_g.shape[0]
    conv_dim = conv_w.shape[1]
    groups = (conv_dim - inner) // (2 * SSD_STATE)
    tm, tn = tiles["mm_m"], tiles["mm_n"]
    zs = _proj_w32(xb, w_in_all, j, 0, inner, BF16, tm, tn, "ssd_z_proj_silu", "silu")
    xbc = _proj_w32(xb, w_in_all, j, inner, conv_dim, BF16, tm, tn, "ssd_xbc_proj_conv_silu", "conv_silu",
                     (conv_w, conv_b), s)
    dt = _proj_small(xb, w_in_all[j, inner + conv_dim:, :], tiles["small_m"], "ssd_dt_proj")
    y = _ssd_core(xbc.reshape(bn, s, conv_dim), zs.reshape(bn, s, inner), dt.reshape(bn, s, _LANES),
                  dt_bias, a_log, d_skip, norm_g, groups)
    return _matmul_residual_ln([y.reshape(t, inner)], w_out_all, j, xf, g, b, alpha,
                               tiles["ln_m"], tiles["ln_k"], "ssd_out_proj_ln")


_TILES = dict(mm_m=1024, mm_n=1024, small_m=1024, ln_m=512, ln_k=2048, lru=256, attn=512,
              router=512, dispatch=512, expert_rows=512, combine=256)


def kernel(x, ln_g, ln_b, hy_w_in, hy_conv_w, hy_conv_b, hy_w_r, hy_b_r, hy_w_i, hy_b_i, hy_lambda, hy_b_f, hy_w_out, ssd_w_in, ssd_conv_w, ssd_conv_b, ssd_dt_bias, ssd_a_log, ssd_d, ssd_norm_g, ssd_w_out, moe_w_r1, moe_b_r1, moe_w_r2, moe_b_r2, moe_w_gate, moe_w_up, moe_w_down):
    bn, s, d = x.shape
    depth = ln_g.shape[0]
    alpha = (2 * depth) ** 0.25
    ff = moe_w_gate.shape[-1]
    hy_w_in_t = jnp.swapaxes(hy_w_in, 1, 2)
    ssd_w_in_t = jnp.swapaxes(ssd_w_in, 1, 2)
    hy_w_out_b = hy_w_out.astype(BF16)
    ssd_w_out_b = ssd_w_out.astype(BF16)
    w_gate_s = moe_w_gate.reshape(depth * N_EXPERTS, d, ff).astype(BF16)
    w_up_s = moe_w_up.reshape(depth * N_EXPERTS, d, ff).astype(BF16)
    w_down_s = moe_w_down.reshape(depth * N_EXPERTS, ff, d).astype(BF16)
    xf = x.reshape(bn * s, d)
    xb = xf.astype(BF16)
    for layer in range(depth):
        j = layer // 2
        if layer % 2 == 0:
            xf, xb = _hybrid_mixer_ln(xf, xb, bn, j, hy_w_in_t, hy_conv_w[j], hy_conv_b[j], hy_w_r[j], hy_b_r[j],
                                      hy_w_i[j], hy_b_i[j], hy_lambda[j], hy_b_f[j], hy_w_out_b,
                                      ln_g[layer, 0], ln_b[layer, 0], alpha, _TILES)
        else:
            xf, xb = _ssd_mixer_ln(xf, xb, bn, j, ssd_w_in_t, ssd_conv_w[j], ssd_conv_b[j], ssd_dt_bias[j],
                                   ssd_a_log[j], ssd_d[j], ssd_norm_g[j], ssd_w_out_b,
                                   ln_g[layer, 0], ln_b[layer, 0], alpha, _TILES)
        xf, xb = _moe_ln(xf, moe_w_r1[layer], moe_b_r1[layer], moe_w_r2[layer], moe_b_r2[layer],
                         w_gate_s, w_up_s, w_down_s, layer, ln_g[layer, 1], ln_b[layer, 1], alpha, _TILES)
    return xf.reshape(bn, s, d)
```

```python
import functools
import math

import jax
import jax.numpy as jnp
from jax import lax
from jax.experimental import pallas as pl
from jax.experimental.pallas import tpu as pltpu

F32 = jnp.float32
BF16 = jnp.bfloat16

_VMEM_LIMIT_BYTES = 56 * 1024 * 1024
_LANES = 128
_SUBLANES = 8

LN_EPS = 1e-5
RMS_EPS = 1e-5
LRU_C = 8.0
CONV_K = 4
SSD_CHUNK = 128
SSD_HEAD_DIM = 64
SSD_STATE = 128
MOE_GROUPS = 4
MOE_PER_GROUP = 4
N_EXPERTS = MOE_GROUPS * MOE_PER_GROUP
ROUTER_COLS = 128
ROUTER_L2_ROW0 = 8
LOG2E = math.log2(math.e)


def _params(*sem):
    return pltpu.CompilerParams(dimension_semantics=sem, vmem_limit_bytes=_VMEM_LIMIT_BYTES)


def _sigmoid(x):
    return 1.0 / (1.0 + jnp.exp(-x))


def _softplus(x):
    e = jnp.exp(-jnp.abs(x))
    u = 1.0 + e
    return jnp.maximum(x, 0.0) + jnp.where(u == 1.0, e, jnp.log(u) * (e / (u - 1.0)))


def _shift_rows(x, tail, d):
    r = pltpu.roll(x, d, axis=0)
    t = pltpu.roll(tail, d, axis=0)
    row = lax.broadcasted_iota(jnp.int32, tail.shape, 0)
    first = jnp.where(row < d, t, r[:_SUBLANES])
    return jnp.concatenate([first, r[_SUBLANES:]], axis=0)


def _causal_conv(x, tail, w_ref, b_ref):
    acc = b_ref[...] + w_ref[CONV_K - 1:CONV_K, :] * x
    for d in range(1, CONV_K):
        acc = acc + w_ref[CONV_K - 1 - d:CONV_K - d, :] * _shift_rows(x, tail, d)
    return acc


def _cumsum_rows(x):
    n = x.shape[0]
    row = lax.broadcasted_iota(jnp.int32, x.shape, 0)
    d = 1
    while d < n:
        x = x + jnp.where(row >= d, pltpu.roll(x, d, axis=0), 0.0)
        d *= 2
    return x


def _cumsum_lanes(x):
    n = x.shape[1]
    lane = lax.broadcasted_iota(jnp.int32, x.shape, 1)
    d = 1
    while d < n:
        x = x + jnp.where(lane >= d, pltpu.roll(x, d, axis=1), 0.0)
        d *= 2
    return x


def _layer_norm(s, g, b):
    mu = jnp.mean(s, axis=-1, keepdims=True)
    d = s - mu
    var = jnp.mean(d * d, axis=-1, keepdims=True)
    return d * lax.rsqrt(var + LN_EPS) * g + b


def _proj_small_body(x_ref, wt_ref, o_ref):
    o_ref[...] = lax.dot_general(x_ref[...], wt_ref[...].astype(BF16), (((1,), (1,)), ((), ())),
                                 preferred_element_type=F32)


def _proj_small(x, w_t, tm, name):
    m, k = x.shape
    tm = min(tm, m)
    w_t = jnp.pad(w_t, ((0, _LANES - w_t.shape[0]), (0, 0)))
    return pl.pallas_call(
        _proj_small_body,
        grid=(m // tm,),
        in_specs=[pl.BlockSpec((tm, k), lambda i: (i, 0)),
                  pl.BlockSpec((_LANES, k), lambda i: (0, 0))],
        out_specs=pl.BlockSpec((tm, _LANES), lambda i: (i, 0)),
        out_shape=jax.ShapeDtypeStruct((m, _LANES), F32),
        compiler_params=_params("parallel"),
        name=name,
    )(x, w_t)


def _proj_w32_body(x_ref, w_ref, *rest, epilogue, tiles_per_seq, chunk, scale_cols, scale):
    conv = epilogue.startswith("conv")
    if conv:
        cw_ref, cb_ref, o_ref, wb_ref, carry_ref = rest
    else:
        o_ref, wb_ref = rest
    j = pl.program_id(0)
    t = pl.program_id(1)
    tm = x_ref.shape[0]

    @pl.when(t == 0)
    def _():
        wb_ref[...] = w_ref[...].T.astype(BF16)

    if conv:
        @pl.when(t % tiles_per_seq == 0)
        def _():
            carry_ref[...] = jnp.zeros_like(carry_ref)

    x = x_ref[...]
    for c0 in range(0, o_ref.shape[1], chunk):
        cols = slice(c0, c0 + chunk)
        acc = jnp.dot(x, wb_ref[:, cols], preferred_element_type=F32)
        if conv:
            tail = carry_ref[:, cols]
            carry_ref[:, cols] = acc[tm - _SUBLANES:, :]
            acc = _causal_conv(acc, tail, cw_ref.at[:, cols], cb_ref.at[:, cols])
        if epilogue.endswith("silu"):
            acc = acc * _sigmoid(acc)
        elif epilogue == "gelu":
            acc = jax.nn.gelu(acc)
        elif epilogue == "scale":
            acc = acc * jnp.where(j < scale_cols, scale, 1.0)
        o_ref[:, cols] = acc.astype(o_ref.dtype)


def _proj_w32(x, w_all, layer, col0, n, out_dtype, tm, tn, name, epilogue="none", conv=None, seq_len=None,
              scale_cols=0, scale=1.0, chunk=256):
    m, k = x.shape
    tm, tn = min(tm, m), min(tn, n)
    assert col0 % tn == 0 and n % tn == 0 and m % tm == 0 and tn % chunk == 0
    jb = col0 // tn
    in_specs = [pl.BlockSpec((tm, k), lambda j, t: (t, 0)),
                pl.BlockSpec((None, tn, k), lambda j, t: (layer, jb + j, 0))]
    args = [x, w_all]
    scratch = [pltpu.VMEM((k, tn), BF16)]
    tiles_per_seq = 1
    if conv is not None:
        assert seq_len % tm == 0
        tiles_per_seq = seq_len // tm
        in_specs += [pl.BlockSpec((CONV_K, tn), lambda j, t: (0, j)),
                     pl.BlockSpec((1, tn), lambda j, t: (0, j))]
        args += [conv[0], conv[1].reshape(1, n)]
        scratch.append(pltpu.VMEM((_SUBLANES, tn), F32))
    return pl.pallas_call(
        functools.partial(_proj_w32_body, epilogue=epilogue, tiles_per_seq=tiles_per_seq, chunk=chunk,
                          scale_cols=scale_cols, scale=scale),
        grid=(n // tn, m // tm),
        in_specs=in_specs,
        out_specs=pl.BlockSpec((tm, tn), lambda j, t: (t, j)),
        out_shape=jax.ShapeDtypeStruct((m, n), out_dtype),
        scratch_shapes=scratch,
        compiler_params=_params("arbitrary", "arbitrary"),
        name=name,
    )(*args)


def _mm_ln_body(*refs, alpha, n_a, nk):
    a_refs = refs[:n_a]
    w_ref, x_ref, g_ref, b_ref, of_ref, ob_ref = refs[n_a:n_a + 6]
    ka = a_refs[0].shape[1]
    part = jnp.dot(a_refs[0][...], w_ref[:ka, :], preferred_element_type=F32)
    for idx in range(1, n_a):
        part = part + jnp.dot(a_refs[idx][...], w_ref[idx * ka:(idx + 1) * ka, :], preferred_element_type=F32)

    def finish(acc):
        y = _layer_norm(alpha * x_ref[...] + acc, g_ref[...], b_ref[...])
        of_ref[...] = y
        ob_ref[...] = y.astype(BF16)

    if nk == 1:
        finish(part)
        return
    acc_ref = refs[n_a + 6]
    k = pl.program_id(1)

    @pl.when(k == 0)
    def _():
        acc_ref[...] = part

    @pl.when(k > 0)
    def _():
        acc_ref[...] += part

    @pl.when(k == nk - 1)
    def _():
        finish(acc_ref[...])


def _matmul_residual_ln(a_list, w_all, layer, x, g, b, alpha, tm, tk, name):
    n_a = len(a_list)
    m, ka = a_list[0].shape
    n = w_all.shape[2]
    tm = min(tm, m)
    tka = ka if n_a > 1 else min(tk, ka)
    nk = ka // tka
    w_mode = dict(pipeline_mode=pl.Buffered(1)) if nk == 1 else {}
    return pl.pallas_call(
        functools.partial(_mm_ln_body, alpha=alpha, n_a=n_a, nk=nk),
        grid=(m // tm, nk),
        in_specs=[pl.BlockSpec((tm, tka), lambda i, k: (i, k))] * n_a + [
                  pl.BlockSpec((None, n_a * tka, n), lambda i, k: (layer, k, 0), **w_mode),
                  pl.BlockSpec((tm, n), lambda i, k: (i, 0)),
                  pl.BlockSpec((1, n), lambda i, k: (0, 0)),
                  pl.BlockSpec((1, n), lambda i, k: (0, 0))],
        out_specs=[pl.BlockSpec((tm, n), lambda i, k: (i, 0)),
                   pl.BlockSpec((tm, n), lambda i, k: (i, 0))],
        out_shape=[jax.ShapeDtypeStruct((m, n), F32), jax.ShapeDtypeStruct((m, n), BF16)],
        scratch_shapes=[pltpu.VMEM((tm, n), F32)] if nk > 1 else [],
        compiler_params=_params("parallel", "arbitrary"),
        name=name,
    )(*a_list, w_all, x, g.reshape(1, n), b.reshape(1, n))


def _fox_prep_body(fl_ref, bf_ref, f_ref):
    x = fl_ref[0] + bf_ref[...]
    f_ref[0] = _cumsum_lanes(-_softplus(-x)) * LOG2E


def _fox_prep(fl_t, b_f):
    bn, h, s = fl_t.shape
    return pl.pallas_call(
        _fox_prep_body,
        grid=(bn,),
        in_specs=[pl.BlockSpec((1, h, s), lambda b: (b, 0, 0)),
                  pl.BlockSpec((h, 1), lambda b: (0, 0))],
        out_specs=pl.BlockSpec((1, h, s), lambda b: (b, 0, 0)),
        out_shape=jax.ShapeDtypeStruct((bn, h, s), F32),
        compiler_params=_params("parallel"),
        name="fox_prep",
    )(fl_t, b_f.reshape(h, 1))


def _lru_body(xc_ref, gg_ref, wr_ref, br_ref, wi_ref, bi_ref, lam_ref, y_ref, h_ref, acum_ref, hloc_ref,
              *, heads, block):
    @pl.when(pl.program_id(1) == 0)
    def _():
        h_ref[...] = jnp.zeros_like(h_ref)

    xcb = xc_ref[0]
    xc = xcb.astype(F32)
    ts = xc.shape[0]
    r_parts, i_parts = [], []
    for hd in range(heads):
        xh = xcb[:, hd * block:(hd + 1) * block]
        r_parts.append(jnp.dot(xh, wr_ref[hd], preferred_element_type=F32))
        i_parts.append(jnp.dot(xh, wi_ref[hd], preferred_element_type=F32))
    r = _sigmoid(jnp.concatenate(r_parts, axis=1) + br_ref[...])
    ig = _sigmoid(jnp.concatenate(i_parts, axis=1) + bi_ref[...])
    log_a = -LRU_C * r * _softplus(-lam_ref[...])
    a = jnp.exp(log_a)
    u = jnp.sqrt(-jnp.tanh(log_a) * (a * a + 1.0)) * (ig * xc)

    width = a.shape[1]
    a3 = a.reshape(ts // _SUBLANES, _SUBLANES, width)
    u3 = u.reshape(ts // _SUBLANES, _SUBLANES, width)
    row = lax.broadcasted_iota(jnp.int32, a3.shape, 1)
    d = 1
    while d < _SUBLANES:
        keep = row >= d
        u3 = u3 + a3 * jnp.where(keep, pltpu.roll(u3, d, axis=1), 0.0)
        a3 = a3 * jnp.where(keep, pltpu.roll(a3, d, axis=1), 1.0)
        d *= 2
    acum_ref[...] = a3
    hloc_ref[...] = u3

    def carry_group(g, h_prev):
        hg = hloc_ref[g] + acum_ref[g] * h_prev
        hloc_ref[g] = hg
        return hg[_SUBLANES - 1:_SUBLANES, :]

    h_ref[...] = lax.fori_loop(0, ts // _SUBLANES, carry_group, h_ref[...], unroll=4)
    h = hloc_ref[...].reshape(ts, width)
    y_ref[0] = (h * gg_ref[0].astype(F32)).astype(y_ref.dtype)


def _lru(xc, gg, w_r, b_r, w_i, b_i, lam, ts):
    bn, s, width = xc.shape
    heads, block, _ = w_r.shape
    ts = min(ts, s)
    vec = lambda v: v.reshape(1, width)
    const2 = lambda b, i: (0, 0)
    const3 = lambda b, i: (0, 0, 0)
    tile = pl.BlockSpec((1, ts, width), lambda b, i: (b, i, 0))
    return pl.pallas_call(
        functools.partial(_lru_body, heads=heads, block=block),
        grid=(bn, s // ts),
        in_specs=[tile, tile,
                  pl.BlockSpec((heads, block, block), const3),
                  pl.BlockSpec((1, width), const2),
                  pl.BlockSpec((heads, block, block), const3),
                  pl.BlockSpec((1, width), const2),
                  pl.BlockSpec((1, width), const2)],
        out_specs=tile,
        out_shape=jax.ShapeDtypeStruct((bn, s, width), BF16),
        scratch_shapes=[pltpu.VMEM((1, width), F32),
                        pltpu.VMEM((ts // _SUBLANES, _SUBLANES, width), F32),
                        pltpu.VMEM((ts // _SUBLANES, _SUBLANES, width), F32)],
        compiler_params=_params("parallel", "arbitrary"),
        name="rg_lru",
    )(xc, gg, w_r.astype(BF16), vec(b_r), w_i.astype(BF16), vec(b_i), vec(lam))


def _split3(f):
    hi = f.astype(BF16).astype(F32)
    r = f - hi
    mid = r.astype(BF16).astype(F32)
    return hi, mid, r - mid


def _fox_attn_body(q_ref, k_ref, v_ref, fc_ref, o_ref, kx_ref, *, tq, hp, dh):
    qi = pl.program_id(2)
    s_len = k_ref.shape[1]
    heads = range(hp)
    lane = lax.broadcasted_iota(jnp.int32, (tq, _LANES), 1)

    @pl.when(qi == 0)
    def _():
        for e in heads:
            for c0 in range(0, s_len, tq):
                hi, mid, lo = _split3(fc_ref[0, e, c0:c0 + tq, :])
                kx = jnp.where(lane < 3, 1.0, jnp.where(lane == 3, -hi, jnp.where(lane == 4, -mid,
                               jnp.where(lane == 5, -lo, 0.0))))
                kx_ref[e, c0:c0 + tq, :] = kx.astype(BF16)

    qs = []
    for e in heads:
        hi, mid, lo = _split3(fc_ref[0, e, pl.ds(pl.multiple_of(qi * tq, tq), tq), :])
        qx = jnp.where(lane == 0, hi, jnp.where(lane == 1, mid, jnp.where(lane == 2, lo,
                       jnp.where(lane < 6, 1.0, 0.0))))
        qs.append(jnp.concatenate([q_ref[0, :, e * dh:(e + 1) * dh], qx.astype(BF16)], axis=1))

    def scores(e, j):
        start = pl.multiple_of(j * tq, tq)
        k = jnp.concatenate([k_ref[0, pl.ds(start, tq), e * dh:(e + 1) * dh], kx_ref[e, pl.ds(start, tq), :]], axis=1)
        return lax.dot_general(qs[e], k, (((1,), (1,)), ((), ())), preferred_element_type=F32)

    def update(e, carry, sc, j):
        m, l, acc = carry
        m_new = jnp.maximum(m, jnp.max(sc, axis=1, keepdims=True))
        corr = jnp.exp2(m - m_new)
        p = jnp.exp2(sc - m_new)
        l = corr * l + jnp.sum(p, axis=1, keepdims=True)
        v = v_ref[0, pl.ds(pl.multiple_of(j * tq, tq), tq), e * dh:(e + 1) * dh]
        acc = corr * acc + jnp.dot(p.astype(BF16), v, preferred_element_type=F32)
        return m_new, l, acc

    def full_blocks(carries, js):
        scs = [[scores(e, j) for j in js] for e in heads]
        out = []
        for e in heads:
            c = carries[e]
            for sc, j in zip(scs[e], js):
                c = update(e, c, sc, j)
            out.append(c)
        return tuple(out)

    def diagonal(carries):
        scs = [scores(e, qi) for e in heads]
        rq = lax.broadcasted_iota(jnp.int32, (tq, tq), 0)
        ck = lax.broadcasted_iota(jnp.int32, (tq, tq), 1)
        return tuple(update(e, carries[e], jnp.where(ck <= rq, scs[e], -1e30), qi) for e in heads)

    def two_blocks(jj, carries):
        return full_blocks(carries, (2 * jj, 2 * jj + 1))

    def odd_tail(carries):
        return diagonal(full_blocks(carries, (qi - 1,)))

    init = tuple((jnp.full((tq, 1), -1e30, F32), jnp.zeros((tq, 1), F32), jnp.zeros((tq, dh), F32))
                 for _ in heads)
    carries = lax.fori_loop(0, qi // 2, two_blocks, init)
    carries = lax.cond(qi % 2 == 1, odd_tail, diagonal, carries)
    o_ref[0] = jnp.concatenate([acc / l for _, l, acc in carries], axis=1).astype(o_ref.dtype)


def _fox_attention(qkv, f2, heads, dh, tq, hp):
    bn, s, _ = qkv.shape
    tq = min(tq, s)
    assert heads % hp == 0
    hb = heads // hp
    return pl.pallas_call(
        functools.partial(_fox_attn_body, tq=tq, hp=hp, dh=dh),
        grid=(bn, hb, s // tq),
        in_specs=[pl.BlockSpec((1, tq, hp * dh), lambda b, h, i: (b, i, h)),
                  pl.BlockSpec((1, s, hp * dh), lambda b, h, i: (b, 0, hb + h)),
                  pl.BlockSpec((1, s, hp * dh), lambda b, h, i: (b, 0, 2 * hb + h)),
                  pl.BlockSpec((1, hp, s, 1), lambda b, h, i: (b, h, 0, 0))],
        out_specs=pl.BlockSpec((1, tq, hp * dh), lambda b, h, i: (b, i, h)),
        out_shape=jax.ShapeDtypeStruct((bn, s, heads * dh), BF16),
        scratch_shapes=[pltpu.VMEM((hp, s, _LANES), BF16)],
        compiler_params=_params("parallel", "parallel", "arbitrary"),
        name="fox_attention",
    )(qkv, qkv, qkv, f2[..., None])


def _ssd_body(xbc_ref, z_ref, dt_ref, dtb_ref, alog_ref, dexp_ref, ng_ref, o_ref,
              state_ref, acst_ref, dtt_ref, wt_ref, *, groups, inner):
    L = SSD_CHUNK
    gw = inner // groups
    pairs = gw // _LANES
    hpg = gw // SSD_HEAD_DIM

    @pl.when(pl.program_id(1) == 0)
    def _():
        state_ref[...] = jnp.zeros_like(state_ref)

    dt = _softplus(dt_ref[0] + dtb_ref[...])
    acs = _cumsum_rows(dt * (-jnp.exp(alog_ref[...]))) * LOG2E
    acs_t = acs.T
    dt_t = dt.T
    acst_ref[...] = acs_t
    dtt_ref[...] = dt_t
    wt_ref[...] = dt_t * jnp.exp2(acs_t[:, L - 1:L] - acs_t)
    dec_last = jnp.exp2(acs[L - 1:L, :])

    rq = lax.broadcasted_iota(jnp.int32, (L, L), 0)
    ck = lax.broadcasted_iota(jnp.int32, (L, L), 1)
    causal = ck <= rq
    lo = lax.broadcasted_iota(jnp.int32, (L, _LANES), 1) < SSD_HEAD_DIM
    lo_row = lo[:1]

    for g in range(groups):
        bmat = xbc_ref[0, :, inner + g * SSD_STATE:inner + (g + 1) * SSD_STATE]
        cmat = xbc_ref[0, :, inner + (groups + g) * SSD_STATE:inner + (groups + g + 1) * SSD_STATE]
        cbm = lax.dot_general(cmat, bmat, (((1,), (1,)), ((), ())), preferred_element_type=F32)
        cmat = cmat.astype(F32)
        b_t = bmat.astype(F32).T
        ys = []
        for j in range(pairs):
            col0 = g * gw + j * _LANES
            xs_p = xbc_ref[0, :, col0:col0 + _LANES].astype(F32)
            m_parts, c_parts, bw_parts, dl = [], [], [], []
            for e in range(2):
                hd = g * hpg + 2 * j + e
                col = jnp.broadcast_to(acs[:, hd:hd + 1], (L, L))
                row = acst_ref[hd:hd + 1, :]
                dec = jnp.exp2(jnp.where(causal, col - row, -jnp.inf))
                m_parts.append((cbm * dec * dtt_ref[hd:hd + 1, :]).astype(BF16))
                c_parts.append((cmat * jnp.exp2(col)).astype(BF16))
                bw_parts.append((b_t * wt_ref[hd:hd + 1, :]).astype(BF16))
                dl.append(dec_last[:, hd:hd + 1])
            prev = state_ref[:, col0:col0 + _LANES]
            x_top = jnp.where(lo, xs_p, 0.0).astype(BF16)
            x_bot = jnp.where(lo, 0.0, xs_p).astype(BF16)
            p_top = jnp.where(lo, prev, 0.0).astype(BF16)
            p_bot = jnp.where(lo, 0.0, prev).astype(BF16)
            lhs = jnp.concatenate(m_parts + c_parts, axis=1)
            rhs = jnp.concatenate([x_top, x_bot, p_top, p_bot], axis=0)
            y = jnp.dot(lhs, rhs, preferred_element_type=F32)
            st = jnp.dot(jnp.concatenate(bw_parts, axis=1), jnp.concatenate([x_top, x_bot], axis=0),
                         preferred_element_type=F32)
            state_ref[:, col0:col0 + _LANES] = prev * jnp.where(lo_row, dl[0], dl[1]) + st
            y = y + dexp_ref[:, col0:col0 + _LANES] * xs_p
            ys.append(y * z_ref[0, :, col0:col0 + _LANES].astype(F32))
        yg = jnp.concatenate(ys, axis=1)
        ms = jnp.mean(yg * yg, axis=1, keepdims=True)
        gcols = slice(g * gw, (g + 1) * gw)
        o_ref[0, :, gcols] = (yg * lax.rsqrt(ms + RMS_EPS) * ng_ref[:, gcols]).astype(o_ref.dtype)


def _ssd_core(xbc, zs, dt, dt_bias, a_log, d_skip, norm_g, groups):
    bn, s, conv_dim = xbc.shape
    inner = zs.shape[2]
    heads = a_log.shape[0]
    L = SSD_CHUNK
    pad = lambda v: jnp.pad(v, (0, _LANES - heads)).reshape(1, _LANES)
    const2 = lambda b, c: (0, 0)
    return pl.pallas_call(
        functools.partial(_ssd_body, groups=groups, inner=inner),
        grid=(bn, s // L),
        in_specs=[pl.BlockSpec((1, L, conv_dim), lambda b, c: (b, c, 0)),
                  pl.BlockSpec((1, L, inner), lambda b, c: (b, c, 0)),
                  pl.BlockSpec((1, L, _LANES), lambda b, c: (b, c, 0)),
                  pl.BlockSpec((1, _LANES), const2),
                  pl.BlockSpec((1, _LANES), const2),
                  pl.BlockSpec((1, inner), const2),
                  pl.BlockSpec((1, inner), const2)],
        out_specs=pl.BlockSpec((1, L, inner), lambda b, c: (b, c, 0)),
        out_shape=jax.ShapeDtypeStruct((bn, s, inner), BF16),
        scratch_shapes=[pltpu.VMEM((SSD_STATE, inner), F32),
                        pltpu.VMEM((_LANES, L), F32),
                        pltpu.VMEM((_LANES, L), F32),
                        pltpu.VMEM((_LANES, L), F32)],
        compiler_params=_params("parallel", "arbitrary"),
        name="ssd_core",
    )(xbc, zs, dt, pad(dt_bias), pad(a_log),
      jnp.repeat(d_skip, SSD_HEAD_DIM).reshape(1, inner), norm_g.reshape(1, inner))


def _router_body(x_ref, whl_ref, wh_ref, b_ref, oi_ref, of_ref, cnt_ref, base_ref):
    i = pl.program_id(0)

    @pl.when(i == 0)
    def _():
        base_ref[...] = jnp.zeros_like(base_ref)

    x = x_ref[...]
    tm = x.shape[0]
    xh = x.astype(BF16)
    xl = (x - xh.astype(F32)).astype(BF16)
    r = jnp.dot(xh, whl_ref[...], preferred_element_type=F32)
    logits = (r[:, :ROUTER_COLS] + r[:, ROUTER_COLS:]
              + jnp.dot(xl, wh_ref[...], preferred_element_type=F32) + b_ref[...])
    lt = logits.T

    row = lax.broadcasted_iota(jnp.int32, (_SUBLANES, tm), 0)
    neg = -jnp.inf
    l1 = jnp.where(row < MOE_GROUPS, lt[:_SUBLANES], neg)
    e1 = jnp.exp(l1 - jnp.max(l1, axis=0, keepdims=True))
    p1 = e1 / jnp.sum(e1, axis=0, keepdims=True)
    pg = jnp.max(p1, axis=0, keepdims=True)
    gidx = jnp.min(jnp.where(p1 == pg, row, _SUBLANES), axis=0, keepdims=True)

    sel = jnp.full((_SUBLANES, tm), neg, F32)
    for gi in range(MOE_GROUPS):
        r0 = ROUTER_L2_ROW0 + _SUBLANES * gi
        sel = jnp.where((gidx == gi) & (row < MOE_PER_GROUP), lt[r0:r0 + _SUBLANES], sel)
    va = jnp.max(sel, axis=0, keepdims=True)
    ia = jnp.min(jnp.where(sel == va, row, _SUBLANES), axis=0, keepdims=True)
    sel_b = jnp.where(row == ia, neg, sel)
    vb = jnp.max(sel_b, axis=0, keepdims=True)
    ib = jnp.min(jnp.where(sel_b == vb, row, _SUBLANES), axis=0, keepdims=True)
    t = jnp.exp(vb - va)
    ca = pg * (1.0 / (1.0 + t))
    cb = pg * (t / (1.0 + t))
    ea = gidx * MOE_PER_GROUP + ia
    eb = gidx * MOE_PER_GROUP + ib

    erow = lax.broadcasted_iota(jnp.int32, (N_EXPERTS, tm), 0)
    hit_a = erow == ea
    hit_b = erow == eb
    oh = jnp.where(hit_a, 1.0, 0.0) + jnp.where(hit_b, 1.0, 0.0)
    before = (lax.broadcasted_iota(jnp.int32, (tm, tm), 0)
              < lax.broadcasted_iota(jnp.int32, (tm, tm), 1))
    prefix = jnp.dot(oh.astype(BF16), jnp.where(before, 1.0, 0.0).astype(BF16),
                     preferred_element_type=F32) + base_ref[...]
    rank_a = jnp.sum(jnp.where(hit_a, prefix, 0.0), axis=0, keepdims=True)
    rank_b = jnp.sum(jnp.where(hit_b, prefix, 0.0), axis=0, keepdims=True)
    base_ref[...] += jnp.sum(oh, axis=1, keepdims=True)
    cnt_ref[...] = jnp.broadcast_to(base_ref[...], cnt_ref.shape)

    zi = jnp.zeros((1, tm), jnp.int32)
    oi_ref[...] = jnp.concatenate(
        [ea, eb, rank_a.astype(jnp.int32), rank_b.astype(jnp.int32), zi, zi, zi, zi], axis=0)
    zf = jnp.zeros((1, tm), F32)
    of_ref[...] = jnp.concatenate([ca, cb, zf, zf, zf, zf, zf, zf], axis=0)


def _router(x, w_r1, b_r1, w_r2, b_r2, tm):
    t, d = x.shape
    tm = min(tm, t)
    w = jnp.zeros((d, ROUTER_COLS), F32).at[:, :MOE_GROUPS].set(w_r1)
    b = jnp.zeros((ROUTER_COLS,), F32).at[:MOE_GROUPS].set(b_r1)
    for gi in range(MOE_GROUPS):
        c0 = ROUTER_L2_ROW0 + _SUBLANES * gi
        w = w.at[:, c0:c0 + MOE_PER_GROUP].set(w_r2[gi])
        b = b.at[c0:c0 + MOE_PER_GROUP].set(b_r2[gi])
    wh = w.astype(BF16)
    wl = (w - wh.astype(F32)).astype(BF16)
    const = lambda i: (0, 0)
    return pl.pallas_call(
        _router_body,
        grid=(t // tm,),
        in_specs=[pl.BlockSpec((tm, d), lambda i: (i, 0)),
                  pl.BlockSpec((d, 2 * ROUTER_COLS), const),
                  pl.BlockSpec((d, ROUTER_COLS), const),
                  pl.BlockSpec((1, ROUTER_COLS), const)],
        out_specs=[pl.BlockSpec((_SUBLANES, tm), lambda i: (0, i)),
                   pl.BlockSpec((_SUBLANES, tm), lambda i: (0, i)),
                   pl.BlockSpec((N_EXPERTS, _LANES), const)],
        out_shape=[jax.ShapeDtypeStruct((_SUBLANES, t), jnp.int32),
                   jax.ShapeDtypeStruct((_SUBLANES, t), F32),
                   jax.ShapeDtypeStruct((N_EXPERTS, _LANES), F32)],
        scratch_shapes=[pltpu.VMEM((N_EXPERTS, 1), F32)],
        compiler_params=_params("arbitrary"),
        name="moe_router",
    )(x, jnp.concatenate([wh, wl], axis=1), wh, b.reshape(1, ROUTER_COLS))


def _row_copy(src, src_row, dst, dst_row, sem):
    return pltpu.make_async_copy(src.at[pl.ds(src_row, 1)], dst.at[pl.ds(dst_row, 1)], sem)


def _dispatch_body(pos_ref, zt_ref, x_ref, xs_ref, zeros_ref, sem, zsem, *, td, tz, t_total):
    step = pl.program_id(0)
    base = step * td

    def zero_tile(i):
        return pltpu.make_async_copy(zeros_ref, xs_ref.at[pl.ds(pl.multiple_of(i * tz, tz), tz)], zsem)

    @pl.when(step == 0)
    def _():
        zeros_ref[...] = jnp.zeros_like(zeros_ref)

        def start(i, carry):
            @pl.when(zt_ref[i] == 1)
            def _():
                zero_tile(i).start()
            return carry

        def wait(i, carry):
            @pl.when(zt_ref[i] == 1)
            def _():
                zero_tile(i).wait()
            return carry

        lax.fori_loop(0, zt_ref.shape[0], start, 0)
        lax.fori_loop(0, zt_ref.shape[0], wait, 0)

    def issue(r, carry):
        _row_copy(x_ref, r, xs_ref, pos_ref[base + r], sem).start()
        _row_copy(x_ref, r, xs_ref, pos_ref[t_total + base + r], sem).start(priority=1)
        return carry

    lax.fori_loop(0, td, issue, 0, unroll=4)
    tile_rows = xs_ref.at[pl.ds(0, td)]
    pltpu.make_async_copy(x_ref, tile_rows, sem).wait()
    pltpu.make_async_copy(x_ref, tile_rows, sem).wait()


def _dispatch(x, pos, zero_tiles, n_rows, td, tz):
    t, d = x.shape
    td = min(td, t)
    return pl.pallas_call(
        functools.partial(_dispatch_body, td=td, tz=tz, t_total=t),
        grid_spec=pltpu.PrefetchScalarGridSpec(
            num_scalar_prefetch=2,
            grid=(t // td,),
            in_specs=[pl.BlockSpec((td, d), lambda i, pos, zt: (i, 0))],
            out_specs=pl.BlockSpec(memory_space=pl.ANY),
            scratch_shapes=[pltpu.VMEM((tz, d), x.dtype),
                            pltpu.SemaphoreType.DMA(()), pltpu.SemaphoreType.DMA(())]),
        out_shape=jax.ShapeDtypeStruct((n_rows, d), x.dtype),
        compiler_params=_params("arbitrary"),
        name="moe_dispatch",
    )(pos, zero_tiles, x)


def _expert_body(blk_ref, te_ref, nv_ref, xs_ref, wg_ref, wu_ref, wd_ref, o_ref):
    del blk_ref, te_ref
    valid = pl.program_id(0) < nv_ref[0]

    @pl.when(jnp.logical_not(valid))
    def _():
        o_ref[...] = jnp.zeros_like(o_ref)

    @pl.when(valid)
    def _():
        x = xs_ref[...].astype(BF16)
        g = jnp.dot(x, wg_ref[0], preferred_element_type=F32)
        u = jnp.dot(x, wu_ref[0], preferred_element_type=F32)
        hid = (g * _sigmoid(g)) * u
        o_ref[...] = jnp.dot(hid.astype(BF16), wd_ref[0], preferred_element_type=F32)


def _experts(xs, w_gate, w_up, w_down, expert0, blk, tile_expert, n_valid, tm):
    p, d = xs.shape
    ff = w_gate.shape[2]
    n_tiles = p // tm
    rows = lambda i, blk, te, nv: (blk[i], 0)
    wsel = lambda i, blk, te, nv: (expert0 + te[i], 0, 0)
    return pl.pallas_call(
        _expert_body,
        grid_spec=pltpu.PrefetchScalarGridSpec(
            num_scalar_prefetch=3,
            grid=(n_tiles,),
            in_specs=[pl.BlockSpec((tm, d), rows),
                      pl.BlockSpec((1, d, ff), wsel),
                      pl.BlockSpec((1, d, ff), wsel),
                      pl.BlockSpec((1, ff, d), wsel)],
            out_specs=pl.BlockSpec((tm, d), lambda i, blk, te, nv: (i, 0))),
        out_shape=jax.ShapeDtypeStruct((p, d), F32),
        compiler_params=_params("arbitrary"),
        name="moe_experts",
    )(blk, tile_expert, n_valid, xs, w_gate, w_up, w_down)


def _combine_body(pos_ref, ys_ref, x_ref, c_ref, g_ref, b_ref, of_ref, ob_ref, buf_a, buf_b, sem,
                  *, tc, t_total, alpha):
    step = pl.program_id(0)
    slot = step % 2

    def issue(tile, which):
        base = tile * tc

        def body(r, carry):
            _row_copy(ys_ref, pos_ref[base + r], buf_a.at[which], r, sem.at[which]).start()
            _row_copy(ys_ref, pos_ref[t_total + base + r], buf_b.at[which], r, sem.at[which]).start(priority=1)
            return carry

        lax.fori_loop(0, tc, body, 0, unroll=4)

    @pl.when(step == 0)
    def _():
        issue(0, 0)

    @pl.when(step + 1 < pl.num_programs(0))
    def _():
        issue(step + 1, 1 - slot)

    tile_rows = ys_ref.at[pl.ds(0, tc)]
    pltpu.make_async_copy(tile_rows, buf_a.at[slot], sem.at[slot]).wait()
    pltpu.make_async_copy(tile_rows, buf_b.at[slot], sem.at[slot]).wait()
    c = c_ref[...]
    f = c[:, 0:1] * buf_a[slot] + c[:, 1:2] * buf_b[slot]
    y = _layer_norm(alpha * x_ref[...] + f, g_ref[...], b_ref[...])
    of_ref[...] = y
    ob_ref[...] = y.astype(BF16)


def _combine_ln(ys, pos, x, c, g, b, alpha, tc):
    t, d = x.shape
    tc = min(tc, t)
    const = lambda i, pos: (0, 0)
    tile = lambda i, pos: (i, 0)
    return pl.pallas_call(
        functools.partial(_combine_body, tc=tc, t_total=t, alpha=alpha),
        grid_spec=pltpu.PrefetchScalarGridSpec(
            num_scalar_prefetch=1,
            grid=(t // tc,),
            in_specs=[pl.BlockSpec(memory_space=pl.ANY),
                      pl.BlockSpec((tc, d), tile),
                      pl.BlockSpec((tc, 2), tile),
                      pl.BlockSpec((1, d), const),
                      pl.BlockSpec((1, d), const)],
            out_specs=[pl.BlockSpec((tc, d), tile), pl.BlockSpec((tc, d), tile)],
            scratch_shapes=[pltpu.VMEM((2, tc, d), F32), pltpu.VMEM((2, tc, d), F32),
                            pltpu.SemaphoreType.DMA((2,))]),
        out_shape=[jax.ShapeDtypeStruct((t, d), F32), jax.ShapeDtypeStruct((t, d), BF16)],
        compiler_params=_params("arbitrary"),
        name="moe_combine_ln",
    )(pos, ys, x, c, g.reshape(1, d), b.reshape(1, d))


def _moe_ln(x, w_r1, b_r1, w_r2, b_r2, w_gate, w_up, w_down, layer, g, b, alpha, tiles):
    t, d = x.shape
    tm_e = min(tiles["expert_rows"], t)
    oi, of, cnt = _router(x, w_r1, b_r1, w_r2, b_r2, tiles["router"])
    counts = cnt[:, 0].astype(jnp.int32)
    seg_tiles = (counts + tm_e - 1) // tm_e
    tile_end = jnp.cumsum(seg_tiles)
    seg_start = (tile_end - seg_tiles) * tm_e
    start_of = lambda e: jnp.sum(jnp.where(e[:, None] == jnp.arange(N_EXPERTS, dtype=jnp.int32)[None, :],
                                           seg_start[None, :], 0), axis=1)
    pos = jnp.concatenate([start_of(oi[0]) + oi[2], start_of(oi[1]) + oi[3]])
    n_tiles = 2 * t // tm_e + N_EXPERTS
    n_valid = tile_end[-1:]
    tile_id = jnp.arange(n_tiles, dtype=jnp.int32)
    blk = jnp.minimum(tile_id, n_valid - 1)
    tile_expert = jnp.minimum(jnp.sum((blk[:, None] >= tile_end[None, :]).astype(jnp.int32), axis=1), N_EXPERTS - 1)
    zero_tiles = (jnp.any(tile_id[:, None] == (tile_end - 1)[None, :], axis=1) | (tile_id >= n_valid)).astype(jnp.int32)

    xs = _dispatch(x, pos, zero_tiles, n_tiles * tm_e, tiles["dispatch"], tm_e)
    ys = _experts(xs, w_gate, w_up, w_down, layer * N_EXPERTS, blk, tile_expert, n_valid, tm_e)
    return _combine_ln(ys, pos, x, of[:2].T, g, b, alpha, tiles["combine"])


def _hybrid_mixer_ln(xf, xb, bn, j, w_in_all, conv_w, conv_b, w_r, b_r, w_i, b_i, lam, b_f, w_out_all, g, b, alpha, tiles):
    t, d = xf.shape
    s = t // bn
    heads, block, _ = w_r.shape
    lru_w = heads * block
    fox_heads = b_f.shape[0]
    main = w_in_all.shape[1] - fox_heads
    fox_w = (main - 2 * lru_w) // 3
    dh = fox_w // fox_heads
    tm, tn = tiles["mm_m"], tiles["mm_n"]
    xc = _proj_w32(xb, w_in_all, j, 0, lru_w, BF16, tm, tn, "hy_lru_x_proj_conv", "conv", (conv_w, conv_b), s)
    gg = _proj_w32(xb, w_in_all, j, lru_w, lru_w, BF16, tm, tn, "hy_lru_gate_proj_gelu", "gelu")
    qkv = _proj_w32(xb, w_in_all, j, 2 * lru_w, 3 * fox_w, BF16, tm, tn, "hy_qkv_proj", "scale",
                     scale_cols=fox_w // min(tn, fox_w), scale=dh ** -0.5 * LOG2E)
    fl = _proj_small(xb, w_in_all[j, main:, :], tiles["small_m"], "hy_forget_proj")[:, :fox_heads]
    f2 = _fox_prep(fl.reshape(bn, s, fox_heads).transpose(0, 2, 1), b_f)
    y_lru = _lru(xc.reshape(bn, s, lru_w), gg.reshape(bn, s, lru_w), w_r, b_r, w_i, b_i, lam, tiles["lru"])
    y_att = _fox_attention(qkv.reshape(bn, s, 3 * fox_w), f2, fox_heads, dh, tiles["attn"], tiles["attn_heads"])
    return _matmul_residual_ln([y_lru.reshape(t, lru_w), y_att.reshape(t, fox_w)], w_out_all, j, xf, g, b, alpha,
                               tiles["ln_m"], tiles["ln_k"], "hy_out_proj_ln")


def _ssd_mixer_ln(xf, xb, bn, j, w_in_all, conv_w, conv_b, dt_bias, a_log, d_skip, norm_g, w_out_all, g, b, alpha, tiles):
    t, d = xf.shape
    s = t // bn
    heads = a_log.shape[0]
    inner = norm_g.shape[0]
    conv_dim = conv_w.shape[1]
    groups = (conv_dim - inner) // (2 * SSD_STATE)
    tm, tn = tiles["mm_m"], tiles["mm_n"]
    zs = _proj_w32(xb, w_in_all, j, 0, inner, BF16, tm, tn, "ssd_z_proj_silu", "silu")
    xbc = _proj_w32(xb, w_in_all, j, inner, conv_dim, BF16, tm, tn, "ssd_xbc_proj_conv_silu", "conv_silu",
                     (conv_w, conv_b), s)
    dt = _proj_small(xb, w_in_all[j, inner + conv_dim:, :], tiles["small_m"], "ssd_dt_proj")
    y = _ssd_core(xbc.reshape(bn, s, conv_dim), zs.reshape(bn, s, inner), dt.reshape(bn, s, _LANES),
                  dt_bias, a_log, d_skip, norm_g, groups)
    return _matmul_residual_ln([y.reshape(t, inner)], w_out_all, j, xf, g, b, alpha,
                               tiles["ln_m"], tiles["ln_k"], "ssd_out_proj_ln")


_TILES = dict(mm_m=1024, mm_n=1024, small_m=1024, ln_m=512, ln_k=4096, lru=256, attn=512, attn_heads=1,
              router=512, dispatch=1024, expert_rows=512, combine=512)


def kernel(x, ln_g, ln_b, hy_w_in, hy_conv_w, hy_conv_b, hy_w_r, hy_b_r, hy_w_i, hy_b_i, hy_lambda, hy_b_f, hy_w_out, ssd_w_in, ssd_conv_w, ssd_conv_b, ssd_dt_bias, ssd_a_log, ssd_d, ssd_norm_g, ssd_w_out, moe_w_r1, moe_b_r1, moe_w_r2, moe_b_r2, moe_w_gate, moe_w_up, moe_w_down):
    bn, s, d = x.shape
    depth = ln_g.shape[0]
    alpha = (2 * depth) ** 0.25
    ff = moe_w_gate.shape[-1]
    hy_w_in_t = jnp.swapaxes(hy_w_in, 1, 2)
    ssd_w_in_t = jnp.swapaxes(ssd_w_in, 1, 2)
    hy_w_out_b = hy_w_out.astype(BF16)
    ssd_w_out_b = ssd_w_out.astype(BF16)
    w_gate_s = moe_w_gate.reshape(depth * N_EXPERTS, d, ff).astype(BF16)
    w_up_s = moe_w_up.reshape(depth * N_EXPERTS, d, ff).astype(BF16)
    w_down_s = moe_w_down.reshape(depth * N_EXPERTS, ff, d).astype(BF16)
    xf = x.reshape(bn * s, d)
    xb = xf.astype(BF16)
    for layer in range(depth):
        j = layer // 2
        if layer % 2 == 0:
            xf, xb = _hybrid_mixer_ln(xf, xb, bn, j, hy_w_in_t, hy_conv_w[j], hy_conv_b[j], hy_w_r[j], hy_b_r[j],
                                      hy_w_i[j], hy_b_i[j], hy_lambda[j], hy_b_f[j], hy_w_out_b,
                                      ln_g[layer, 0], ln_b[layer, 0], alpha, _TILES)
        else:
            xf, xb = _ssd_mixer_ln(xf, xb, bn, j, ssd_w_in_t, ssd_conv_w[j], ssd_conv_b[j], ssd_dt_bias[j],
                                   ssd_a_log[j], ssd_d[j], ssd_norm_g[j], ssd_w_out_b,
                                   ln_g[layer, 0], ln_b[layer, 0], alpha, _TILES)
        xf, xb = _moe_ln(xf, moe_w_r1[layer], moe_b_r1[layer], moe_w_r2[layer], moe_b_r2[layer],
                         w_gate_s, w_up_s, w_down_s, layer, ln_g[layer, 1], ln_b[layer, 1], alpha, _TILES)
    return xf.reshape(bn, s, d)
```

```python
import functools
import math

import jax
import jax.numpy as jnp
from jax import lax
from jax.experimental import pallas as pl
from jax.experimental.pallas import tpu as pltpu

F32 = jnp.float32
BF16 = jnp.bfloat16

_VMEM_LIMIT_BYTES = 56 * 1024 * 1024
_LANES = 128
_SUBLANES = 8

LN_EPS = 1e-5
RMS_EPS = 1e-5
LRU_C = 8.0
CONV_K = 4
SSD_CHUNK = 128
SSD_HEAD_DIM = 64
SSD_STATE = 128
MOE_GROUPS = 4
MOE_PER_GROUP = 4
N_EXPERTS = MOE_GROUPS * MOE_PER_GROUP
ROUTER_COLS = 128
ROUTER_L2_ROW0 = 8
LOG2E = math.log2(math.e)


def _params(*sem):
    return pltpu.CompilerParams(dimension_semantics=sem, vmem_limit_bytes=_VMEM_LIMIT_BYTES)


def _sigmoid(x):
    return 1.0 / (1.0 + jnp.exp(-x))


def _softplus(x):
    e = jnp.exp(-jnp.abs(x))
    u = 1.0 + e
    return jnp.maximum(x, 0.0) + jnp.where(u == 1.0, e, jnp.log(u) * (e / (u - 1.0)))


def _shift_rows(x, tail, d):
    r = pltpu.roll(x, d, axis=0)
    t = pltpu.roll(tail, d, axis=0)
    row = lax.broadcasted_iota(jnp.int32, tail.shape, 0)
    first = jnp.where(row < d, t, r[:_SUBLANES])
    return jnp.concatenate([first, r[_SUBLANES:]], axis=0)


def _causal_conv(x, tail, w_ref, b_ref):
    acc = b_ref[...] + w_ref[CONV_K - 1:CONV_K, :] * x
    for d in range(1, CONV_K):
        acc = acc + w_ref[CONV_K - 1 - d:CONV_K - d, :] * _shift_rows(x, tail, d)
    return acc


def _cumsum_rows(x):
    n = x.shape[0]
    row = lax.broadcasted_iota(jnp.int32, x.shape, 0)
    d = 1
    while d < n:
        x = x + jnp.where(row >= d, pltpu.roll(x, d, axis=0), 0.0)
        d *= 2
    return x


def _cumsum_lanes(x):
    n = x.shape[1]
    lane = lax.broadcasted_iota(jnp.int32, x.shape, 1)
    d = 1
    while d < n:
        x = x + jnp.where(lane >= d, pltpu.roll(x, d, axis=1), 0.0)
        d *= 2
    return x


def _layer_norm(s, g, b):
    mu = jnp.mean(s, axis=-1, keepdims=True)
    d = s - mu
    var = jnp.mean(d * d, axis=-1, keepdims=True)
    return d * lax.rsqrt(var + LN_EPS) * g + b


def _proj_small_body(x_ref, wt_ref, o_ref):
    o_ref[...] = lax.dot_general(x_ref[...], wt_ref[...].astype(BF16), (((1,), (1,)), ((), ())),
                                 preferred_element_type=F32)


def _proj_small(x, w_t, tm, name):
    m, k = x.shape
    tm = min(tm, m)
    w_t = jnp.pad(w_t, ((0, _LANES - w_t.shape[0]), (0, 0)))
    return pl.pallas_call(
        _proj_small_body,
        grid=(m // tm,),
        in_specs=[pl.BlockSpec((tm, k), lambda i: (i, 0)),
                  pl.BlockSpec((_LANES, k), lambda i: (0, 0))],
        out_specs=pl.BlockSpec((tm, _LANES), lambda i: (i, 0)),
        out_shape=jax.ShapeDtypeStruct((m, _LANES), F32),
        compiler_params=_params("parallel"),
        name=name,
    )(x, w_t)


def _proj_w32_body(x_ref, w_ref, *rest, epilogue, tiles_per_seq, chunk, scale_cols, scale):
    conv = epilogue.startswith("conv")
    if conv:
        cw_ref, cb_ref, o_ref, wb_ref, carry_ref = rest
    else:
        o_ref, wb_ref = rest
    j = pl.program_id(0)
    t = pl.program_id(1)
    tm = x_ref.shape[0]

    @pl.when(t == 0)
    def _():
        wb_ref[...] = w_ref[...].T.astype(BF16)

    if conv:
        @pl.when(t % tiles_per_seq == 0)
        def _():
            carry_ref[...] = jnp.zeros_like(carry_ref)

    x = x_ref[...]
    for c0 in range(0, o_ref.shape[1], chunk):
        cols = slice(c0, c0 + chunk)
        acc = jnp.dot(x, wb_ref[:, cols], preferred_element_type=F32)
        if conv:
            tail = carry_ref[:, cols]
            carry_ref[:, cols] = acc[tm - _SUBLANES:, :]
            acc = _causal_conv(acc, tail, cw_ref.at[:, cols], cb_ref.at[:, cols])
        if epilogue.endswith("silu"):
            acc = acc * _sigmoid(acc)
        elif epilogue == "gelu":
            acc = jax.nn.gelu(acc)
        elif epilogue == "scale":
            acc = acc * jnp.where(j < scale_cols, scale, 1.0)
        o_ref[:, cols] = acc.astype(o_ref.dtype)


def _proj_w32(x, w_all, layer, col0, n, out_dtype, tm, tn, name, epilogue="none", conv=None, seq_len=None,
              scale_cols=0, scale=1.0, chunk=256):
    m, k = x.shape
    tm, tn = min(tm, m), min(tn, n)
    assert col0 % tn == 0 and n % tn == 0 and m % tm == 0 and tn % chunk == 0
    jb = col0 // tn
    in_specs = [pl.BlockSpec((tm, k), lambda j, t: (t, 0)),
                pl.BlockSpec((None, tn, k), lambda j, t: (layer, jb + j, 0))]
    args = [x, w_all]
    scratch = [pltpu.VMEM((k, tn), BF16)]
    tiles_per_seq = 1
    if conv is not None:
        assert seq_len % tm == 0
        tiles_per_seq = seq_len // tm
        in_specs += [pl.BlockSpec((CONV_K, tn), lambda j, t: (0, j)),
                     pl.BlockSpec((1, tn), lambda j, t: (0, j))]
        args += [conv[0], conv[1].reshape(1, n)]
        scratch.append(pltpu.VMEM((_SUBLANES, tn), F32))
    return pl.pallas_call(
        functools.partial(_proj_w32_body, epilogue=epilogue, tiles_per_seq=tiles_per_seq, chunk=chunk,
                          scale_cols=scale_cols, scale=scale),
        grid=(n // tn, m // tm),
        in_specs=in_specs,
        out_specs=pl.BlockSpec((tm, tn), lambda j, t: (t, j)),
        out_shape=jax.ShapeDtypeStruct((m, n), out_dtype),
        scratch_shapes=scratch,
        compiler_params=_params("arbitrary", "arbitrary"),
        name=name,
    )(*args)


def _mm_ln_body(*refs, alpha, n_a, nk):
    a_refs = refs[:n_a]
    w_ref, x_ref, g_ref, b_ref, of_ref, ob_ref = refs[n_a:n_a + 6]
    ka = a_refs[0].shape[1]
    part = jnp.dot(a_refs[0][...], w_ref[:ka, :], preferred_element_type=F32)
    for idx in range(1, n_a):
        part = part + jnp.dot(a_refs[idx][...], w_ref[idx * ka:(idx + 1) * ka, :], preferred_element_type=F32)

    def finish(acc):
        y = _layer_norm(alpha * x_ref[...] + acc, g_ref[...], b_ref[...])
        of_ref[...] = y
        ob_ref[...] = y.astype(BF16)

    if nk == 1:
        finish(part)
        return
    acc_ref = refs[n_a + 6]
    k = pl.program_id(1)

    @pl.when(k == 0)
    def _():
        acc_ref[...] = part

    @pl.when(k > 0)
    def _():
        acc_ref[...] += part

    @pl.when(k == nk - 1)
    def _():
        finish(acc_ref[...])


def _matmul_residual_ln(a_list, w_all, layer, x, g, b, alpha, tm, tk, name):
    n_a = len(a_list)
    m, ka = a_list[0].shape
    n = w_all.shape[2]
    tm = min(tm, m)
    tka = ka if n_a > 1 else min(tk, ka)
    nk = ka // tka
    w_mode = dict(pipeline_mode=pl.Buffered(1)) if nk == 1 else {}
    return pl.pallas_call(
        functools.partial(_mm_ln_body, alpha=alpha, n_a=n_a, nk=nk),
        grid=(m // tm, nk),
        in_specs=[pl.BlockSpec((tm, tka), lambda i, k: (i, k))] * n_a + [
                  pl.BlockSpec((None, n_a * tka, n), lambda i, k: (layer, k, 0), **w_mode),
                  pl.BlockSpec((tm, n), lambda i, k: (i, 0)),
                  pl.BlockSpec((1, n), lambda i, k: (0, 0)),
                  pl.BlockSpec((1, n), lambda i, k: (0, 0))],
        out_specs=[pl.BlockSpec((tm, n), lambda i, k: (i, 0)),
                   pl.BlockSpec((tm, n), lambda i, k: (i, 0))],
        out_shape=[jax.ShapeDtypeStruct((m, n), F32), jax.ShapeDtypeStruct((m, n), BF16)],
        scratch_shapes=[pltpu.VMEM((tm, n), F32)] if nk > 1 else [],
        compiler_params=_params("parallel", "arbitrary"),
        name=name,
    )(*a_list, w_all, x, g.reshape(1, n), b.reshape(1, n))


def _fox_prep_body(fl_ref, bf_ref, f_ref):
    x = fl_ref[0] + bf_ref[...]
    f_ref[0] = _cumsum_lanes(-_softplus(-x)) * LOG2E


def _fox_prep(fl_t, b_f):
    bn, h, s = fl_t.shape
    return pl.pallas_call(
        _fox_prep_body,
        grid=(bn,),
        in_specs=[pl.BlockSpec((1, h, s), lambda b: (b, 0, 0)),
                  pl.BlockSpec((h, 1), lambda b: (0, 0))],
        out_specs=pl.BlockSpec((1, h, s), lambda b: (b, 0, 0)),
        out_shape=jax.ShapeDtypeStruct((bn, h, s), F32),
        compiler_params=_params("parallel"),
        name="fox_prep",
    )(fl_t, b_f.reshape(h, 1))


def _lru_body(xc_ref, gg_ref, wr_ref, br_ref, wi_ref, bi_ref, lam_ref, y_ref, h_ref, acum_ref, hloc_ref,
              *, heads, block):
    @pl.when(pl.program_id(1) == 0)
    def _():
        h_ref[...] = jnp.zeros_like(h_ref)

    xcb = xc_ref[0]
    xc = xcb.astype(F32)
    ts = xc.shape[0]
    r_parts, i_parts = [], []
    for hd in range(heads):
        xh = xcb[:, hd * block:(hd + 1) * block]
        r_parts.append(jnp.dot(xh, wr_ref[hd], preferred_element_type=F32))
        i_parts.append(jnp.dot(xh, wi_ref[hd], preferred_element_type=F32))
    r = _sigmoid(jnp.concatenate(r_parts, axis=1) + br_ref[...])
    ig = _sigmoid(jnp.concatenate(i_parts, axis=1) + bi_ref[...])
    log_a = -LRU_C * r * _softplus(-lam_ref[...])
    a = jnp.exp(log_a)
    u = jnp.sqrt(-jnp.tanh(log_a) * (a * a + 1.0)) * (ig * xc)

    width = a.shape[1]
    a3 = a.reshape(ts // _SUBLANES, _SUBLANES, width)
    u3 = u.reshape(ts // _SUBLANES, _SUBLANES, width)
    row = lax.broadcasted_iota(jnp.int32, a3.shape, 1)
    d = 1
    while d < _SUBLANES:
        keep = row >= d
        u3 = u3 + a3 * jnp.where(keep, pltpu.roll(u3, d, axis=1), 0.0)
        a3 = a3 * jnp.where(keep, pltpu.roll(a3, d, axis=1), 1.0)
        d *= 2
    acum_ref[...] = a3
    hloc_ref[...] = u3

    def carry_group(g, h_prev):
        hg = hloc_ref[g] + acum_ref[g] * h_prev
        hloc_ref[g] = hg
        return hg[_SUBLANES - 1:_SUBLANES, :]

    h_ref[...] = lax.fori_loop(0, ts // _SUBLANES, carry_group, h_ref[...], unroll=4)
    h = hloc_ref[...].reshape(ts, width)
    y_ref[0] = (h * gg_ref[0].astype(F32)).astype(y_ref.dtype)


def _lru(xc, gg, w_r, b_r, w_i, b_i, lam, ts):
    bn, s, width = xc.shape
    heads, block, _ = w_r.shape
    ts = min(ts, s)
    vec = lambda v: v.reshape(1, width)
    const2 = lambda b, i: (0, 0)
    const3 = lambda b, i: (0, 0, 0)
    tile = pl.BlockSpec((1, ts, width), lambda b, i: (b, i, 0))
    return pl.pallas_call(
        functools.partial(_lru_body, heads=heads, block=block),
        grid=(bn, s // ts),
        in_specs=[tile, tile,
                  pl.BlockSpec((heads, block, block), const3),
                  pl.BlockSpec((1, width), const2),
                  pl.BlockSpec((heads, block, block), const3),
                  pl.BlockSpec((1, width), const2),
                  pl.BlockSpec((1, width), const2)],
        out_specs=tile,
        out_shape=jax.ShapeDtypeStruct((bn, s, width), BF16),
        scratch_shapes=[pltpu.VMEM((1, width), F32),
                        pltpu.VMEM((ts // _SUBLANES, _SUBLANES, width), F32),
                        pltpu.VMEM((ts // _SUBLANES, _SUBLANES, width), F32)],
        compiler_params=_params("parallel", "arbitrary"),
        name="rg_lru",
    )(xc, gg, w_r.astype(BF16), vec(b_r), w_i.astype(BF16), vec(b_i), vec(lam))


def _split3(f):
    hi = f.astype(BF16).astype(F32)
    r = f - hi
    mid = r.astype(BF16).astype(F32)
    return hi, mid, r - mid


def _fox_attn_body(q_ref, k_ref, v_ref, fc_ref, o_ref, kx_ref, *, tq, hp, dh):
    qi = pl.program_id(2)
    s_len = k_ref.shape[1]
    heads = range(hp)
    lane = lax.broadcasted_iota(jnp.int32, (tq, _LANES), 1)

    @pl.when(qi == 0)
    def _():
        for e in heads:
            for c0 in range(0, s_len, tq):
                hi, mid, lo = _split3(fc_ref[0, e, c0:c0 + tq, :])
                kx = jnp.where(lane < 3, 1.0, jnp.where(lane == 3, -hi, jnp.where(lane == 4, -mid,
                               jnp.where(lane == 5, -lo, 0.0))))
                kx_ref[e, c0:c0 + tq, :] = kx.astype(BF16)

    qs = []
    for e in heads:
        hi, mid, lo = _split3(fc_ref[0, e, pl.ds(pl.multiple_of(qi * tq, tq), tq), :])
        qx = jnp.where(lane == 0, hi, jnp.where(lane == 1, mid, jnp.where(lane == 2, lo,
                       jnp.where(lane < 6, 1.0, 0.0))))
        qs.append(jnp.concatenate([q_ref[0, :, e * dh:(e + 1) * dh], qx.astype(BF16)], axis=1))

    def scores(e, j):
        start = pl.multiple_of(j * tq, tq)
        k = jnp.concatenate([k_ref[0, pl.ds(start, tq), e * dh:(e + 1) * dh], kx_ref[e, pl.ds(start, tq), :]], axis=1)
        return lax.dot_general(qs[e], k, (((1,), (1,)), ((), ())), preferred_element_type=F32)

    def update(e, carry, sc, j):
        m, l, acc = carry
        m_new = jnp.maximum(m, jnp.max(sc, axis=1, keepdims=True))
        corr = jnp.exp2(m - m_new)
        p = jnp.exp2(sc - m_new)
        l = corr * l + jnp.sum(p, axis=1, keepdims=True)
        v = v_ref[0, pl.ds(pl.multiple_of(j * tq, tq), tq), e * dh:(e + 1) * dh]
        acc = corr * acc + jnp.dot(p.astype(BF16), v, preferred_element_type=F32)
        return m_new, l, acc

    def full_blocks(carries, js):
        scs = [[scores(e, j) for j in js] for e in heads]
        out = []
        for e in heads:
            c = carries[e]
            for sc, j in zip(scs[e], js):
                c = update(e, c, sc, j)
            out.append(c)
        return tuple(out)

    def diagonal(carries):
        scs = [scores(e, qi) for e in heads]
        rq = lax.broadcasted_iota(jnp.int32, (tq, tq), 0)
        ck = lax.broadcasted_iota(jnp.int32, (tq, tq), 1)
        return tuple(update(e, carries[e], jnp.where(ck <= rq, scs[e], -1e30), qi) for e in heads)

    def two_blocks(jj, carries):
        return full_blocks(carries, (2 * jj, 2 * jj + 1))

    def odd_tail(carries):
        return diagonal(full_blocks(carries, (qi - 1,)))

    init = tuple((jnp.full((tq, 1), -1e30, F32), jnp.zeros((tq, 1), F32), jnp.zeros((tq, dh), F32))
                 for _ in heads)
    carries = lax.fori_loop(0, qi // 2, two_blocks, init)
    carries = lax.cond(qi % 2 == 1, odd_tail, diagonal, carries)
    o_ref[0] = jnp.concatenate([acc / l for _, l, acc in carries], axis=1).astype(o_ref.dtype)


def _fox_attention(qkv, f2, heads, dh, tq, hp):
    bn, s, _ = qkv.shape
    tq = min(tq, s)
    assert heads % hp == 0
    hb = heads // hp
    return pl.pallas_call(
        functools.partial(_fox_attn_body, tq=tq, hp=hp, dh=dh),
        grid=(bn, hb, s // tq),
        in_specs=[pl.BlockSpec((1, tq, hp * dh), lambda b, h, i: (b, i, h)),
                  pl.BlockSpec((1, s, hp * dh), lambda b, h, i: (b, 0, hb + h)),
                  pl.BlockSpec((1, s, hp * dh), lambda b, h, i: (b, 0, 2 * hb + h)),
                  pl.BlockSpec((1, hp, s, 1), lambda b, h, i: (b, h, 0, 0))],
        out_specs=pl.BlockSpec((1, tq, hp * dh), lambda b, h, i: (b, i, h)),
        out_shape=jax.ShapeDtypeStruct((bn, s, heads * dh), BF16),
        scratch_shapes=[pltpu.VMEM((hp, s, _LANES), BF16)],
        compiler_params=_params("parallel", "parallel", "arbitrary"),
        name="fox_attention",
    )(qkv, qkv, qkv, f2[..., None])


def _ssd_body(xbc_ref, z_ref, dt_ref, dtb_ref, alog_ref, dexp_ref, ng_ref, o_ref,
              state_ref, acst_ref, dtt_ref, wt_ref, *, groups, inner):
    L = SSD_CHUNK
    gw = inner // groups
    pairs = gw // _LANES
    hpg = gw // SSD_HEAD_DIM

    @pl.when(pl.program_id(1) == 0)
    def _():
        state_ref[...] = jnp.zeros_like(state_ref)

    dt = _softplus(dt_ref[0] + dtb_ref[...])
    acs = _cumsum_rows(dt * (-jnp.exp(alog_ref[...]))) * LOG2E
    acs_t = acs.T
    dt_t = dt.T
    acst_ref[...] = acs_t
    dtt_ref[...] = dt_t
    wt_ref[...] = dt_t * jnp.exp2(acs_t[:, L - 1:L] - acs_t)
    dec_last = jnp.exp2(acs[L - 1:L, :])

    rq = lax.broadcasted_iota(jnp.int32, (L, L), 0)
    ck = lax.broadcasted_iota(jnp.int32, (L, L), 1)
    causal = ck <= rq
    lo = lax.broadcasted_iota(jnp.int32, (L, _LANES), 1) < SSD_HEAD_DIM
    lo_row = lo[:1]

    for g in range(groups):
        bmat = xbc_ref[0, :, inner + g * SSD_STATE:inner + (g + 1) * SSD_STATE]
        cmat = xbc_ref[0, :, inner + (groups + g) * SSD_STATE:inner + (groups + g + 1) * SSD_STATE]
        cbm = lax.dot_general(cmat, bmat, (((1,), (1,)), ((), ())), preferred_element_type=F32)
        b_t = bmat.astype(F32).T
        ys = []
        for j in range(pairs):
            col0 = g * gw + j * _LANES
            xs_p = xbc_ref[0, :, col0:col0 + _LANES].astype(F32)
            m_parts, bw_parts, cols, dl = [], [], [], []
            for e in range(2):
                hd = g * hpg + 2 * j + e
                col = jnp.broadcast_to(acs[:, hd:hd + 1], (L, L))
                row = acst_ref[hd:hd + 1, :]
                dec = jnp.exp2(jnp.where(causal, col - row, -jnp.inf))
                m_parts.append((cbm * dec * dtt_ref[hd:hd + 1, :]).astype(BF16))
                bw_parts.append((b_t * wt_ref[hd:hd + 1, :]).astype(BF16))
                cols.append(col)
                dl.append(dec_last[:, hd:hd + 1])
            prev = state_ref[:, col0:col0 + _LANES]
            x2 = jnp.concatenate([jnp.where(lo, xs_p, 0.0).astype(BF16),
                                  jnp.where(lo, 0.0, xs_p).astype(BF16)], axis=0)
            y_diag = jnp.dot(jnp.concatenate(m_parts, axis=1), x2, preferred_element_type=F32)
            y_off = jnp.dot(cmat, prev.astype(BF16), preferred_element_type=F32)
            y = y_diag + jnp.exp2(jnp.where(lo, cols[0], cols[1])) * y_off
            st = jnp.dot(jnp.concatenate(bw_parts, axis=1), x2, preferred_element_type=F32)
            state_ref[:, col0:col0 + _LANES] = prev * jnp.where(lo_row, dl[0], dl[1]) + st
            y = y + dexp_ref[:, col0:col0 + _LANES] * xs_p
            ys.append(y * z_ref[0, :, col0:col0 + _LANES].astype(F32))
        yg = jnp.concatenate(ys, axis=1)
        ms = jnp.mean(yg * yg, axis=1, keepdims=True)
        gcols = slice(g * gw, (g + 1) * gw)
        o_ref[0, :, gcols] = (yg * lax.rsqrt(ms + RMS_EPS) * ng_ref[:, gcols]).astype(o_ref.dtype)


def _ssd_core(xbc, zs, dt, dt_bias, a_log, d_skip, norm_g, groups):
    bn, s, conv_dim = xbc.shape
    inner = zs.shape[2]
    heads = a_log.shape[0]
    L = SSD_CHUNK
    pad = lambda v: jnp.pad(v, (0, _LANES - heads)).reshape(1, _LANES)
    const2 = lambda b, c: (0, 0)
    return pl.pallas_call(
        functools.partial(_ssd_body, groups=groups, inner=inner),
        grid=(bn, s // L),
        in_specs=[pl.BlockSpec((1, L, conv_dim), lambda b, c: (b, c, 0)),
                  pl.BlockSpec((1, L, inner), lambda b, c: (b, c, 0)),
                  pl.BlockSpec((1, L, _LANES), lambda b, c: (b, c, 0)),
                  pl.BlockSpec((1, _LANES), const2),
                  pl.BlockSpec((1, _LANES), const2),
                  pl.BlockSpec((1, inner), const2),
                  pl.BlockSpec((1, inner), const2)],
        out_specs=pl.BlockSpec((1, L, inner), lambda b, c: (b, c, 0)),
        out_shape=jax.ShapeDtypeStruct((bn, s, inner), BF16),
        scratch_shapes=[pltpu.VMEM((SSD_STATE, inner), F32),
                        pltpu.VMEM((_LANES, L), F32),
                        pltpu.VMEM((_LANES, L), F32),
                        pltpu.VMEM((_LANES, L), F32)],
        compiler_params=_params("parallel", "arbitrary"),
        name="ssd_core",
    )(xbc, zs, dt, pad(dt_bias), pad(a_log),
      jnp.repeat(d_skip, SSD_HEAD_DIM).reshape(1, inner), norm_g.reshape(1, inner))


def _router_body(x_ref, whl_ref, wh_ref, b_ref, oi_ref, of_ref, cnt_ref, base_ref):
    i = pl.program_id(0)

    @pl.when(i == 0)
    def _():
        base_ref[...] = jnp.zeros_like(base_ref)

    x = x_ref[...]
    tm = x.shape[0]
    xh = x.astype(BF16)
    xl = (x - xh.astype(F32)).astype(BF16)
    r = jnp.dot(xh, whl_ref[...], preferred_element_type=F32)
    logits = (r[:, :ROUTER_COLS] + r[:, ROUTER_COLS:]
              + jnp.dot(xl, wh_ref[...], preferred_element_type=F32) + b_ref[...])
    lt = logits.T

    row = lax.broadcasted_iota(jnp.int32, (_SUBLANES, tm), 0)
    neg = -jnp.inf
    l1 = jnp.where(row < MOE_GROUPS, lt[:_SUBLANES], neg)
    e1 = jnp.exp(l1 - jnp.max(l1, axis=0, keepdims=True))
    p1 = e1 / jnp.sum(e1, axis=0, keepdims=True)
    pg = jnp.max(p1, axis=0, keepdims=True)
    gidx = jnp.min(jnp.where(p1 == pg, row, _SUBLANES), axis=0, keepdims=True)

    sel = jnp.full((_SUBLANES, tm), neg, F32)
    for gi in range(MOE_GROUPS):
        r0 = ROUTER_L2_ROW0 + _SUBLANES * gi
        sel = jnp.where((gidx == gi) & (row < MOE_PER_GROUP), lt[r0:r0 + _SUBLANES], sel)
    va = jnp.max(sel, axis=0, keepdims=True)
    ia = jnp.min(jnp.where(sel == va, row, _SUBLANES), axis=0, keepdims=True)
    sel_b = jnp.where(row == ia, neg, sel)
    vb = jnp.max(sel_b, axis=0, keepdims=True)
    ib = jnp.min(jnp.where(sel_b == vb, row, _SUBLANES), axis=0, keepdims=True)
    t = jnp.exp(vb - va)
    ca = pg * (1.0 / (1.0 + t))
    cb = pg * (t / (1.0 + t))
    ea = gidx * MOE_PER_GROUP + ia
    eb = gidx * MOE_PER_GROUP + ib

    erow = lax.broadcasted_iota(jnp.int32, (N_EXPERTS, tm), 0)
    hit_a = erow == ea
    hit_b = erow == eb
    oh = jnp.where(hit_a, 1.0, 0.0) + jnp.where(hit_b, 1.0, 0.0)
    before = (lax.broadcasted_iota(jnp.int32, (tm, tm), 0)
              < lax.broadcasted_iota(jnp.int32, (tm, tm), 1))
    prefix = jnp.dot(oh.astype(BF16), jnp.where(before, 1.0, 0.0).astype(BF16),
                     preferred_element_type=F32) + base_ref[...]
    rank_a = jnp.sum(jnp.where(hit_a, prefix, 0.0), axis=0, keepdims=True)
    rank_b = jnp.sum(jnp.where(hit_b, prefix, 0.0), axis=0, keepdims=True)
    base_ref[...] += jnp.sum(oh, axis=1, keepdims=True)
    cnt_ref[...] = jnp.broadcast_to(base_ref[...], cnt_ref.shape)

    zi = jnp.zeros((1, tm), jnp.int32)
    oi_ref[...] = jnp.concatenate(
        [ea, eb, rank_a.astype(jnp.int32), rank_b.astype(jnp.int32), zi, zi, zi, zi], axis=0)
    zf = jnp.zeros((1, tm), F32)
    of_ref[...] = jnp.concatenate([ca, cb, zf, zf, zf, zf, zf, zf], axis=0)


def _router(x, w_r1, b_r1, w_r2, b_r2, tm):
    t, d = x.shape
    tm = min(tm, t)
    def slots(first, per_group):
        pad8 = lambda v: jnp.pad(v, [(0, 0)] * (v.ndim - 1) + [(0, _SUBLANES - v.shape[-1])])
        parts = [pad8(first)] + [pad8(per_group[gi]) for gi in range(MOE_GROUPS)]
        out = jnp.concatenate(parts, axis=-1)
        return jnp.pad(out, [(0, 0)] * (out.ndim - 1) + [(0, ROUTER_COLS - out.shape[-1])])

    assert ROUTER_L2_ROW0 == _SUBLANES and MOE_GROUPS <= _SUBLANES and MOE_PER_GROUP <= _SUBLANES
    w = slots(w_r1, w_r2)
    b = slots(b_r1, b_r2)
    wh = w.astype(BF16)
    wl = (w - wh.astype(F32)).astype(BF16)
    const = lambda i: (0, 0)
    return pl.pallas_call(
        _router_body,
        grid=(t // tm,),
        in_specs=[pl.BlockSpec((tm, d), lambda i: (i, 0)),
                  pl.BlockSpec((d, 2 * ROUTER_COLS), const),
                  pl.BlockSpec((d, ROUTER_COLS), const),
                  pl.BlockSpec((1, ROUTER_COLS), const)],
        out_specs=[pl.BlockSpec((_SUBLANES, tm), lambda i: (0, i)),
                   pl.BlockSpec((_SUBLANES, tm), lambda i: (0, i)),
                   pl.BlockSpec((N_EXPERTS, _LANES), const)],
        out_shape=[jax.ShapeDtypeStruct((_SUBLANES, t), jnp.int32),
                   jax.ShapeDtypeStruct((_SUBLANES, t), F32),
                   jax.ShapeDtypeStruct((N_EXPERTS, _LANES), F32)],
        scratch_shapes=[pltpu.VMEM((N_EXPERTS, 1), F32)],
        compiler_params=_params("arbitrary"),
        name="moe_router",
    )(x, jnp.concatenate([wh, wl], axis=1), wh, b.reshape(1, ROUTER_COLS))


def _row_copy(src, src_row, dst, dst_row, sem):
    return pltpu.make_async_copy(src.at[pl.ds(src_row, 1)], dst.at[pl.ds(dst_row, 1)], sem)


def _dispatch_body(pos_ref, zt_ref, x_ref, xs_ref, zeros_ref, sem, zsem, *, td, tz, t_total):
    step = pl.program_id(0)
    base = step * td

    def zero_tile(i):
        return pltpu.make_async_copy(zeros_ref, xs_ref.at[pl.ds(pl.multiple_of(i * tz, tz), tz)], zsem)

    @pl.when(step == 0)
    def _():
        zeros_ref[...] = jnp.zeros_like(zeros_ref)

        def start(i, carry):
            @pl.when(zt_ref[i] == 1)
            def _():
                zero_tile(i).start()
            return carry

        def wait(i, carry):
            @pl.when(zt_ref[i] == 1)
            def _():
                zero_tile(i).wait()
            return carry

        lax.fori_loop(0, zt_ref.shape[0], start, 0)
        lax.fori_loop(0, zt_ref.shape[0], wait, 0)

    def issue(r, carry):
        _row_copy(x_ref, r, xs_ref, pos_ref[base + r], sem).start()
        _row_copy(x_ref, r, xs_ref, pos_ref[t_total + base + r], sem).start(priority=1)
        return carry

    lax.fori_loop(0, td, issue, 0, unroll=4)
    tile_rows = xs_ref.at[pl.ds(0, td)]
    pltpu.make_async_copy(x_ref, tile_rows, sem).wait()
    pltpu.make_async_copy(x_ref, tile_rows, sem).wait()


def _dispatch(x, pos, zero_tiles, n_rows, td, tz):
    t, d = x.shape
    td = min(td, t)
    return pl.pallas_call(
        functools.partial(_dispatch_body, td=td, tz=tz, t_total=t),
        grid_spec=pltpu.PrefetchScalarGridSpec(
            num_scalar_prefetch=2,
            grid=(t // td,),
            in_specs=[pl.BlockSpec((td, d), lambda i, pos, zt: (i, 0))],
            out_specs=pl.BlockSpec(memory_space=pl.ANY),
            scratch_shapes=[pltpu.VMEM((tz, d), x.dtype),
                            pltpu.SemaphoreType.DMA(()), pltpu.SemaphoreType.DMA(())]),
        out_shape=jax.ShapeDtypeStruct((n_rows, d), x.dtype),
        compiler_params=_params("arbitrary"),
        name="moe_dispatch",
    )(pos, zero_tiles, x)


def _expert_body(blk_ref, te_ref, nv_ref, xs_ref, wg_ref, wu_ref, wd_ref, o_ref):
    del blk_ref, te_ref
    valid = pl.program_id(0) < nv_ref[0]

    @pl.when(jnp.logical_not(valid))
    def _():
        o_ref[...] = jnp.zeros_like(o_ref)

    @pl.when(valid)
    def _():
        x = xs_ref[...].astype(BF16)
        g = jnp.dot(x, wg_ref[0], preferred_element_type=F32)
        u = jnp.dot(x, wu_ref[0], preferred_element_type=F32)
        hid = (g * _sigmoid(g)) * u
        o_ref[...] = jnp.dot(hid.astype(BF16), wd_ref[0], preferred_element_type=F32)


def _experts(xs, w_gate, w_up, w_down, expert0, blk, tile_expert, n_valid, tm):
    p, d = xs.shape
    ff = w_gate.shape[2]
    n_tiles = p // tm
    rows = lambda i, blk, te, nv: (blk[i], 0)
    wsel = lambda i, blk, te, nv: (expert0 + te[i], 0, 0)
    return pl.pallas_call(
        _expert_body,
        grid_spec=pltpu.PrefetchScalarGridSpec(
            num_scalar_prefetch=3,
            grid=(n_tiles,),
            in_specs=[pl.BlockSpec((tm, d), rows),
                      pl.BlockSpec((1, d, ff), wsel),
                      pl.BlockSpec((1, d, ff), wsel),
                      pl.BlockSpec((1, ff, d), wsel)],
            out_specs=pl.BlockSpec((tm, d), lambda i, blk, te, nv: (i, 0))),
        out_shape=jax.ShapeDtypeStruct((p, d), F32),
        compiler_params=_params("arbitrary"),
        name="moe_experts",
    )(blk, tile_expert, n_valid, xs, w_gate, w_up, w_down)


def _combine_body(pos_ref, ys_ref, x_ref, c_ref, g_ref, b_ref, of_ref, ob_ref, buf_a, buf_b, sem,
                  *, tc, t_total, alpha):
    step = pl.program_id(0)
    slot = step % 2

    def issue(tile, which):
        base = tile * tc

        def body(r, carry):
            _row_copy(ys_ref, pos_ref[base + r], buf_a.at[which], r, sem.at[which]).start()
            _row_copy(ys_ref, pos_ref[t_total + base + r], buf_b.at[which], r, sem.at[which]).start(priority=1)
            return carry

        lax.fori_loop(0, tc, body, 0, unroll=4)

    @pl.when(step == 0)
    def _():
        issue(0, 0)

    @pl.when(step + 1 < pl.num_programs(0))
    def _():
        issue(step + 1, 1 - slot)

    tile_rows = ys_ref.at[pl.ds(0, tc)]
    pltpu.make_async_copy(tile_rows, buf_a.at[slot], sem.at[slot]).wait()
    pltpu.make_async_copy(tile_rows, buf_b.at[slot], sem.at[slot]).wait()
    c = c_ref[...]
    f = c[:, 0:1] * buf_a[slot] + c[:, 1:2] * buf_b[slot]
    y = _layer_norm(alpha * x_ref[...] + f, g_ref[...], b_ref[...])
    of_ref[...] = y
    ob_ref[...] = y.astype(BF16)


def _combine_ln(ys, pos, x, c, g, b, alpha, tc):
    t, d = x.shape
    tc = min(tc, t)
    const = lambda i, pos: (0, 0)
    tile = lambda i, pos: (i, 0)
    return pl.pallas_call(
        functools.partial(_combine_body, tc=tc, t_total=t, alpha=alpha),
        grid_spec=pltpu.PrefetchScalarGridSpec(
            num_scalar_prefetch=1,
            grid=(t // tc,),
            in_specs=[pl.BlockSpec(memory_space=pl.ANY),
                      pl.BlockSpec((tc, d), tile),
                      pl.BlockSpec((tc, 2), tile),
                      pl.BlockSpec((1, d), const),
                      pl.BlockSpec((1, d), const)],
            out_specs=[pl.BlockSpec((tc, d), tile), pl.BlockSpec((tc, d), tile)],
            scratch_shapes=[pltpu.VMEM((2, tc, d), F32), pltpu.VMEM((2, tc, d), F32),
                            pltpu.SemaphoreType.DMA((2,))]),
        out_shape=[jax.ShapeDtypeStruct((t, d), F32), jax.ShapeDtypeStruct((t, d), BF16)],
        compiler_params=_params("arbitrary"),
        name="moe_combine_ln",
    )(pos, ys, x, c, g.reshape(1, d), b.reshape(1, d))


def _moe_ln(x, w_r1, b_r1, w_r2, b_r2, w_gate, w_up, w_down, layer, g, b, alpha, tiles):
    t, d = x.shape
    tm_e = min(tiles["expert_rows"], t)
    oi, of, cnt = _router(x, w_r1, b_r1, w_r2, b_r2, tiles["router"])
    counts = cnt[:, 0].astype(jnp.int32)
    seg_tiles = (counts + tm_e - 1) // tm_e
    tile_end = jnp.cumsum(seg_tiles)
    seg_start = (tile_end - seg_tiles) * tm_e
    start_of = lambda e: jnp.sum(jnp.where(e[:, None] == jnp.arange(N_EXPERTS, dtype=jnp.int32)[None, :],
                                           seg_start[None, :], 0), axis=1)
    pos = jnp.concatenate([start_of(oi[0]) + oi[2], start_of(oi[1]) + oi[3]])
    n_tiles = 2 * t // tm_e + N_EXPERTS
    n_valid = tile_end[-1:]
    tile_id = jnp.arange(n_tiles, dtype=jnp.int32)
    blk = jnp.minimum(tile_id, n_valid - 1)
    tile_expert = jnp.minimum(jnp.sum((blk[:, None] >= tile_end[None, :]).astype(jnp.int32), axis=1), N_EXPERTS - 1)
    zero_tiles = (jnp.any(tile_id[:, None] == (tile_end - 1)[None, :], axis=1) | (tile_id >= n_valid)).astype(jnp.int32)

    xs = _dispatch(x, pos, zero_tiles, n_tiles * tm_e, tiles["dispatch"], tm_e)
    ys = _experts(xs, w_gate, w_up, w_down, layer * N_EXPERTS, blk, tile_expert, n_valid, tm_e)
    return _combine_ln(ys, pos, x, of[:2].T, g, b, alpha, tiles["combine"])


def _hybrid_mixer_ln(xf, xb, bn, j, w_in_all, conv_w, conv_b, w_r, b_r, w_i, b_i, lam, b_f, w_out_all, g, b, alpha, tiles):
    t, d = xf.shape
    s = t // bn
    heads, block, _ = w_r.shape
    lru_w = heads * block
    fox_heads = b_f.shape[0]
    main = w_in_all.shape[1] - fox_heads
    fox_w = (main - 2 * lru_w) // 3
    dh = fox_w // fox_heads
    tm, tn = tiles["mm_m"], tiles["mm_n"]
    xc = _proj_w32(xb, w_in_all, j, 0, lru_w, BF16, tm, tn, "hy_lru_x_proj_conv", "conv", (conv_w, conv_b), s)
    gg = _proj_w32(xb, w_in_all, j, lru_w, lru_w, BF16, tm, tn, "hy_lru_gate_proj_gelu", "gelu")
    qkv = _proj_w32(xb, w_in_all, j, 2 * lru_w, 3 * fox_w, BF16, tm, tn, "hy_qkv_proj", "scale",
                     scale_cols=fox_w // min(tn, fox_w), scale=dh ** -0.5 * LOG2E)
    fl = _proj_small(xb, w_in_all[j, main:, :], tiles["small_m"], "hy_forget_proj")[:, :fox_heads]
    f2 = _fox_prep(fl.reshape(bn, s, fox_heads).transpose(0, 2, 1), b_f)
    y_lru = _lru(xc.reshape(bn, s, lru_w), gg.reshape(bn, s, lru_w), w_r, b_r, w_i, b_i, lam, tiles["lru"])
    y_att = _fox_attention(qkv.reshape(bn, s, 3 * fox_w), f2, fox_heads, dh, tiles["attn"], tiles["attn_heads"])
    return _matmul_residual_ln([y_lru.reshape(t, lru_w), y_att.reshape(t, fox_w)], w_out_all, j, xf, g, b, alpha,
                               tiles["ln_m"], tiles["ln_k"], "hy_out_proj_ln")


def _ssd_mixer_ln(xf, xb, bn, j, w_in_all, conv_w, conv_b, dt_bias, a_log, d_skip, norm_g, w_out_all, g, b, alpha, tiles):
    t, d = xf.shape
    s = t // bn
    heads = a_log.shape[0]
    inner = norm_g.shape[0]
    conv_dim = conv_w.shape[1]
    groups = (conv_dim - inner) // (2 * SSD_STATE)
    tm, tn = tiles["mm_m"], tiles["mm_n"]
    zs = _proj_w32(xb, w_in_all, j, 0, inner, BF16, tm, tn, "ssd_z_proj_silu", "silu")
    xbc = _proj_w32(xb, w_in_all, j, inner, conv_dim, BF16, tm, tn, "ssd_xbc_proj_conv_silu", "conv_silu",
                     (conv_w, conv_b), s)
    dt = _proj_small(xb, w_in_all[j, inner + conv_dim:, :], tiles["small_m"], "ssd_dt_proj")
    y = _ssd_core(xbc.reshape(bn, s, conv_dim), zs.reshape(bn, s, inner), dt.reshape(bn, s, _LANES),
                  dt_bias, a_log, d_skip, norm_g, groups)
    return _matmul_residual_ln([y.reshape(t, inner)], w_out_all, j, xf, g, b, alpha,
                               tiles["ln_m"], tiles["ln_k"], "ssd_out_proj_ln")


_TILES = dict(mm_m=1024, mm_n=1024, small_m=1024, ln_m=512, ln_k=4096, lru=256, attn=512, attn_heads=1,
              router=512, dispatch=1024, expert_rows=512, combine=256)


def kernel(x, ln_g, ln_b, hy_w_in, hy_conv_w, hy_conv_b, hy_w_r, hy_b_r, hy_w_i, hy_b_i, hy_lambda, hy_b_f, hy_w_out, ssd_w_in, ssd_conv_w, ssd_conv_b, ssd_dt_bias, ssd_a_log, ssd_d, ssd_norm_g, ssd_w_out, moe_w_r1, moe_b_r1, moe_w_r2, moe_b_r2, moe_w_gate, moe_w_up, moe_w_down):
    bn, s, d = x.shape
    depth = ln_g.shape[0]
    alpha = (2 * depth) ** 0.25
    ff = moe_w_gate.shape[-1]
    hy_w_in_t = jnp.swapaxes(hy_w_in, 1, 2)
    ssd_w_in_t = jnp.swapaxes(ssd_w_in, 1, 2)
    hy_w_out_b = hy_w_out.astype(BF16)
    ssd_w_out_b = ssd_w_out.astype(BF16)
    w_gate_s = moe_w_gate.reshape(depth * N_EXPERTS, d, ff).astype(BF16)
    w_up_s = moe_w_up.reshape(depth * N_EXPERTS, d, ff).astype(BF16)
    w_down_s = moe_w_down.reshape(depth * N_EXPERTS, ff, d).astype(BF16)
    xf = x.reshape(bn * s, d)
    xb = xf.astype(BF16)
    for layer in range(depth):
        j = layer // 2
        if layer % 2 == 0:
            xf, xb = _hybrid_mixer_ln(xf, xb, bn, j, hy_w_in_t, hy_conv_w[j], hy_conv_b[j], hy_w_r[j], hy_b_r[j],
                                      hy_w_i[j], hy_b_i[j], hy_lambda[j], hy_b_f[j], hy_w_out_b,
                                      ln_g[layer, 0], ln_b[layer, 0], alpha, _TILES)
        else:
            xf, xb = _ssd_mixer_ln(xf, xb, bn, j, ssd_w_in_t, ssd_conv_w[j], ssd_conv_b[j], ssd_dt_bias[j],
                                   ssd_a_log[j], ssd_d[j], ssd_norm_g[j], ssd_w_out_b,
                                   ln_g[layer, 0], ln_b[layer, 0], alpha, _TILES)
        xf, xb = _moe_ln(xf, moe_w_r1[layer], moe_b_r1[layer], moe_w_r2[layer], moe_b_r2[layer],
                         w_gate_s, w_up_s, w_down_s, layer, ln_g[layer, 1], ln_b[layer, 1], alpha, _TILES)
    return xf.reshape(bn, s, d)
```

```python
import functools
import math

import jax
import jax.numpy as jnp
from jax import lax
from jax.experimental import pallas as pl
from jax.experimental.pallas import tpu as pltpu

F32 = jnp.float32
BF16 = jnp.bfloat16

_VMEM_LIMIT_BYTES = 56 * 1024 * 1024
_LANES = 128
_SUBLANES = 8

LN_EPS = 1e-5
RMS_EPS = 1e-5
LRU_C = 8.0
CONV_K = 4
SSD_CHUNK = 128
SSD_HEAD_DIM = 64
SSD_STATE = 128
MOE_GROUPS = 4
MOE_PER_GROUP = 4
N_EXPERTS = MOE_GROUPS * MOE_PER_GROUP
ROUTER_COLS = 128
ROUTER_L2_ROW0 = 8
LOG2E = math.log2(math.e)


def _params(*sem):
    return pltpu.CompilerParams(dimension_semantics=sem, vmem_limit_bytes=_VMEM_LIMIT_BYTES)


def _sigmoid(x):
    return 1.0 / (1.0 + jnp.exp(-x))


def _softplus(x):
    e = jnp.exp(-jnp.abs(x))
    u = 1.0 + e
    return jnp.maximum(x, 0.0) + jnp.where(u == 1.0, e, jnp.log(u) * (e / (u - 1.0)))


def _shift_rows(x, tail, d):
    r = pltpu.roll(x, d, axis=0)
    t = pltpu.roll(tail, d, axis=0)
    row = lax.broadcasted_iota(jnp.int32, tail.shape, 0)
    first = jnp.where(row < d, t, r[:_SUBLANES])
    return jnp.concatenate([first, r[_SUBLANES:]], axis=0)


def _causal_conv(x, tail, w_ref, b_ref):
    acc = b_ref[...] + w_ref[CONV_K - 1:CONV_K, :] * x
    for d in range(1, CONV_K):
        acc = acc + w_ref[CONV_K - 1 - d:CONV_K - d, :] * _shift_rows(x, tail, d)
    return acc


def _cumsum_rows(x):
    n = x.shape[0]
    row = lax.broadcasted_iota(jnp.int32, x.shape, 0)
    d = 1
    while d < n:
        x = x + jnp.where(row >= d, pltpu.roll(x, d, axis=0), 0.0)
        d *= 2
    return x


def _cumsum_lanes(x):
    n = x.shape[1]
    lane = lax.broadcasted_iota(jnp.int32, x.shape, 1)
    d = 1
    while d < n:
        x = x + jnp.where(lane >= d, pltpu.roll(x, d, axis=1), 0.0)
        d *= 2
    return x


def _layer_norm(s, g, b):
    mu = jnp.mean(s, axis=-1, keepdims=True)
    d = s - mu
    var = jnp.mean(d * d, axis=-1, keepdims=True)
    return d * lax.rsqrt(var + LN_EPS) * g + b


def _proj_small_body(x_ref, wt_ref, o_ref):
    o_ref[...] = lax.dot_general(x_ref[...], wt_ref[...].astype(BF16), (((1,), (1,)), ((), ())),
                                 preferred_element_type=F32)


def _proj_small(x, w_t, tm, name):
    m, k = x.shape
    tm = min(tm, m)
    w_t = jnp.pad(w_t, ((0, _LANES - w_t.shape[0]), (0, 0)))
    return pl.pallas_call(
        _proj_small_body,
        grid=(m // tm,),
        in_specs=[pl.BlockSpec((tm, k), lambda i: (i, 0)),
                  pl.BlockSpec((_LANES, k), lambda i: (0, 0))],
        out_specs=pl.BlockSpec((tm, _LANES), lambda i: (i, 0)),
        out_shape=jax.ShapeDtypeStruct((m, _LANES), F32),
        compiler_params=_params("parallel"),
        name=name,
    )(x, w_t)


def _proj_w32_body(x_ref, w_ref, *rest, epilogue, tiles_per_seq, chunk, scale_cols, scale):
    conv = epilogue.startswith("conv")
    if conv:
        cw_ref, cb_ref, o_ref, wb_ref, carry_ref = rest
    else:
        o_ref, wb_ref = rest
    j = pl.program_id(0)
    t = pl.program_id(1)
    tm = x_ref.shape[0]

    @pl.when(t == 0)
    def _():
        wb_ref[...] = w_ref[...].T.astype(BF16)

    if conv:
        @pl.when(t % tiles_per_seq == 0)
        def _():
            carry_ref[...] = jnp.zeros_like(carry_ref)

    x = x_ref[...]
    for c0 in range(0, o_ref.shape[1], chunk):
        cols = slice(c0, c0 + chunk)
        acc = jnp.dot(x, wb_ref[:, cols], preferred_element_type=F32)
        if conv:
            tail = carry_ref[:, cols]
            carry_ref[:, cols] = acc[tm - _SUBLANES:, :]
            acc = _causal_conv(acc, tail, cw_ref.at[:, cols], cb_ref.at[:, cols])
        if epilogue.endswith("silu"):
            acc = acc * _sigmoid(acc)
        elif epilogue == "gelu":
            acc = jax.nn.gelu(acc)
        elif epilogue == "scale":
            acc = acc * jnp.where(j < scale_cols, scale, 1.0)
        o_ref[:, cols] = acc.astype(o_ref.dtype)


def _proj_w32(x, w_all, layer, col0, n, out_dtype, tm, tn, name, epilogue="none", conv=None, seq_len=None,
              scale_cols=0, scale=1.0, chunk=256):
    m, k = x.shape
    tm, tn = min(tm, m), min(tn, n)
    assert col0 % tn == 0 and n % tn == 0 and m % tm == 0 and tn % chunk == 0
    jb = col0 // tn
    in_specs = [pl.BlockSpec((tm, k), lambda j, t: (t, 0)),
                pl.BlockSpec((None, tn, k), lambda j, t: (layer, jb + j, 0))]
    args = [x, w_all]
    scratch = [pltpu.VMEM((k, tn), BF16)]
    tiles_per_seq = 1
    if conv is not None:
        assert seq_len % tm == 0
        tiles_per_seq = seq_len // tm
        in_specs += [pl.BlockSpec((CONV_K, tn), lambda j, t: (0, j)),
                     pl.BlockSpec((1, tn), lambda j, t: (0, j))]
        args += [conv[0], conv[1].reshape(1, n)]
        scratch.append(pltpu.VMEM((_SUBLANES, tn), F32))
    return pl.pallas_call(
        functools.partial(_proj_w32_body, epilogue=epilogue, tiles_per_seq=tiles_per_seq, chunk=chunk,
                          scale_cols=scale_cols, scale=scale),
        grid=(n // tn, m // tm),
        in_specs=in_specs,
        out_specs=pl.BlockSpec((tm, tn), lambda j, t: (t, j)),
        out_shape=jax.ShapeDtypeStruct((m, n), out_dtype),
        scratch_shapes=scratch,
        compiler_params=_params("arbitrary", "arbitrary"),
        name=name,
    )(*args)


def _mm_ln_body(*refs, alpha, n_a, nk):
    a_refs = refs[:n_a]
    w_ref, x_ref, g_ref, b_ref, of_ref, ob_ref = refs[n_a:n_a + 6]
    ka = a_refs[0].shape[1]
    part = jnp.dot(a_refs[0][...], w_ref[:ka, :], preferred_element_type=F32)
    for idx in range(1, n_a):
        part = part + jnp.dot(a_refs[idx][...], w_ref[idx * ka:(idx + 1) * ka, :], preferred_element_type=F32)

    def finish(acc):
        y = _layer_norm(alpha * x_ref[...] + acc, g_ref[...], b_ref[...])
        of_ref[...] = y
        ob_ref[...] = y.astype(BF16)

    if nk == 1:
        finish(part)
        return
    acc_ref = refs[n_a + 6]
    k = pl.program_id(1)

    @pl.when(k == 0)
    def _():
        acc_ref[...] = part

    @pl.when(k > 0)
    def _():
        acc_ref[...] += part

    @pl.when(k == nk - 1)
    def _():
        finish(acc_ref[...])


def _matmul_residual_ln(a_list, w_all, layer, x, g, b, alpha, tm, tk, name):
    n_a = len(a_list)
    m, ka = a_list[0].shape
    n = w_all.shape[2]
    tm = min(tm, m)
    tka = ka if n_a > 1 else min(tk, ka)
    nk = ka // tka
    w_mode = dict(pipeline_mode=pl.Buffered(1)) if nk == 1 else {}
    return pl.pallas_call(
        functools.partial(_mm_ln_body, alpha=alpha, n_a=n_a, nk=nk),
        grid=(m // tm, nk),
        in_specs=[pl.BlockSpec((tm, tka), lambda i, k: (i, k))] * n_a + [
                  pl.BlockSpec((None, n_a * tka, n), lambda i, k: (layer, k, 0), **w_mode),
                  pl.BlockSpec((tm, n), lambda i, k: (i, 0)),
                  pl.BlockSpec((1, n), lambda i, k: (0, 0)),
                  pl.BlockSpec((1, n), lambda i, k: (0, 0))],
        out_specs=[pl.BlockSpec((tm, n), lambda i, k: (i, 0)),
                   pl.BlockSpec((tm, n), lambda i, k: (i, 0))],
        out_shape=[jax.ShapeDtypeStruct((m, n), F32), jax.ShapeDtypeStruct((m, n), BF16)],
        scratch_shapes=[pltpu.VMEM((tm, n), F32)] if nk > 1 else [],
        compiler_params=_params("parallel", "arbitrary"),
        name=name,
    )(*a_list, w_all, x, g.reshape(1, n), b.reshape(1, n))


def _fox_prep_body(fl_ref, bf_ref, f_ref):
    x = fl_ref[0] + bf_ref[...]
    f_ref[0] = _cumsum_lanes(-_softplus(-x)) * LOG2E


def _fox_prep(fl_t, b_f):
    bn, h, s = fl_t.shape
    return pl.pallas_call(
        _fox_prep_body,
        grid=(bn,),
        in_specs=[pl.BlockSpec((1, h, s), lambda b: (b, 0, 0)),
                  pl.BlockSpec((h, 1), lambda b: (0, 0))],
        out_specs=pl.BlockSpec((1, h, s), lambda b: (b, 0, 0)),
        out_shape=jax.ShapeDtypeStruct((bn, h, s), F32),
        compiler_params=_params("parallel"),
        name="fox_prep",
    )(fl_t, b_f.reshape(h, 1))


def _lru_body(xc_ref, gg_ref, wr_ref, br_ref, wi_ref, bi_ref, lam_ref, y_ref, h_ref, acum_ref, hloc_ref,
              *, heads, block):
    @pl.when(pl.program_id(1) == 0)
    def _():
        h_ref[...] = jnp.zeros_like(h_ref)

    xcb = xc_ref[0]
    xc = xcb.astype(F32)
    ts = xc.shape[0]
    r_parts, i_parts = [], []
    for hd in range(heads):
        xh = xcb[:, hd * block:(hd + 1) * block]
        r_parts.append(jnp.dot(xh, wr_ref[hd], preferred_element_type=F32))
        i_parts.append(jnp.dot(xh, wi_ref[hd], preferred_element_type=F32))
    r = _sigmoid(jnp.concatenate(r_parts, axis=1) + br_ref[...])
    ig = _sigmoid(jnp.concatenate(i_parts, axis=1) + bi_ref[...])
    log_a = -LRU_C * r * _softplus(-lam_ref[...])
    a = jnp.exp(log_a)
    u = jnp.sqrt(-jnp.tanh(log_a) * (a * a + 1.0)) * (ig * xc)

    width = a.shape[1]
    a3 = a.reshape(ts // _SUBLANES, _SUBLANES, width)
    u3 = u.reshape(ts // _SUBLANES, _SUBLANES, width)
    row = lax.broadcasted_iota(jnp.int32, a3.shape, 1)
    d = 1
    while d < _SUBLANES:
        keep = row >= d
        u3 = u3 + a3 * jnp.where(keep, pltpu.roll(u3, d, axis=1), 0.0)
        a3 = a3 * jnp.where(keep, pltpu.roll(a3, d, axis=1), 1.0)
        d *= 2
    acum_ref[...] = a3
    hloc_ref[...] = u3

    def carry_group(g, h_prev):
        hg = hloc_ref[g] + acum_ref[g] * h_prev
        hloc_ref[g] = hg
        return hg[_SUBLANES - 1:_SUBLANES, :]

    h_ref[...] = lax.fori_loop(0, ts // _SUBLANES, carry_group, h_ref[...], unroll=4)
    h = hloc_ref[...].reshape(ts, width)
    y_ref[0] = (h * gg_ref[0].astype(F32)).astype(y_ref.dtype)


def _lru(xc, gg, w_r, b_r, w_i, b_i, lam, ts):
    bn, s, width = xc.shape
    heads, block, _ = w_r.shape
    ts = min(ts, s)
    vec = lambda v: v.reshape(1, width)
    const2 = lambda b, i: (0, 0)
    const3 = lambda b, i: (0, 0, 0)
    tile = pl.BlockSpec((1, ts, width), lambda b, i: (b, i, 0))
    return pl.pallas_call(
        functools.partial(_lru_body, heads=heads, block=block),
        grid=(bn, s // ts),
        in_specs=[tile, tile,
                  pl.BlockSpec((heads, block, block), const3),
                  pl.BlockSpec((1, width), const2),
                  pl.BlockSpec((heads, block, block), const3),
                  pl.BlockSpec((1, width), const2),
                  pl.BlockSpec((1, width), const2)],
        out_specs=tile,
        out_shape=jax.ShapeDtypeStruct((bn, s, width), BF16),
        scratch_shapes=[pltpu.VMEM((1, width), F32),
                        pltpu.VMEM((ts // _SUBLANES, _SUBLANES, width), F32),
                        pltpu.VMEM((ts // _SUBLANES, _SUBLANES, width), F32)],
        compiler_params=_params("parallel", "arbitrary"),
        name="rg_lru",
    )(xc, gg, w_r.astype(BF16), vec(b_r), w_i.astype(BF16), vec(b_i), vec(lam))


def _split3(f):
    hi = f.astype(BF16).astype(F32)
    r = f - hi
    mid = r.astype(BF16).astype(F32)
    return hi, mid, r - mid


def _fox_attn_body(q_ref, k_ref, v_ref, fc_ref, o_ref, kx_ref, *, tq, hp, dh):
    qi = pl.program_id(2)
    s_len = k_ref.shape[1]
    heads = range(hp)
    lane = lax.broadcasted_iota(jnp.int32, (tq, _LANES), 1)

    @pl.when(qi == 0)
    def _():
        for e in heads:
            for c0 in range(0, s_len, tq):
                hi, mid, lo = _split3(fc_ref[0, e, c0:c0 + tq, :])
                kx = jnp.where(lane < 3, 1.0, jnp.where(lane == 3, -hi, jnp.where(lane == 4, -mid,
                               jnp.where(lane == 5, -lo, 0.0))))
                kx_ref[e, c0:c0 + tq, :] = kx.astype(BF16)

    qs = []
    for e in heads:
        hi, mid, lo = _split3(fc_ref[0, e, pl.ds(pl.multiple_of(qi * tq, tq), tq), :])
        qx = jnp.where(lane == 0, hi, jnp.where(lane == 1, mid, jnp.where(lane == 2, lo,
                       jnp.where(lane < 6, 1.0, 0.0))))
        qs.append(jnp.concatenate([q_ref[0, :, e * dh:(e + 1) * dh], qx.astype(BF16)], axis=1))

    def scores(e, j):
        start = pl.multiple_of(j * tq, tq)
        k = jnp.concatenate([k_ref[0, pl.ds(start, tq), e * dh:(e + 1) * dh], kx_ref[e, pl.ds(start, tq), :]], axis=1)
        return lax.dot_general(qs[e], k, (((1,), (1,)), ((), ())), preferred_element_type=F32)

    def update(e, carry, sc, j):
        m, l, acc = carry
        m_new = jnp.maximum(m, jnp.max(sc, axis=1, keepdims=True))
        corr = jnp.exp2(m - m_new)
        p = jnp.exp2(sc - m_new)
        l = corr * l + jnp.sum(p, axis=1, keepdims=True)
        v = v_ref[0, pl.ds(pl.multiple_of(j * tq, tq), tq), e * dh:(e + 1) * dh]
        acc = corr * acc + jnp.dot(p.astype(BF16), v, preferred_element_type=F32)
        return m_new, l, acc

    def full_blocks(carries, js):
        scs = [[scores(e, j) for j in js] for e in heads]
        out = []
        for e in heads:
            c = carries[e]
            for sc, j in zip(scs[e], js):
                c = update(e, c, sc, j)
            out.append(c)
        return tuple(out)

    def diagonal(carries):
        scs = [scores(e, qi) for e in heads]
        rq = lax.broadcasted_iota(jnp.int32, (tq, tq), 0)
        ck = lax.broadcasted_iota(jnp.int32, (tq, tq), 1)
        return tuple(update(e, carries[e], jnp.where(ck <= rq, scs[e], -1e30), qi) for e in heads)

    def two_blocks(jj, carries):
        return full_blocks(carries, (2 * jj, 2 * jj + 1))

    def odd_tail(carries):
        return diagonal(full_blocks(carries, (qi - 1,)))

    init = tuple((jnp.full((tq, 1), -1e30, F32), jnp.zeros((tq, 1), F32), jnp.zeros((tq, dh), F32))
                 for _ in heads)
    carries = lax.fori_loop(0, qi // 2, two_blocks, init)
    carries = lax.cond(qi % 2 == 1, odd_tail, diagonal, carries)
    o_ref[0] = jnp.concatenate([acc / l for _, l, acc in carries], axis=1).astype(o_ref.dtype)


def _fox_attention(qkv, f2, heads, dh, tq, hp):
    bn, s, _ = qkv.shape
    tq = min(tq, s)
    assert heads % hp == 0
    hb = heads // hp
    return pl.pallas_call(
        functools.partial(_fox_attn_body, tq=tq, hp=hp, dh=dh),
        grid=(bn, hb, s // tq),
        in_specs=[pl.BlockSpec((1, tq, hp * dh), lambda b, h, i: (b, i, h)),
                  pl.BlockSpec((1, s, hp * dh), lambda b, h, i: (b, 0, hb + h)),
                  pl.BlockSpec((1, s, hp * dh), lambda b, h, i: (b, 0, 2 * hb + h)),
                  pl.BlockSpec((1, hp, s, 1), lambda b, h, i: (b, h, 0, 0))],
        out_specs=pl.BlockSpec((1, tq, hp * dh), lambda b, h, i: (b, i, h)),
        out_shape=jax.ShapeDtypeStruct((bn, s, heads * dh), BF16),
        scratch_shapes=[pltpu.VMEM((hp, s, _LANES), BF16)],
        compiler_params=_params("parallel", "parallel", "arbitrary"),
        name="fox_attention",
    )(qkv, qkv, qkv, f2[..., None])


def _ssd_body(xbc_ref, z_ref, dt_ref, dtb_ref, alog_ref, dexp_ref, ng_ref, o_ref,
              state_ref, acst_ref, dtt_ref, wt_ref, *, groups, inner):
    L = SSD_CHUNK
    gw = inner // groups
    pairs = gw // _LANES
    hpg = gw // SSD_HEAD_DIM

    @pl.when(pl.program_id(1) == 0)
    def _():
        state_ref[...] = jnp.zeros_like(state_ref)

    dt = _softplus(dt_ref[0] + dtb_ref[...])
    acs = _cumsum_rows(dt * (-jnp.exp(alog_ref[...]))) * LOG2E
    acs_t = acs.T
    dt_t = dt.T
    acst_ref[...] = acs_t
    dtt_ref[...] = dt_t
    wt_ref[...] = dt_t * jnp.exp2(acs_t[:, L - 1:L] - acs_t)
    dec_last = jnp.exp2(acs[L - 1:L, :])

    rq = lax.broadcasted_iota(jnp.int32, (L, L), 0)
    ck = lax.broadcasted_iota(jnp.int32, (L, L), 1)
    causal = ck <= rq
    lo = lax.broadcasted_iota(jnp.int32, (L, _LANES), 1) < SSD_HEAD_DIM
    lo_row = lo[:1]

    for g in range(groups):
        bmat = xbc_ref[0, :, inner + g * SSD_STATE:inner + (g + 1) * SSD_STATE]
        cmat = xbc_ref[0, :, inner + (groups + g) * SSD_STATE:inner + (groups + g + 1) * SSD_STATE]
        cbm = lax.dot_general(cmat, bmat, (((1,), (1,)), ((), ())), preferred_element_type=F32)
        b_t = bmat.astype(F32).T
        ys = []
        for j in range(pairs):
            col0 = g * gw + j * _LANES
            xs_p = xbc_ref[0, :, col0:col0 + _LANES].astype(F32)
            m_parts, bw_parts, cols, dl = [], [], [], []
            for e in range(2):
                hd = g * hpg + 2 * j + e
                col = jnp.broadcast_to(acs[:, hd:hd + 1], (L, L))
                row = acst_ref[hd:hd + 1, :]
                dec = jnp.exp2(jnp.where(causal, col - row, -jnp.inf))
                m_parts.append((cbm * dec * dtt_ref[hd:hd + 1, :]).astype(BF16))
                bw_parts.append((b_t * wt_ref[hd:hd + 1, :]).astype(BF16))
                cols.append(col)
                dl.append(dec_last[:, hd:hd + 1])
            prev = state_ref[:, col0:col0 + _LANES]
            x2 = jnp.concatenate([jnp.where(lo, xs_p, 0.0).astype(BF16),
                                  jnp.where(lo, 0.0, xs_p).astype(BF16)], axis=0)
            y_diag = jnp.dot(jnp.concatenate(m_parts, axis=1), x2, preferred_element_type=F32)
            y_off = jnp.dot(cmat, prev.astype(BF16), preferred_element_type=F32)
            y = y_diag + jnp.exp2(jnp.where(lo, cols[0], cols[1])) * y_off
            st = jnp.dot(jnp.concatenate(bw_parts, axis=1), x2, preferred_element_type=F32)
            state_ref[:, col0:col0 + _LANES] = prev * jnp.where(lo_row, dl[0], dl[1]) + st
            y = y + dexp_ref[:, col0:col0 + _LANES] * xs_p
            ys.append(y * z_ref[0, :, col0:col0 + _LANES].astype(F32))
        yg = jnp.concatenate(ys, axis=1)
        ms = jnp.mean(yg * yg, axis=1, keepdims=True)
        gcols = slice(g * gw, (g + 1) * gw)
        o_ref[0, :, gcols] = (yg * lax.rsqrt(ms + RMS_EPS) * ng_ref[:, gcols]).astype(o_ref.dtype)


def _ssd_core(xbc, zs, dt, dt_bias, a_log, d_skip, norm_g, groups):
    bn, s, conv_dim = xbc.shape
    inner = zs.shape[2]
    heads = a_log.shape[0]
    L = SSD_CHUNK
    pad = lambda v: jnp.pad(v, (0, _LANES - heads)).reshape(1, _LANES)
    const2 = lambda b, c: (0, 0)
    return pl.pallas_call(
        functools.partial(_ssd_body, groups=groups, inner=inner),
        grid=(bn, s // L),
        in_specs=[pl.BlockSpec((1, L, conv_dim), lambda b, c: (b, c, 0)),
                  pl.BlockSpec((1, L, inner), lambda b, c: (b, c, 0)),
                  pl.BlockSpec((1, L, _LANES), lambda b, c: (b, c, 0)),
                  pl.BlockSpec((1, _LANES), const2),
                  pl.BlockSpec((1, _LANES), const2),
                  pl.BlockSpec((1, inner), const2),
                  pl.BlockSpec((1, inner), const2)],
        out_specs=pl.BlockSpec((1, L, inner), lambda b, c: (b, c, 0)),
        out_shape=jax.ShapeDtypeStruct((bn, s, inner), BF16),
        scratch_shapes=[pltpu.VMEM((SSD_STATE, inner), F32),
                        pltpu.VMEM((_LANES, L), F32),
                        pltpu.VMEM((_LANES, L), F32),
                        pltpu.VMEM((_LANES, L), F32)],
        compiler_params=_params("parallel", "arbitrary"),
        name="ssd_core",
    )(xbc, zs, dt, pad(dt_bias), pad(a_log),
      jnp.repeat(d_skip, SSD_HEAD_DIM).reshape(1, inner), norm_g.reshape(1, inner))


def _router_body(x_ref, whl_ref, wh_ref, b_ref, oi_ref, of_ref, cnt_ref, base_ref):
    i = pl.program_id(0)

    @pl.when(i == 0)
    def _():
        base_ref[...] = jnp.zeros_like(base_ref)

    x = x_ref[...]
    tm = x.shape[0]
    xh = x.astype(BF16)
    xl = (x - xh.astype(F32)).astype(BF16)
    r = jnp.dot(xh, whl_ref[...], preferred_element_type=F32)
    logits = (r[:, :ROUTER_COLS] + r[:, ROUTER_COLS:]
              + jnp.dot(xl, wh_ref[...], preferred_element_type=F32) + b_ref[...])
    lt = logits.T

    row = lax.broadcasted_iota(jnp.int32, (_SUBLANES, tm), 0)
    neg = -jnp.inf
    l1 = jnp.where(row < MOE_GROUPS, lt[:_SUBLANES], neg)
    e1 = jnp.exp(l1 - jnp.max(l1, axis=0, keepdims=True))
    p1 = e1 / jnp.sum(e1, axis=0, keepdims=True)
    pg = jnp.max(p1, axis=0, keepdims=True)
    gidx = jnp.min(jnp.where(p1 == pg, row, _SUBLANES), axis=0, keepdims=True)

    sel = jnp.full((_SUBLANES, tm), neg, F32)
    for gi in range(MOE_GROUPS):
        r0 = ROUTER_L2_ROW0 + _SUBLANES * gi
        sel = jnp.where((gidx == gi) & (row < MOE_PER_GROUP), lt[r0:r0 + _SUBLANES], sel)
    va = jnp.max(sel, axis=0, keepdims=True)
    ia = jnp.min(jnp.where(sel == va, row, _SUBLANES), axis=0, keepdims=True)
    sel_b = jnp.where(row == ia, neg, sel)
    vb = jnp.max(sel_b, axis=0, keepdims=True)
    ib = jnp.min(jnp.where(sel_b == vb, row, _SUBLANES), axis=0, keepdims=True)
    t = jnp.exp(vb - va)
    ca = pg * (1.0 / (1.0 + t))
    cb = pg * (t / (1.0 + t))
    ea = gidx * MOE_PER_GROUP + ia
    eb = gidx * MOE_PER_GROUP + ib

    erow = lax.broadcasted_iota(jnp.int32, (N_EXPERTS, tm), 0)
    hit_a = erow == ea
    hit_b = erow == eb
    oh = jnp.where(hit_a, 1.0, 0.0) + jnp.where(hit_b, 1.0, 0.0)
    before = (lax.broadcasted_iota(jnp.int32, (tm, tm), 0)
              < lax.broadcasted_iota(jnp.int32, (tm, tm), 1))
    prefix = jnp.dot(oh.astype(BF16), jnp.where(before, 1.0, 0.0).astype(BF16),
                     preferred_element_type=F32) + base_ref[...]
    rank_a = jnp.sum(jnp.where(hit_a, prefix, 0.0), axis=0, keepdims=True)
    rank_b = jnp.sum(jnp.where(hit_b, prefix, 0.0), axis=0, keepdims=True)
    base_ref[...] += jnp.sum(oh, axis=1, keepdims=True)
    cnt_ref[...] = jnp.broadcast_to(base_ref[...], cnt_ref.shape)

    zi = jnp.zeros((1, tm), jnp.int32)
    oi_ref[...] = jnp.concatenate(
        [ea, eb, rank_a.astype(jnp.int32), rank_b.astype(jnp.int32), zi, zi, zi, zi], axis=0)
    zf = jnp.zeros((1, tm), F32)
    of_ref[...] = jnp.concatenate([ca, cb, zf, zf, zf, zf, zf, zf], axis=0)


def _router(x, w_r1, b_r1, w_r2, b_r2, tm):
    t, d = x.shape
    tm = min(tm, t)
    def slots(first, per_group):
        pad8 = lambda v: jnp.pad(v, [(0, 0)] * (v.ndim - 1) + [(0, _SUBLANES - v.shape[-1])])
        parts = [pad8(first)] + [pad8(per_group[gi]) for gi in range(MOE_GROUPS)]
        out = jnp.concatenate(parts, axis=-1)
        return jnp.pad(out, [(0, 0)] * (out.ndim - 1) + [(0, ROUTER_COLS - out.shape[-1])])

    assert ROUTER_L2_ROW0 == _SUBLANES and MOE_GROUPS <= _SUBLANES and MOE_PER_GROUP <= _SUBLANES
    w = slots(w_r1, w_r2)
    b = slots(b_r1, b_r2)
    wh = w.astype(BF16)
    wl = (w - wh.astype(F32)).astype(BF16)
    const = lambda i: (0, 0)
    return pl.pallas_call(
        _router_body,
        grid=(t // tm,),
        in_specs=[pl.BlockSpec((tm, d), lambda i: (i, 0)),
                  pl.BlockSpec((d, 2 * ROUTER_COLS), const),
                  pl.BlockSpec((d, ROUTER_COLS), const),
                  pl.BlockSpec((1, ROUTER_COLS), const)],
        out_specs=[pl.BlockSpec((_SUBLANES, tm), lambda i: (0, i)),
                   pl.BlockSpec((_SUBLANES, tm), lambda i: (0, i)),
                   pl.BlockSpec((N_EXPERTS, _LANES), const)],
        out_shape=[jax.ShapeDtypeStruct((_SUBLANES, t), jnp.int32),
                   jax.ShapeDtypeStruct((_SUBLANES, t), F32),
                   jax.ShapeDtypeStruct((N_EXPERTS, _LANES), F32)],
        scratch_shapes=[pltpu.VMEM((N_EXPERTS, 1), F32)],
        compiler_params=_params("arbitrary"),
        name="moe_router",
    )(x, jnp.concatenate([wh, wl], axis=1), wh, b.reshape(1, ROUTER_COLS))


def _row_copy(src, src_row, dst, dst_row, sem):
    return pltpu.make_async_copy(src.at[pl.ds(src_row, 1)], dst.at[pl.ds(dst_row, 1)], sem)


def _dispatch_body(pos_ref, zt_ref, x_ref, xs_ref, zeros_ref, sem, zsem, *, td, tz, t_total):
    step = pl.program_id(0)
    base = step * td

    def zero_tile(i):
        return pltpu.make_async_copy(zeros_ref, xs_ref.at[pl.ds(pl.multiple_of(i * tz, tz), tz)], zsem)

    @pl.when(step == 0)
    def _():
        zeros_ref[...] = jnp.zeros_like(zeros_ref)

        def start(i, carry):
            @pl.when(zt_ref[i] == 1)
            def _():
                zero_tile(i).start()
            return carry

        def wait(i, carry):
            @pl.when(zt_ref[i] == 1)
            def _():
                zero_tile(i).wait()
            return carry

        lax.fori_loop(0, zt_ref.shape[0], start, 0)
        lax.fori_loop(0, zt_ref.shape[0], wait, 0)

    def issue(r, carry):
        _row_copy(x_ref, r, xs_ref, pos_ref[base + r], sem).start()
        _row_copy(x_ref, r, xs_ref, pos_ref[t_total + base + r], sem).start(priority=1)
        return carry

    lax.fori_loop(0, td, issue, 0, unroll=4)
    tile_rows = xs_ref.at[pl.ds(0, td)]
    pltpu.make_async_copy(x_ref, tile_rows, sem).wait()
    pltpu.make_async_copy(x_ref, tile_rows, sem).wait()


def _dispatch(x, pos, zero_tiles, n_rows, td, tz):
    t, d = x.shape
    td = min(td, t)
    return pl.pallas_call(
        functools.partial(_dispatch_body, td=td, tz=tz, t_total=t),
        grid_spec=pltpu.PrefetchScalarGridSpec(
            num_scalar_prefetch=2,
            grid=(t // td,),
            in_specs=[pl.BlockSpec((td, d), lambda i, pos, zt: (i, 0))],
            out_specs=pl.BlockSpec(memory_space=pl.ANY),
            scratch_shapes=[pltpu.VMEM((tz, d), x.dtype),
                            pltpu.SemaphoreType.DMA(()), pltpu.SemaphoreType.DMA(())]),
        out_shape=jax.ShapeDtypeStruct((n_rows, d), x.dtype),
        compiler_params=_params("arbitrary"),
        name="moe_dispatch",
    )(pos, zero_tiles, x)


def _new_expert(te_ref, i):
    return (i == 0) | (te_ref[i] != te_ref[jnp.maximum(i - 1, 0)])


def _expert_up_body(blk_ref, te_ref, nv_ref, xs_ref, wg_ref, wu_ref, h_ref, wgb_ref, wub_ref):
    del blk_ref
    i = pl.program_id(1)
    valid = i < nv_ref[0]

    @pl.when(jnp.logical_not(valid))
    def _():
        h_ref[...] = jnp.zeros_like(h_ref)

    @pl.when(valid & _new_expert(te_ref, i))
    def _():
        wgb_ref[...] = wg_ref[0].astype(BF16)
        wub_ref[...] = wu_ref[0].astype(BF16)

    @pl.when(valid)
    def _():
        x = xs_ref[...].astype(BF16)
        g = jnp.dot(x, wgb_ref[...], preferred_element_type=F32)
        u = jnp.dot(x, wub_ref[...], preferred_element_type=F32)
        h_ref[...] = ((g * _sigmoid(g)) * u).astype(h_ref.dtype)


def _expert_down_body(te_ref, nv_ref, h_ref, wd_ref, o_ref, wdb_ref):
    i = pl.program_id(0)
    valid = i < nv_ref[0]

    @pl.when(jnp.logical_not(valid))
    def _():
        o_ref[...] = jnp.zeros_like(o_ref)

    @pl.when(valid & _new_expert(te_ref, i))
    def _():
        wdb_ref[...] = wd_ref[0].astype(BF16)

    @pl.when(valid)
    def _():
        o_ref[...] = jnp.dot(h_ref[...], wdb_ref[...], preferred_element_type=F32)


def _experts(xs, w_gate, w_up, w_down, expert0, blk, tile_expert, n_valid, tm, tf):
    p, d = xs.shape
    ff = w_gate.shape[2]
    n_tiles = p // tm
    hid = pl.pallas_call(
        _expert_up_body,
        grid_spec=pltpu.PrefetchScalarGridSpec(
            num_scalar_prefetch=3,
            grid=(ff // tf, n_tiles),
            in_specs=[pl.BlockSpec((tm, d), lambda c, i, blk, te, nv: (blk[i], 0)),
                      pl.BlockSpec((1, d, tf), lambda c, i, blk, te, nv: (expert0 + te[i], 0, c)),
                      pl.BlockSpec((1, d, tf), lambda c, i, blk, te, nv: (expert0 + te[i], 0, c))],
            out_specs=pl.BlockSpec((tm, tf), lambda c, i, blk, te, nv: (i, c)),
            scratch_shapes=[pltpu.VMEM((d, tf), BF16), pltpu.VMEM((d, tf), BF16)]),
        out_shape=jax.ShapeDtypeStruct((p, ff), BF16),
        compiler_params=_params("arbitrary", "arbitrary"),
        name="moe_experts_up",
    )(blk, tile_expert, n_valid, xs, w_gate, w_up)
    return pl.pallas_call(
        _expert_down_body,
        grid_spec=pltpu.PrefetchScalarGridSpec(
            num_scalar_prefetch=2,
            grid=(n_tiles,),
            in_specs=[pl.BlockSpec((tm, ff), lambda i, te, nv: (jnp.minimum(i, nv[0] - 1), 0)),
                      pl.BlockSpec((1, ff, d), lambda i, te, nv: (expert0 + te[i], 0, 0))],
            out_specs=pl.BlockSpec((tm, d), lambda i, te, nv: (i, 0)),
            scratch_shapes=[pltpu.VMEM((ff, d), BF16)]),
        out_shape=jax.ShapeDtypeStruct((p, d), F32),
        compiler_params=_params("arbitrary"),
        name="moe_experts_down",
    )(tile_expert, n_valid, hid, w_down)


def _combine_body(pos_ref, ys_ref, x_ref, c_ref, g_ref, b_ref, of_ref, ob_ref, buf_a, buf_b, sem,
                  *, tc, t_total, alpha):
    step = pl.program_id(0)
    slot = step % 2

    def issue(tile, which):
        base = tile * tc

        def body(r, carry):
            _row_copy(ys_ref, pos_ref[base + r], buf_a.at[which], r, sem.at[which]).start()
            _row_copy(ys_ref, pos_ref[t_total + base + r], buf_b.at[which], r, sem.at[which]).start(priority=1)
            return carry

        lax.fori_loop(0, tc, body, 0, unroll=4)

    @pl.when(step == 0)
    def _():
        issue(0, 0)

    @pl.when(step + 1 < pl.num_programs(0))
    def _():
        issue(step + 1, 1 - slot)

    tile_rows = ys_ref.at[pl.ds(0, tc)]
    pltpu.make_async_copy(tile_rows, buf_a.at[slot], sem.at[slot]).wait()
    pltpu.make_async_copy(tile_rows, buf_b.at[slot], sem.at[slot]).wait()
    c = c_ref[...]
    f = c[:, 0:1] * buf_a[slot] + c[:, 1:2] * buf_b[slot]
    y = _layer_norm(alpha * x_ref[...] + f, g_ref[...], b_ref[...])
    of_ref[...] = y
    ob_ref[...] = y.astype(BF16)


def _combine_ln(ys, pos, x, c, g, b, alpha, tc):
    t, d = x.shape
    tc = min(tc, t)
    const = lambda i, pos: (0, 0)
    tile = lambda i, pos: (i, 0)
    return pl.pallas_call(
        functools.partial(_combine_body, tc=tc, t_total=t, alpha=alpha),
        grid_spec=pltpu.PrefetchScalarGridSpec(
            num_scalar_prefetch=1,
            grid=(t // tc,),
            in_specs=[pl.BlockSpec(memory_space=pl.ANY),
                      pl.BlockSpec((tc, d), tile),
                      pl.BlockSpec((tc, 2), tile),
                      pl.BlockSpec((1, d), const),
                      pl.BlockSpec((1, d), const)],
            out_specs=[pl.BlockSpec((tc, d), tile), pl.BlockSpec((tc, d), tile)],
            scratch_shapes=[pltpu.VMEM((2, tc, d), F32), pltpu.VMEM((2, tc, d), F32),
                            pltpu.SemaphoreType.DMA((2,))]),
        out_shape=[jax.ShapeDtypeStruct((t, d), F32), jax.ShapeDtypeStruct((t, d), BF16)],
        compiler_params=_params("arbitrary"),
        name="moe_combine_ln",
    )(pos, ys, x, c, g.reshape(1, d), b.reshape(1, d))


def _moe_ln(x, w_r1, b_r1, w_r2, b_r2, w_gate, w_up, w_down, layer, g, b, alpha, tiles):
    t, d = x.shape
    tm_e = min(tiles["expert_rows"], t)
    oi, of, cnt = _router(x, w_r1, b_r1, w_r2, b_r2, tiles["router"])
    counts = cnt[:, 0].astype(jnp.int32)
    seg_tiles = (counts + tm_e - 1) // tm_e
    tile_end = jnp.cumsum(seg_tiles)
    seg_start = (tile_end - seg_tiles) * tm_e
    start_of = lambda e: jnp.sum(jnp.where(e[:, None] == jnp.arange(N_EXPERTS, dtype=jnp.int32)[None, :],
                                           seg_start[None, :], 0), axis=1)
    pos = jnp.concatenate([start_of(oi[0]) + oi[2], start_of(oi[1]) + oi[3]])
    n_tiles = 2 * t // tm_e + N_EXPERTS
    n_valid = tile_end[-1:]
    tile_id = jnp.arange(n_tiles, dtype=jnp.int32)
    blk = jnp.minimum(tile_id, n_valid - 1)
    tile_expert = jnp.minimum(jnp.sum((blk[:, None] >= tile_end[None, :]).astype(jnp.int32), axis=1), N_EXPERTS - 1)
    zero_tiles = (jnp.any(tile_id[:, None] == (tile_end - 1)[None, :], axis=1) | (tile_id >= n_valid)).astype(jnp.int32)

    xs = _dispatch(x, pos, zero_tiles, n_tiles * tm_e, tiles["dispatch"], tm_e)
    ys = _experts(xs, w_gate, w_up, w_down, layer * N_EXPERTS, blk, tile_expert, n_valid, tm_e,
                  min(tiles["expert_ff"], w_gate.shape[2]))
    return _combine_ln(ys, pos, x, of[:2].T, g, b, alpha, tiles["combine"])


def _hybrid_mixer_ln(xf, xb, bn, j, w_in_all, conv_w, conv_b, w_r, b_r, w_i, b_i, lam, b_f, w_out_all, g, b, alpha, tiles):
    t, d = xf.shape
    s = t // bn
    heads, block, _ = w_r.shape
    lru_w = heads * block
    fox_heads = b_f.shape[0]
    main = w_in_all.shape[1] - fox_heads
    fox_w = (main - 2 * lru_w) // 3
    dh = fox_w // fox_heads
    tm, tn = tiles["mm_m"], tiles["mm_n"]
    xc = _proj_w32(xb, w_in_all, j, 0, lru_w, BF16, tm, tn, "hy_lru_x_proj_conv", "conv", (conv_w, conv_b), s)
    gg = _proj_w32(xb, w_in_all, j, lru_w, lru_w, BF16, tm, tn, "hy_lru_gate_proj_gelu", "gelu")
    qkv = _proj_w32(xb, w_in_all, j, 2 * lru_w, 3 * fox_w, BF16, tm, tn, "hy_qkv_proj", "scale",
                     scale_cols=fox_w // min(tn, fox_w), scale=dh ** -0.5 * LOG2E)
    fl = _proj_small(xb, w_in_all[j, main:, :], tiles["small_m"], "hy_forget_proj")[:, :fox_heads]
    f2 = _fox_prep(fl.reshape(bn, s, fox_heads).transpose(0, 2, 1), b_f)
    y_lru = _lru(xc.reshape(bn, s, lru_w), gg.reshape(bn, s, lru_w), w_r, b_r, w_i, b_i, lam, tiles["lru"])
    y_att = _fox_attention(qkv.reshape(bn, s, 3 * fox_w), f2, fox_heads, dh, tiles["attn"], tiles["attn_heads"])
    return _matmul_residual_ln([y_lru.reshape(t, lru_w), y_att.reshape(t, fox_w)], w_out_all, j, xf, g, b, alpha,
                               tiles["ln_m"], tiles["ln_k"], "hy_out_proj_ln")


def _ssd_mixer_ln(xf, xb, bn, j, w_in_all, conv_w, conv_b, dt_bias, a_log, d_skip, norm_g, w_out_all, g, b, alpha, tiles):
    t, d = xf.shape
    s = t // bn
    heads = a_log.shape[0]
    inner = norm_g.shape[0]
    conv_dim = conv_w.shape[1]
    groups = (conv_dim - inner) // (2 * SSD_STATE)
    tm, tn = tiles["mm_m"], tiles["mm_n"]
    zs = _proj_w32(xb, w_in_all, j, 0, inner, BF16, tm, tn, "ssd_z_proj_silu", "silu")
    xbc = _proj_w32(xb, w_in_all, j, inner, conv_dim, BF16, tm, tn, "ssd_xbc_proj_conv_silu", "conv_silu",
                     (conv_w, conv_b), s)
    dt = _proj_small(xb, w_in_all[j, inner + conv_dim:, :], tiles["small_m"], "ssd_dt_proj")
    y = _ssd_core(xbc.reshape(bn, s, conv_dim), zs.reshape(bn, s, inner), dt.reshape(bn, s, _LANES),
                  dt_bias, a_log, d_skip, norm_g, groups)
    return _matmul_residual_ln([y.reshape(t, inner)], w_out_all, j, xf, g, b, alpha,
                               tiles["ln_m"], tiles["ln_k"], "ssd_out_proj_ln")


_TILES = dict(mm_m=1024, mm_n=1024, small_m=1024, ln_m=512, ln_k=4096, lru=256, attn=512, attn_heads=1,
              router=512, dispatch=1024, expert_rows=512, expert_ff=512, combine=256)


def kernel(x, ln_g, ln_b, hy_w_in, hy_conv_w, hy_conv_b, hy_w_r, hy_b_r, hy_w_i, hy_b_i, hy_lambda, hy_b_f, hy_w_out, ssd_w_in, ssd_conv_w, ssd_conv_b, ssd_dt_bias, ssd_a_log, ssd_d, ssd_norm_g, ssd_w_out, moe_w_r1, moe_b_r1, moe_w_r2, moe_b_r2, moe_w_gate, moe_w_up, moe_w_down):
    bn, s, d = x.shape
    depth = ln_g.shape[0]
    alpha = (2 * depth) ** 0.25
    ff = moe_w_gate.shape[-1]
    hy_w_in_t = jnp.swapaxes(hy_w_in, 1, 2)
    ssd_w_in_t = jnp.swapaxes(ssd_w_in, 1, 2)
    hy_w_out_b = hy_w_out.astype(BF16)
    ssd_w_out_b = ssd_w_out.astype(BF16)
    w_gate_s = moe_w_gate.reshape(depth * N_EXPERTS, d, ff)
    w_up_s = moe_w_up.reshape(depth * N_EXPERTS, d, ff)
    w_down_s = moe_w_down.reshape(depth * N_EXPERTS, ff, d)
    xf = x.reshape(bn * s, d)
    xb = xf.astype(BF16)
    for layer in range(depth):
        j = layer // 2
        if layer % 2 == 0:
            xf, xb = _hybrid_mixer_ln(xf, xb, bn, j, hy_w_in_t, hy_conv_w[j], hy_conv_b[j], hy_w_r[j], hy_b_r[j],
                                      hy_w_i[j], hy_b_i[j], hy_lambda[j], hy_b_f[j], hy_w_out_b,
                                      ln_g[layer, 0], ln_b[layer, 0], alpha, _TILES)
        else:
            xf, xb = _ssd_mixer_ln(xf, xb, bn, j, ssd_w_in_t, ssd_conv_w[j], ssd_conv_b[j], ssd_dt_bias[j],
                                   ssd_a_log[j], ssd_d[j], ssd_norm_g[j], ssd_w_out_b,
                                   ln_g[layer, 0], ln_b[layer, 0], alpha, _TILES)
        xf, xb = _moe_ln(xf, moe_w_r1[layer], moe_b_r1[layer], moe_w_r2[layer], moe_b_r2[layer],
                         w_gate_s, w_up_s, w_down_s, layer, ln_g[layer, 1], ln_b[layer, 1], alpha, _TILES)
    return xf.reshape(bn, s, d)
```

```python
import functools
import math

import jax
import jax.numpy as jnp
from jax import lax
from jax.experimental import pallas as pl
from jax.experimental.pallas import tpu as pltpu

F32 = jnp.float32
BF16 = jnp.bfloat16

_VMEM_LIMIT_BYTES = 56 * 1024 * 1024
_LANES = 128
_SUBLANES = 8

LN_EPS = 1e-5
RMS_EPS = 1e-5
LRU_C = 8.0
CONV_K = 4
SSD_CHUNK = 128
SSD_HEAD_DIM = 64
SSD_STATE = 128
MOE_GROUPS = 4
MOE_PER_GROUP = 4
N_EXPERTS = MOE_GROUPS * MOE_PER_GROUP
ROUTER_COLS = 128
ROUTER_L2_ROW0 = 8
LOG2E = math.log2(math.e)


def _params(*sem):
    return pltpu.CompilerParams(dimension_semantics=sem, vmem_limit_bytes=_VMEM_LIMIT_BYTES)


def _sigmoid(x):
    return 1.0 / (1.0 + jnp.exp(-x))


def _softplus(x):
    e = jnp.exp(-jnp.abs(x))
    u = 1.0 + e
    return jnp.maximum(x, 0.0) + jnp.where(u == 1.0, e, jnp.log(u) * (e / (u - 1.0)))


def _shift_rows(x, tail, d):
    r = pltpu.roll(x, d, axis=0)
    t = pltpu.roll(tail, d, axis=0)
    row = lax.broadcasted_iota(jnp.int32, tail.shape, 0)
    first = jnp.where(row < d, t, r[:_SUBLANES])
    return jnp.concatenate([first, r[_SUBLANES:]], axis=0)


def _causal_conv(x, tail, w_ref, b_ref):
    acc = b_ref[...] + w_ref[CONV_K - 1:CONV_K, :] * x
    for d in range(1, CONV_K):
        acc = acc + w_ref[CONV_K - 1 - d:CONV_K - d, :] * _shift_rows(x, tail, d)
    return acc


def _cumsum_rows(x):
    n = x.shape[0]
    row = lax.broadcasted_iota(jnp.int32, x.shape, 0)
    d = 1
    while d < n:
        x = x + jnp.where(row >= d, pltpu.roll(x, d, axis=0), 0.0)
        d *= 2
    return x


def _cumsum_lanes(x):
    n = x.shape[1]
    lane = lax.broadcasted_iota(jnp.int32, x.shape, 1)
    d = 1
    while d < n:
        x = x + jnp.where(lane >= d, pltpu.roll(x, d, axis=1), 0.0)
        d *= 2
    return x


def _layer_norm(s, g, b):
    mu = jnp.mean(s, axis=-1, keepdims=True)
    d = s - mu
    var = jnp.mean(d * d, axis=-1, keepdims=True)
    return d * lax.rsqrt(var + LN_EPS) * g + b


def _proj_small_body(x_ref, wt_ref, o_ref):
    o_ref[...] = lax.dot_general(x_ref[...], wt_ref[...].astype(BF16), (((1,), (1,)), ((), ())),
                                 preferred_element_type=F32)


def _proj_small(x, w_t, tm, name):
    m, k = x.shape
    tm = min(tm, m)
    w_t = jnp.pad(w_t, ((0, _LANES - w_t.shape[0]), (0, 0)))
    return pl.pallas_call(
        _proj_small_body,
        grid=(m // tm,),
        in_specs=[pl.BlockSpec((tm, k), lambda i: (i, 0)),
                  pl.BlockSpec((_LANES, k), lambda i: (0, 0))],
        out_specs=pl.BlockSpec((tm, _LANES), lambda i: (i, 0)),
        out_shape=jax.ShapeDtypeStruct((m, _LANES), F32),
        compiler_params=_params("parallel"),
        name=name,
    )(x, w_t)


def _proj_w32_body(x_ref, w_ref, *rest, epilogue, tiles_per_seq, chunk, scale_cols, scale):
    conv = epilogue.startswith("conv")
    if conv:
        cw_ref, cb_ref, o_ref, wb_ref, carry_ref = rest
    else:
        o_ref, wb_ref = rest
    j = pl.program_id(0)
    t = pl.program_id(1)
    tm = x_ref.shape[0]

    @pl.when(t == 0)
    def _():
        wb_ref[...] = w_ref[...].T.astype(BF16)

    if conv:
        @pl.when(t % tiles_per_seq == 0)
        def _():
            carry_ref[...] = jnp.zeros_like(carry_ref)

    x = x_ref[...]
    for c0 in range(0, o_ref.shape[1], chunk):
        cols = slice(c0, c0 + chunk)
        acc = jnp.dot(x, wb_ref[:, cols], preferred_element_type=F32)
        if conv:
            tail = carry_ref[:, cols]
            carry_ref[:, cols] = acc[tm - _SUBLANES:, :]
            acc = _causal_conv(acc, tail, cw_ref.at[:, cols], cb_ref.at[:, cols])
        if epilogue.endswith("silu"):
            acc = acc * _sigmoid(acc)
        elif epilogue == "gelu":
            acc = jax.nn.gelu(acc)
        elif epilogue == "scale":
            acc = acc * jnp.where(j < scale_cols, scale, 1.0)
        o_ref[:, cols] = acc.astype(o_ref.dtype)


def _proj_w32(x, w_all, layer, col0, n, out_dtype, tm, tn, name, epilogue="none", conv=None, seq_len=None,
              scale_cols=0, scale=1.0, chunk=256):
    m, k = x.shape
    tm, tn = min(tm, m), min(tn, n)
    assert col0 % tn == 0 and n % tn == 0 and m % tm == 0 and tn % chunk == 0
    jb = col0 // tn
    in_specs = [pl.BlockSpec((tm, k), lambda j, t: (t, 0)),
                pl.BlockSpec((None, tn, k), lambda j, t: (layer, jb + j, 0))]
    args = [x, w_all]
    scratch = [pltpu.VMEM((k, tn), BF16)]
    tiles_per_seq = 1
    if conv is not None:
        assert seq_len % tm == 0
        tiles_per_seq = seq_len // tm
        in_specs += [pl.BlockSpec((CONV_K, tn), lambda j, t: (0, j)),
                     pl.BlockSpec((1, tn), lambda j, t: (0, j))]
        args += [conv[0], conv[1].reshape(1, n)]
        scratch.append(pltpu.VMEM((_SUBLANES, tn), F32))
    return pl.pallas_call(
        functools.partial(_proj_w32_body, epilogue=epilogue, tiles_per_seq=tiles_per_seq, chunk=chunk,
                          scale_cols=scale_cols, scale=scale),
        grid=(n // tn, m // tm),
        in_specs=in_specs,
        out_specs=pl.BlockSpec((tm, tn), lambda j, t: (t, j)),
        out_shape=jax.ShapeDtypeStruct((m, n), out_dtype),
        scratch_shapes=scratch,
        compiler_params=_params("arbitrary", "arbitrary"),
        name=name,
    )(*args)


def _mm_ln_body(*refs, alpha, n_a, nk):
    a_refs = refs[:n_a]
    w_ref, x_ref, g_ref, b_ref, of_ref, ob_ref = refs[n_a:n_a + 6]
    ka = a_refs[0].shape[1]
    part = jnp.dot(a_refs[0][...], w_ref[:ka, :], preferred_element_type=F32)
    for idx in range(1, n_a):
        part = part + jnp.dot(a_refs[idx][...], w_ref[idx * ka:(idx + 1) * ka, :], preferred_element_type=F32)

    def finish(acc):
        y = _layer_norm(alpha * x_ref[...] + acc, g_ref[...], b_ref[...])
        of_ref[...] = y
        ob_ref[...] = y.astype(BF16)

    if nk == 1:
        finish(part)
        return
    acc_ref = refs[n_a + 6]
    k = pl.program_id(1)

    @pl.when(k == 0)
    def _():
        acc_ref[...] = part

    @pl.when(k > 0)
    def _():
        acc_ref[...] += part

    @pl.when(k == nk - 1)
    def _():
        finish(acc_ref[...])


def _matmul_residual_ln(a_list, w_all, layer, x, g, b, alpha, tm, tk, name):
    n_a = len(a_list)
    m, ka = a_list[0].shape
    n = w_all.shape[2]
    tm = min(tm, m)
    tka = ka if n_a > 1 else min(tk, ka)
    nk = ka // tka
    w_mode = dict(pipeline_mode=pl.Buffered(1)) if nk == 1 else {}
    return pl.pallas_call(
        functools.partial(_mm_ln_body, alpha=alpha, n_a=n_a, nk=nk),
        grid=(m // tm, nk),
        in_specs=[pl.BlockSpec((tm, tka), lambda i, k: (i, k))] * n_a + [
                  pl.BlockSpec((None, n_a * tka, n), lambda i, k: (layer, k, 0), **w_mode),
                  pl.BlockSpec((tm, n), lambda i, k: (i, 0)),
                  pl.BlockSpec((1, n), lambda i, k: (0, 0)),
                  pl.BlockSpec((1, n), lambda i, k: (0, 0))],
        out_specs=[pl.BlockSpec((tm, n), lambda i, k: (i, 0)),
                   pl.BlockSpec((tm, n), lambda i, k: (i, 0))],
        out_shape=[jax.ShapeDtypeStruct((m, n), F32), jax.ShapeDtypeStruct((m, n), BF16)],
        scratch_shapes=[pltpu.VMEM((tm, n), F32)] if nk > 1 else [],
        compiler_params=_params("parallel", "arbitrary"),
        name=name,
    )(*a_list, w_all, x, g.reshape(1, n), b.reshape(1, n))


def _fox_prep_body(fl_ref, bf_ref, f_ref):
    x = fl_ref[0] + bf_ref[...]
    f_ref[0] = _cumsum_lanes(-_softplus(-x)) * LOG2E


def _fox_prep(fl_t, b_f):
    bn, h, s = fl_t.shape
    return pl.pallas_call(
        _fox_prep_body,
        grid=(bn,),
        in_specs=[pl.BlockSpec((1, h, s), lambda b: (b, 0, 0)),
                  pl.BlockSpec((h, 1), lambda b: (0, 0))],
        out_specs=pl.BlockSpec((1, h, s), lambda b: (b, 0, 0)),
        out_shape=jax.ShapeDtypeStruct((bn, h, s), F32),
        compiler_params=_params("parallel"),
        name="fox_prep",
    )(fl_t, b_f.reshape(h, 1))


def _lru_body(xc_ref, gg_ref, wr_ref, br_ref, wi_ref, bi_ref, lam_ref, y_ref, h_ref, acum_ref, hloc_ref,
              *, heads, block):
    @pl.when(pl.program_id(1) == 0)
    def _():
        h_ref[...] = jnp.zeros_like(h_ref)

    xcb = xc_ref[0]
    xc = xcb.astype(F32)
    ts = xc.shape[0]
    r_parts, i_parts = [], []
    for hd in range(heads):
        xh = xcb[:, hd * block:(hd + 1) * block]
        r_parts.append(jnp.dot(xh, wr_ref[hd], preferred_element_type=F32))
        i_parts.append(jnp.dot(xh, wi_ref[hd], preferred_element_type=F32))
    r = _sigmoid(jnp.concatenate(r_parts, axis=1) + br_ref[...])
    ig = _sigmoid(jnp.concatenate(i_parts, axis=1) + bi_ref[...])
    log_a = -LRU_C * r * _softplus(-lam_ref[...])
    a = jnp.exp(log_a)
    u = jnp.sqrt(-jnp.tanh(log_a) * (a * a + 1.0)) * (ig * xc)

    width = a.shape[1]
    a3 = a.reshape(ts // _SUBLANES, _SUBLANES, width)
    u3 = u.reshape(ts // _SUBLANES, _SUBLANES, width)
    row = lax.broadcasted_iota(jnp.int32, a3.shape, 1)
    d = 1
    while d < _SUBLANES:
        keep = row >= d
        u3 = u3 + a3 * jnp.where(keep, pltpu.roll(u3, d, axis=1), 0.0)
        a3 = a3 * jnp.where(keep, pltpu.roll(a3, d, axis=1), 1.0)
        d *= 2
    acum_ref[...] = a3
    hloc_ref[...] = u3

    def carry_group(g, h_prev):
        hg = hloc_ref[g] + acum_ref[g] * h_prev
        hloc_ref[g] = hg
        return hg[_SUBLANES - 1:_SUBLANES, :]

    h_ref[...] = lax.fori_loop(0, ts // _SUBLANES, carry_group, h_ref[...], unroll=4)
    h = hloc_ref[...].reshape(ts, width)
    y_ref[0] = (h * gg_ref[0].astype(F32)).astype(y_ref.dtype)


def _lru(xc, gg, w_r, b_r, w_i, b_i, lam, ts):
    bn, s, width = xc.shape
    heads, block, _ = w_r.shape
    ts = min(ts, s)
    vec = lambda v: v.reshape(1, width)
    const2 = lambda b, i: (0, 0)
    const3 = lambda b, i: (0, 0, 0)
    tile = pl.BlockSpec((1, ts, width), lambda b, i: (b, i, 0))
    return pl.pallas_call(
        functools.partial(_lru_body, heads=heads, block=block),
        grid=(bn, s // ts),
        in_specs=[tile, tile,
                  pl.BlockSpec((heads, block, block), const3),
                  pl.BlockSpec((1, width), const2),
                  pl.BlockSpec((heads, block, block), const3),
                  pl.BlockSpec((1, width), const2),
                  pl.BlockSpec((1, width), const2)],
        out_specs=tile,
        out_shape=jax.ShapeDtypeStruct((bn, s, width), BF16),
        scratch_shapes=[pltpu.VMEM((1, width), F32),
                        pltpu.VMEM((ts // _SUBLANES, _SUBLANES, width), F32),
                        pltpu.VMEM((ts // _SUBLANES, _SUBLANES, width), F32)],
        compiler_params=_params("parallel", "arbitrary"),
        name="rg_lru",
    )(xc, gg, w_r.astype(BF16), vec(b_r), w_i.astype(BF16), vec(b_i), vec(lam))


def _split3(f):
    hi = f.astype(BF16).astype(F32)
    r = f - hi
    mid = r.astype(BF16).astype(F32)
    return hi, mid, r - mid


def _fox_attn_body(q_ref, k_ref, v_ref, fc_ref, o_ref, kx_ref, *, tq, hp, dh):
    qi = pl.program_id(2)
    s_len = k_ref.shape[1]
    heads = range(hp)
    lane = lax.broadcasted_iota(jnp.int32, (tq, _LANES), 1)

    @pl.when(qi == 0)
    def _():
        for e in heads:
            for c0 in range(0, s_len, tq):
                hi, mid, lo = _split3(fc_ref[0, e, c0:c0 + tq, :])
                kx = jnp.where(lane < 3, 1.0, jnp.where(lane == 3, -hi, jnp.where(lane == 4, -mid,
                               jnp.where(lane == 5, -lo, 0.0))))
                kx_ref[e, c0:c0 + tq, :] = kx.astype(BF16)

    qs = []
    for e in heads:
        hi, mid, lo = _split3(fc_ref[0, e, pl.ds(pl.multiple_of(qi * tq, tq), tq), :])
        qx = jnp.where(lane == 0, hi, jnp.where(lane == 1, mid, jnp.where(lane == 2, lo,
                       jnp.where(lane < 6, 1.0, 0.0))))
        qs.append(jnp.concatenate([q_ref[0, :, e * dh:(e + 1) * dh], qx.astype(BF16)], axis=1))

    def scores(e, j):
        start = pl.multiple_of(j * tq, tq)
        k = jnp.concatenate([k_ref[0, pl.ds(start, tq), e * dh:(e + 1) * dh], kx_ref[e, pl.ds(start, tq), :]], axis=1)
        return lax.dot_general(qs[e], k, (((1,), (1,)), ((), ())), preferred_element_type=F32)

    def update(e, carry, sc, j):
        m, l, acc = carry
        m_new = jnp.maximum(m, jnp.max(sc, axis=1, keepdims=True))
        corr = jnp.exp2(m - m_new)
        p = jnp.exp2(sc - m_new)
        l = corr * l + jnp.sum(p, axis=1, keepdims=True)
        v = v_ref[0, pl.ds(pl.multiple_of(j * tq, tq), tq), e * dh:(e + 1) * dh]
        acc = corr * acc + jnp.dot(p.astype(BF16), v, preferred_element_type=F32)
        return m_new, l, acc

    def full_blocks(carries, js):
        scs = [[scores(e, j) for j in js] for e in heads]
        out = []
        for e in heads:
            c = carries[e]
            for sc, j in zip(scs[e], js):
                c = update(e, c, sc, j)
            out.append(c)
        return tuple(out)

    def diagonal(carries):
        scs = [scores(e, qi) for e in heads]
        rq = lax.broadcasted_iota(jnp.int32, (tq, tq), 0)
        ck = lax.broadcasted_iota(jnp.int32, (tq, tq), 1)
        return tuple(update(e, carries[e], jnp.where(ck <= rq, scs[e], -1e30), qi) for e in heads)

    def two_blocks(jj, carries):
        return full_blocks(carries, (2 * jj, 2 * jj + 1))

    def odd_tail(carries):
        return diagonal(full_blocks(carries, (qi - 1,)))

    init = tuple((jnp.full((tq, 1), -1e30, F32), jnp.zeros((tq, 1), F32), jnp.zeros((tq, dh), F32))
                 for _ in heads)
    carries = lax.fori_loop(0, qi // 2, two_blocks, init)
    carries = lax.cond(qi % 2 == 1, odd_tail, diagonal, carries)
    o_ref[0] = jnp.concatenate([acc / l for _, l, acc in carries], axis=1).astype(o_ref.dtype)


def _fox_attention(qkv, f2, heads, dh, tq, hp):
    bn, s, _ = qkv.shape
    tq = min(tq, s)
    assert heads % hp == 0
    hb = heads // hp
    return pl.pallas_call(
        functools.partial(_fox_attn_body, tq=tq, hp=hp, dh=dh),
        grid=(bn, hb, s // tq),
        in_specs=[pl.BlockSpec((1, tq, hp * dh), lambda b, h, i: (b, i, h)),
                  pl.BlockSpec((1, s, hp * dh), lambda b, h, i: (b, 0, hb + h)),
                  pl.BlockSpec((1, s, hp * dh), lambda b, h, i: (b, 0, 2 * hb + h)),
                  pl.BlockSpec((1, hp, s, 1), lambda b, h, i: (b, h, 0, 0))],
        out_specs=pl.BlockSpec((1, tq, hp * dh), lambda b, h, i: (b, i, h)),
        out_shape=jax.ShapeDtypeStruct((bn, s, heads * dh), BF16),
        scratch_shapes=[pltpu.VMEM((hp, s, _LANES), BF16)],
        compiler_params=_params("parallel", "parallel", "arbitrary"),
        name="fox_attention",
    )(qkv, qkv, qkv, f2[..., None])


def _ssd_body(xbc_ref, z_ref, dt_ref, dtb_ref, alog_ref, dexp_ref, ng_ref, o_ref,
              state_ref, acst_ref, dtt_ref, wt_ref, *, groups, inner):
    L = SSD_CHUNK
    gw = inner // groups
    pairs = gw // _LANES
    hpg = gw // SSD_HEAD_DIM

    @pl.when(pl.program_id(1) == 0)
    def _():
        state_ref[...] = jnp.zeros_like(state_ref)

    dt = _softplus(dt_ref[0] + dtb_ref[...])
    acs = _cumsum_rows(dt * (-jnp.exp(alog_ref[...]))) * LOG2E
    acs_t = acs.T
    dt_t = dt.T
    acst_ref[...] = acs_t
    dtt_ref[...] = dt_t
    wt_ref[...] = dt_t * jnp.exp2(acs_t[:, L - 1:L] - acs_t)
    dec_last = jnp.exp2(acs[L - 1:L, :])

    rq = lax.broadcasted_iota(jnp.int32, (L, L), 0)
    ck = lax.broadcasted_iota(jnp.int32, (L, L), 1)
    causal = ck <= rq
    lo = lax.broadcasted_iota(jnp.int32, (L, _LANES), 1) < SSD_HEAD_DIM
    lo_row = lo[:1]

    for g in range(groups):
        bmat = xbc_ref[0, :, inner + g * SSD_STATE:inner + (g + 1) * SSD_STATE]
        cmat = xbc_ref[0, :, inner + (groups + g) * SSD_STATE:inner + (groups + g + 1) * SSD_STATE]
        cbm = lax.dot_general(cmat, bmat, (((1,), (1,)), ((), ())), preferred_element_type=F32)
        b_t = bmat.astype(F32).T
        ys = []
        for j in range(pairs):
            col0 = g * gw + j * _LANES
            xs_p = xbc_ref[0, :, col0:col0 + _LANES].astype(F32)
            m_parts, bw_parts, cols, dl = [], [], [], []
            for e in range(2):
                hd = g * hpg + 2 * j + e
                col = jnp.broadcast_to(acs[:, hd:hd + 1], (L, L))
                row = acst_ref[hd:hd + 1, :]
                dec = jnp.exp2(jnp.where(causal, col - row, -jnp.inf))
                m_parts.append((cbm * dec * dtt_ref[hd:hd + 1, :]).astype(BF16))
                bw_parts.append((b_t * wt_ref[hd:hd + 1, :]).astype(BF16))
                cols.append(col)
                dl.append(dec_last[:, hd:hd + 1])
            prev = state_ref[:, col0:col0 + _LANES]
            x2 = jnp.concatenate([jnp.where(lo, xs_p, 0.0).astype(BF16),
                                  jnp.where(lo, 0.0, xs_p).astype(BF16)], axis=0)
            y_diag = jnp.dot(jnp.concatenate(m_parts, axis=1), x2, preferred_element_type=F32)
            y_off = jnp.dot(cmat, prev.astype(BF16), preferred_element_type=F32)
            y = y_diag + jnp.exp2(jnp.where(lo, cols[0], cols[1])) * y_off
            st = jnp.dot(jnp.concatenate(bw_parts, axis=1), x2, preferred_element_type=F32)
            state_ref[:, col0:col0 + _LANES] = prev * jnp.where(lo_row, dl[0], dl[1]) + st
            y = y + dexp_ref[:, col0:col0 + _LANES] * xs_p
            ys.append(y * z_ref[0, :, col0:col0 + _LANES].astype(F32))
        yg = jnp.concatenate(ys, axis=1)
        ms = jnp.mean(yg * yg, axis=1, keepdims=True)
        gcols = slice(g * gw, (g + 1) * gw)
        o_ref[0, :, gcols] = (yg * lax.rsqrt(ms + RMS_EPS) * ng_ref[:, gcols]).astype(o_ref.dtype)


def _ssd_core(xbc, zs, dt, dt_bias, a_log, d_skip, norm_g, groups):
    bn, s, conv_dim = xbc.shape
    inner = zs.shape[2]
    heads = a_log.shape[0]
    L = SSD_CHUNK
    pad = lambda v: jnp.pad(v, (0, _LANES - heads)).reshape(1, _LANES)
    const2 = lambda b, c: (0, 0)
    return pl.pallas_call(
        functools.partial(_ssd_body, groups=groups, inner=inner),
        grid=(bn, s // L),
        in_specs=[pl.BlockSpec((1, L, conv_dim), lambda b, c: (b, c, 0)),
                  pl.BlockSpec((1, L, inner), lambda b, c: (b, c, 0)),
                  pl.BlockSpec((1, L, _LANES), lambda b, c: (b, c, 0)),
                  pl.BlockSpec((1, _LANES), const2),
                  pl.BlockSpec((1, _LANES), const2),
                  pl.BlockSpec((1, inner), const2),
                  pl.BlockSpec((1, inner), const2)],
        out_specs=pl.BlockSpec((1, L, inner), lambda b, c: (b, c, 0)),
        out_shape=jax.ShapeDtypeStruct((bn, s, inner), BF16),
        scratch_shapes=[pltpu.VMEM((SSD_STATE, inner), F32),
                        pltpu.VMEM((_LANES, L), F32),
                        pltpu.VMEM((_LANES, L), F32),
                        pltpu.VMEM((_LANES, L), F32)],
        compiler_params=_params("parallel", "arbitrary"),
        name="ssd_core",
    )(xbc, zs, dt, pad(dt_bias), pad(a_log),
      jnp.repeat(d_skip, SSD_HEAD_DIM).reshape(1, inner), norm_g.reshape(1, inner))


def _router_body(x_ref, whl_ref, wh_ref, b_ref, oi_ref, of_ref, cnt_ref, base_ref):
    i = pl.program_id(0)

    @pl.when(i == 0)
    def _():
        base_ref[...] = jnp.zeros_like(base_ref)

    x = x_ref[...]
    tm = x.shape[0]
    xh = x.astype(BF16)
    xl = (x - xh.astype(F32)).astype(BF16)
    r = jnp.dot(xh, whl_ref[...], preferred_element_type=F32)
    logits = (r[:, :ROUTER_COLS] + r[:, ROUTER_COLS:]
              + jnp.dot(xl, wh_ref[...], preferred_element_type=F32) + b_ref[...])
    lt = logits.T

    row = lax.broadcasted_iota(jnp.int32, (_SUBLANES, tm), 0)
    neg = -jnp.inf
    l1 = jnp.where(row < MOE_GROUPS, lt[:_SUBLANES], neg)
    e1 = jnp.exp(l1 - jnp.max(l1, axis=0, keepdims=True))
    p1 = e1 / jnp.sum(e1, axis=0, keepdims=True)
    pg = jnp.max(p1, axis=0, keepdims=True)
    gidx = jnp.min(jnp.where(p1 == pg, row, _SUBLANES), axis=0, keepdims=True)

    sel = jnp.full((_SUBLANES, tm), neg, F32)
    for gi in range(MOE_GROUPS):
        r0 = ROUTER_L2_ROW0 + _SUBLANES * gi
        sel = jnp.where((gidx == gi) & (row < MOE_PER_GROUP), lt[r0:r0 + _SUBLANES], sel)
    va = jnp.max(sel, axis=0, keepdims=True)
    ia = jnp.min(jnp.where(sel == va, row, _SUBLANES), axis=0, keepdims=True)
    sel_b = jnp.where(row == ia, neg, sel)
    vb = jnp.max(sel_b, axis=0, keepdims=True)
    ib = jnp.min(jnp.where(sel_b == vb, row, _SUBLANES), axis=0, keepdims=True)
    t = jnp.exp(vb - va)
    ca = pg * (1.0 / (1.0 + t))
    cb = pg * (t / (1.0 + t))
    ea = gidx * MOE_PER_GROUP + ia
    eb = gidx * MOE_PER_GROUP + ib

    erow = lax.broadcasted_iota(jnp.int32, (N_EXPERTS, tm), 0)
    hit_a = erow == ea
    hit_b = erow == eb
    oh = jnp.where(hit_a, 1.0, 0.0) + jnp.where(hit_b, 1.0, 0.0)
    before = (lax.broadcasted_iota(jnp.int32, (tm, tm), 0)
              < lax.broadcasted_iota(jnp.int32, (tm, tm), 1))
    prefix = jnp.dot(oh.astype(BF16), jnp.where(before, 1.0, 0.0).astype(BF16),
                     preferred_element_type=F32) + base_ref[...]
    rank_a = jnp.sum(jnp.where(hit_a, prefix, 0.0), axis=0, keepdims=True)
    rank_b = jnp.sum(jnp.where(hit_b, prefix, 0.0), axis=0, keepdims=True)
    base_ref[...] += jnp.sum(oh, axis=1, keepdims=True)
    cnt_ref[...] = jnp.broadcast_to(base_ref[...], cnt_ref.shape)

    zi = jnp.zeros((1, tm), jnp.int32)
    oi_ref[...] = jnp.concatenate(
        [ea, eb, rank_a.astype(jnp.int32), rank_b.astype(jnp.int32), zi, zi, zi, zi], axis=0)
    zf = jnp.zeros((1, tm), F32)
    of_ref[...] = jnp.concatenate([ca, cb, zf, zf, zf, zf, zf, zf], axis=0)


def _router(x, w_r1, b_r1, w_r2, b_r2, tm):
    t, d = x.shape
    tm = min(tm, t)
    def slots(first, per_group):
        pad8 = lambda v: jnp.pad(v, [(0, 0)] * (v.ndim - 1) + [(0, _SUBLANES - v.shape[-1])])
        parts = [pad8(first)] + [pad8(per_group[gi]) for gi in range(MOE_GROUPS)]
        out = jnp.concatenate(parts, axis=-1)
        return jnp.pad(out, [(0, 0)] * (out.ndim - 1) + [(0, ROUTER_COLS - out.shape[-1])])

    assert ROUTER_L2_ROW0 == _SUBLANES and MOE_GROUPS <= _SUBLANES and MOE_PER_GROUP <= _SUBLANES
    w = slots(w_r1, w_r2)
    b = slots(b_r1, b_r2)
    wh = w.astype(BF16)
    wl = (w - wh.astype(F32)).astype(BF16)
    const = lambda i: (0, 0)
    return pl.pallas_call(
        _router_body,
        grid=(t // tm,),
        in_specs=[pl.BlockSpec((tm, d), lambda i: (i, 0)),
                  pl.BlockSpec((d, 2 * ROUTER_COLS), const),
                  pl.BlockSpec((d, ROUTER_COLS), const),
                  pl.BlockSpec((1, ROUTER_COLS), const)],
        out_specs=[pl.BlockSpec((_SUBLANES, tm), lambda i: (0, i)),
                   pl.BlockSpec((_SUBLANES, tm), lambda i: (0, i)),
                   pl.BlockSpec((N_EXPERTS, _LANES), const)],
        out_shape=[jax.ShapeDtypeStruct((_SUBLANES, t), jnp.int32),
                   jax.ShapeDtypeStruct((_SUBLANES, t), F32),
                   jax.ShapeDtypeStruct((N_EXPERTS, _LANES), F32)],
        scratch_shapes=[pltpu.VMEM((N_EXPERTS, 1), F32)],
        compiler_params=_params("arbitrary"),
        name="moe_router",
    )(x, jnp.concatenate([wh, wl], axis=1), wh, b.reshape(1, ROUTER_COLS))


def _row_copy(src, src_row, dst, dst_row, sem):
    return pltpu.make_async_copy(src.at[pl.ds(src_row, 1)], dst.at[pl.ds(dst_row, 1)], sem)


def _dispatch_body(pos_ref, zt_ref, x_ref, xs_ref, zeros_ref, sem, zsem, *, td, tz, t_total):
    step = pl.program_id(0)
    base = step * td

    def zero_tile(i):
        return pltpu.make_async_copy(zeros_ref, xs_ref.at[pl.ds(pl.multiple_of(i * tz, tz), tz)], zsem)

    @pl.when(step == 0)
    def _():
        zeros_ref[...] = jnp.zeros_like(zeros_ref)

        def start(i, carry):
            @pl.when(zt_ref[i] == 1)
            def _():
                zero_tile(i).start()
            return carry

        def wait(i, carry):
            @pl.when(zt_ref[i] == 1)
            def _():
                zero_tile(i).wait()
            return carry

        lax.fori_loop(0, zt_ref.shape[0], start, 0)
        lax.fori_loop(0, zt_ref.shape[0], wait, 0)

    def issue(r, carry):
        _row_copy(x_ref, r, xs_ref, pos_ref[base + r], sem).start()
        _row_copy(x_ref, r, xs_ref, pos_ref[t_total + base + r], sem).start(priority=1)
        return carry

    lax.fori_loop(0, td, issue, 0, unroll=4)
    tile_rows = xs_ref.at[pl.ds(0, td)]
    pltpu.make_async_copy(x_ref, tile_rows, sem).wait()
    pltpu.make_async_copy(x_ref, tile_rows, sem).wait()


def _dispatch(x, pos, zero_tiles, n_rows, td, tz):
    t, d = x.shape
    td = min(td, t)
    return pl.pallas_call(
        functools.partial(_dispatch_body, td=td, tz=tz, t_total=t),
        grid_spec=pltpu.PrefetchScalarGridSpec(
            num_scalar_prefetch=2,
            grid=(t // td,),
            in_specs=[pl.BlockSpec((td, d), lambda i, pos, zt: (i, 0))],
            out_specs=pl.BlockSpec(memory_space=pl.ANY),
            scratch_shapes=[pltpu.VMEM((tz, d), x.dtype),
                            pltpu.SemaphoreType.DMA(()), pltpu.SemaphoreType.DMA(())]),
        out_shape=jax.ShapeDtypeStruct((n_rows, d), x.dtype),
        compiler_params=_params("arbitrary"),
        name="moe_dispatch",
    )(pos, zero_tiles, x)


def _expert_body(blk_ref, te_ref, nv_ref, xs_ref, wg_ref, wu_ref, wd_ref, o_ref):
    del blk_ref, te_ref
    valid = pl.program_id(0) < nv_ref[0]

    @pl.when(jnp.logical_not(valid))
    def _():
        o_ref[...] = jnp.zeros_like(o_ref)

    @pl.when(valid)
    def _():
        x = xs_ref[...].astype(BF16)
        g = jnp.dot(x, wg_ref[0], preferred_element_type=F32)
        u = jnp.dot(x, wu_ref[0], preferred_element_type=F32)
        hid = (g * _sigmoid(g)) * u
        o_ref[...] = jnp.dot(hid.astype(BF16), wd_ref[0], preferred_element_type=F32)


def _experts(xs, w_gate, w_up, w_down, expert0, blk, tile_expert, n_valid, tm):
    p, d = xs.shape
    ff = w_gate.shape[2]
    n_tiles = p // tm
    rows = lambda i, blk, te, nv: (blk[i], 0)
    wsel = lambda i, blk, te, nv: (expert0 + te[i], 0, 0)
    return pl.pallas_call(
        _expert_body,
        grid_spec=pltpu.PrefetchScalarGridSpec(
            num_scalar_prefetch=3,
            grid=(n_tiles,),
            in_specs=[pl.BlockSpec((tm, d), rows),
                      pl.BlockSpec((1, d, ff), wsel),
                      pl.BlockSpec((1, d, ff), wsel),
                      pl.BlockSpec((1, ff, d), wsel)],
            out_specs=pl.BlockSpec((tm, d), lambda i, blk, te, nv: (i, 0))),
        out_shape=jax.ShapeDtypeStruct((p, d), F32),
        compiler_params=_params("arbitrary"),
        name="moe_experts",
    )(blk, tile_expert, n_valid, xs, w_gate, w_up, w_down)


def _combine_body(pos_ref, ys_ref, x_ref, c_ref, g_ref, b_ref, of_ref, ob_ref, buf_a, buf_b, sem,
                  *, tc, t_total, alpha):
    step = pl.program_id(0)
    slot = step % 2

    def issue(tile, which):
        base = tile * tc

        def body(r, carry):
            _row_copy(ys_ref, pos_ref[base + r], buf_a.at[which], r, sem.at[which]).start()
            _row_copy(ys_ref, pos_ref[t_total + base + r], buf_b.at[which], r, sem.at[which]).start(priority=1)
            return carry

        lax.fori_loop(0, tc, body, 0, unroll=4)

    @pl.when(step == 0)
    def _():
        issue(0, 0)

    tile_rows = ys_ref.at[pl.ds(0, tc)]
    pltpu.make_async_copy(tile_rows, buf_a.at[slot], sem.at[slot]).wait()
    pltpu.make_async_copy(tile_rows, buf_b.at[slot], sem.at[slot]).wait()

    last = pl.num_programs(0) - 1
    nbase = jnp.minimum(step + 1, last) * tc
    for r in range(tc):
        _row_copy(ys_ref, pos_ref[nbase + r], buf_a.at[1 - slot], r, sem.at[1 - slot]).start()
        _row_copy(ys_ref, pos_ref[t_total + nbase + r], buf_b.at[1 - slot], r, sem.at[1 - slot]).start(priority=1)
    c = c_ref[...]
    f = c[:, 0:1] * buf_a[slot] + c[:, 1:2] * buf_b[slot]
    y = _layer_norm(alpha * x_ref[...] + f, g_ref[...], b_ref[...])
    of_ref[...] = y
    ob_ref[...] = y.astype(BF16)

    @pl.when(step == last)
    def _():
        pltpu.make_async_copy(tile_rows, buf_a.at[1 - slot], sem.at[1 - slot]).wait()
        pltpu.make_async_copy(tile_rows, buf_b.at[1 - slot], sem.at[1 - slot]).wait()


def _combine_ln(ys, pos, x, c, g, b, alpha, tc):
    t, d = x.shape
    tc = min(tc, t)
    const = lambda i, pos: (0, 0)
    tile = lambda i, pos: (i, 0)
    return pl.pallas_call(
        functools.partial(_combine_body, tc=tc, t_total=t, alpha=alpha),
        grid_spec=pltpu.PrefetchScalarGridSpec(
            num_scalar_prefetch=1,
            grid=(t // tc,),
            in_specs=[pl.BlockSpec(memory_space=pl.ANY),
                      pl.BlockSpec((tc, d), tile),
                      pl.BlockSpec((tc, 2), tile),
                      pl.BlockSpec((1, d), const),
                      pl.BlockSpec((1, d), const)],
            out_specs=[pl.BlockSpec((tc, d), tile), pl.BlockSpec((tc, d), tile)],
            scratch_shapes=[pltpu.VMEM((2, tc, d), F32), pltpu.VMEM((2, tc, d), F32),
                            pltpu.SemaphoreType.DMA((2,))]),
        out_shape=[jax.ShapeDtypeStruct((t, d), F32), jax.ShapeDtypeStruct((t, d), BF16)],
        compiler_params=_params("arbitrary"),
        name="moe_combine_ln",
    )(pos, ys, x, c, g.reshape(1, d), b.reshape(1, d))


def _moe_ln(x, w_r1, b_r1, w_r2, b_r2, w_gate, w_up, w_down, layer, g, b, alpha, tiles):
    t, d = x.shape
    tm_e = min(tiles["expert_rows"], t)
    oi, of, cnt = _router(x, w_r1, b_r1, w_r2, b_r2, tiles["router"])
    counts = cnt[:, 0].astype(jnp.int32)
    seg_tiles = (counts + tm_e - 1) // tm_e
    tile_end = jnp.cumsum(seg_tiles)
    seg_start = (tile_end - seg_tiles) * tm_e
    start_of = lambda e: jnp.sum(jnp.where(e[:, None] == jnp.arange(N_EXPERTS, dtype=jnp.int32)[None, :],
                                           seg_start[None, :], 0), axis=1)
    pos = jnp.concatenate([start_of(oi[0]) + oi[2], start_of(oi[1]) + oi[3]])
    n_tiles = 2 * t // tm_e + N_EXPERTS
    n_valid = tile_end[-1:]
    tile_id = jnp.arange(n_tiles, dtype=jnp.int32)
    blk = jnp.minimum(tile_id, n_valid - 1)
    tile_expert = jnp.minimum(jnp.sum((blk[:, None] >= tile_end[None, :]).astype(jnp.int32), axis=1), N_EXPERTS - 1)
    zero_tiles = (jnp.any(tile_id[:, None] == (tile_end - 1)[None, :], axis=1) | (tile_id >= n_valid)).astype(jnp.int32)

    xs = _dispatch(x, pos, zero_tiles, n_tiles * tm_e, tiles["dispatch"], tm_e)
    ys = _experts(xs, w_gate, w_up, w_down, layer * N_EXPERTS, blk, tile_expert, n_valid, tm_e)
    return _combine_ln(ys, pos, x, of[:2].T, g, b, alpha, tiles["combine"])


def _hybrid_mixer_ln(xf, xb, bn, j, w_in_all, conv_w, conv_b, w_r, b_r, w_i, b_i, lam, b_f, w_out_all, g, b, alpha, tiles):
    t, d = xf.shape
    s = t // bn
    heads, block, _ = w_r.shape
    lru_w = heads * block
    fox_heads = b_f.shape[0]
    main = w_in_all.shape[1] - fox_heads
    fox_w = (main - 2 * lru_w) // 3
    dh = fox_w // fox_heads
    tm, tn = tiles["mm_m"], tiles["mm_n"]
    xc = _proj_w32(xb, w_in_all, j, 0, lru_w, BF16, tm, tn, "hy_lru_x_proj_conv", "conv", (conv_w, conv_b), s)
    gg = _proj_w32(xb, w_in_all, j, lru_w, lru_w, BF16, tm, tn, "hy_lru_gate_proj_gelu", "gelu")
    qkv = _proj_w32(xb, w_in_all, j, 2 * lru_w, 3 * fox_w, BF16, tm, tn, "hy_qkv_proj", "scale",
                     scale_cols=fox_w // min(tn, fox_w), scale=dh ** -0.5 * LOG2E)
    fl = _proj_small(xb, w_in_all[j, main:, :], tiles["small_m"], "hy_forget_proj")[:, :fox_heads]
    f2 = _fox_prep(fl.reshape(bn, s, fox_heads).transpose(0, 2, 1), b_f)
    y_lru = _lru(xc.reshape(bn, s, lru_w), gg.reshape(bn, s, lru_w), w_r, b_r, w_i, b_i, lam, tiles["lru"])
    y_att = _fox_attention(qkv.reshape(bn, s, 3 * fox_w), f2, fox_heads, dh, tiles["attn"], tiles["attn_heads"])
    return _matmul_residual_ln([y_lru.reshape(t, lru_w), y_att.reshape(t, fox_w)], w_out_all, j, xf, g, b, alpha,
                               tiles["ln_m"], tiles["ln_k"], "hy_out_proj_ln")


def _ssd_mixer_ln(xf, xb, bn, j, w_in_all, conv_w, conv_b, dt_bias, a_log, d_skip, norm_g, w_out_all, g, b, alpha, tiles):
    t, d = xf.shape
    s = t // bn
    heads = a_log.shape[0]
    inner = norm_g.shape[0]
    conv_dim = conv_w.shape[1]
    groups = (conv_dim - inner) // (2 * SSD_STATE)
    tm, tn = tiles["mm_m"], tiles["mm_n"]
    zs = _proj_w32(xb, w_in_all, j, 0, inner, BF16, tm, tn, "ssd_z_proj_silu", "silu")
    xbc = _proj_w32(xb, w_in_all, j, inner, conv_dim, BF16, tm, tn, "ssd_xbc_proj_conv_silu", "conv_silu",
                     (conv_w, conv_b), s)
    dt = _proj_small(xb, w_in_all[j, inner + conv_dim:, :], tiles["small_m"], "ssd_dt_proj")
    y = _ssd_core(xbc.reshape(bn, s, conv_dim), zs.reshape(bn, s, inner), dt.reshape(bn, s, _LANES),
                  dt_bias, a_log, d_skip, norm_g, groups)
    return _matmul_residual_ln([y.reshape(t, inner)], w_out_all, j, xf, g, b, alpha,
                               tiles["ln_m"], tiles["ln_k"], "ssd_out_proj_ln")


_TILES = dict(mm_m=1024, mm_n=1024, small_m=1024, ln_m=512, ln_k=4096, lru=256, attn=512, attn_heads=1,
              router=512, dispatch=1024, expert_rows=512, combine=256)


def kernel(x, ln_g, ln_b, hy_w_in, hy_conv_w, hy_conv_b, hy_w_r, hy_b_r, hy_w_i, hy_b_i, hy_lambda, hy_b_f, hy_w_out, ssd_w_in, ssd_conv_w, ssd_conv_b, ssd_dt_bias, ssd_a_log, ssd_d, ssd_norm_g, ssd_w_out, moe_w_r1, moe_b_r1, moe_w_r2, moe_b_r2, moe_w_gate, moe_w_up, moe_w_down):
    bn, s, d = x.shape
    depth = ln_g.shape[0]
    alpha = (2 * depth) ** 0.25
    ff = moe_w_gate.shape[-1]
    hy_w_in_t = jnp.swapaxes(hy_w_in, 1, 2)
    ssd_w_in_t = jnp.swapaxes(ssd_w_in, 1, 2)
    hy_w_out_b = hy_w_out.astype(BF16)
    ssd_w_out_b = ssd_w_out.astype(BF16)
    w_gate_s = moe_w_gate.reshape(depth * N_EXPERTS, d, ff).astype(BF16)
    w_up_s = moe_w_up.reshape(depth * N_EXPERTS, d, ff).astype(BF16)
    w_down_s = moe_w_down.reshape(depth * N_EXPERTS, ff, d).astype(BF16)
    xf = x.reshape(bn * s, d)
    xb = xf.astype(BF16)
    for layer in range(depth):
        j = layer // 2
        if layer % 2 == 0:
            xf, xb = _hybrid_mixer_ln(xf, xb, bn, j, hy_w_in_t, hy_conv_w[j], hy_conv_b[j], hy_w_r[j], hy_b_r[j],
                                      hy_w_i[j], hy_b_i[j], hy_lambda[j], hy_b_f[j], hy_w_out_b,
                                      ln_g[layer, 0], ln_b[layer, 0], alpha, _TILES)
        else:
            xf, xb = _ssd_mixer_ln(xf, xb, bn, j, ssd_w_in_t, ssd_conv_w[j], ssd_conv_b[j], ssd_dt_bias[j],
                                   ssd_a_log[j], ssd_d[j], ssd_norm_g[j], ssd_w_out_b,
                                   ln_g[layer, 0], ln_b[layer, 0], alpha, _TILES)
        xf, xb = _moe_ln(xf, moe_w_r1[layer], moe_b_r1[layer], moe_w_r2[layer], moe_b_r2[layer],
                         w_gate_s, w_up_s, w_down_s, layer, ln_g[layer, 1], ln_b[layer, 1], alpha, _TILES)
    return xf.reshape(bn, s, d)
```

```python
import functools
import math

import jax
import jax.numpy as jnp
from jax import lax
from jax.experimental import pallas as pl
from jax.experimental.pallas import tpu as pltpu

F32 = jnp.float32
BF16 = jnp.bfloat16

_VMEM_LIMIT_BYTES = 56 * 1024 * 1024
_LANES = 128
_SUBLANES = 8

LN_EPS = 1e-5
RMS_EPS = 1e-5
LRU_C = 8.0
CONV_K = 4
SSD_CHUNK = 128
SSD_HEAD_DIM = 64
SSD_STATE = 128
MOE_GROUPS = 4
MOE_PER_GROUP = 4
N_EXPERTS = MOE_GROUPS * MOE_PER_GROUP
ROUTER_COLS = 128
ROUTER_L2_ROW0 = 8
LOG2E = math.log2(math.e)


def _params(*sem):
    return pltpu.CompilerParams(dimension_semantics=sem, vmem_limit_bytes=_VMEM_LIMIT_BYTES)


def _sigmoid(x):
    return 1.0 / (1.0 + jnp.exp(-x))


def _softplus(x):
    e = jnp.exp(-jnp.abs(x))
    u = 1.0 + e
    return jnp.maximum(x, 0.0) + jnp.where(u == 1.0, e, jnp.log(u) * (e / (u - 1.0)))


def _shift_rows(x, tail, d):
    r = pltpu.roll(x, d, axis=0)
    t = pltpu.roll(tail, d, axis=0)
    row = lax.broadcasted_iota(jnp.int32, tail.shape, 0)
    first = jnp.where(row < d, t, r[:_SUBLANES])
    return jnp.concatenate([first, r[_SUBLANES:]], axis=0)


def _causal_conv(x, tail, w_ref, b_ref):
    acc = b_ref[...] + w_ref[CONV_K - 1:CONV_K, :] * x
    for d in range(1, CONV_K):
        acc = acc + w_ref[CONV_K - 1 - d:CONV_K - d, :] * _shift_rows(x, tail, d)
    return acc


def _cumsum_rows(x):
    n = x.shape[0]
    row = lax.broadcasted_iota(jnp.int32, x.shape, 0)
    d = 1
    while d < n:
        x = x + jnp.where(row >= d, pltpu.roll(x, d, axis=0), 0.0)
        d *= 2
    return x


def _cumsum_lanes(x):
    n = x.shape[1]
    lane = lax.broadcasted_iota(jnp.int32, x.shape, 1)
    d = 1
    while d < n:
        x = x + jnp.where(lane >= d, pltpu.roll(x, d, axis=1), 0.0)
        d *= 2
    return x


def _layer_norm(s, g, b):
    mu = jnp.mean(s, axis=-1, keepdims=True)
    d = s - mu
    var = jnp.mean(d * d, axis=-1, keepdims=True)
    return d * lax.rsqrt(var + LN_EPS) * g + b


def _proj_small_body(x_ref, wt_ref, o_ref):
    o_ref[...] = lax.dot_general(x_ref[...], wt_ref[...].astype(BF16), (((1,), (1,)), ((), ())),
                                 preferred_element_type=F32)


def _proj_small(x, w_t, tm, name):
    m, k = x.shape
    tm = min(tm, m)
    w_t = jnp.pad(w_t, ((0, _LANES - w_t.shape[0]), (0, 0)))
    return pl.pallas_call(
        _proj_small_body,
        grid=(m // tm,),
        in_specs=[pl.BlockSpec((tm, k), lambda i: (i, 0)),
                  pl.BlockSpec((_LANES, k), lambda i: (0, 0))],
        out_specs=pl.BlockSpec((tm, _LANES), lambda i: (i, 0)),
        out_shape=jax.ShapeDtypeStruct((m, _LANES), F32),
        compiler_params=_params("parallel"),
        name=name,
    )(x, w_t)


def _proj_w32_body(x_ref, w_ref, *rest, epilogue, tiles_per_seq, chunk, scale_cols, scale):
    conv = epilogue.startswith("conv")
    if conv:
        cw_ref, cb_ref, o_ref, wb_ref, carry_ref = rest
    else:
        o_ref, wb_ref = rest
    j = pl.program_id(0)
    t = pl.program_id(1)
    tm = x_ref.shape[0]

    @pl.when(t == 0)
    def _():
        wb_ref[...] = w_ref[...].T.astype(BF16)

    if conv:
        @pl.when(t % tiles_per_seq == 0)
        def _():
            carry_ref[...] = jnp.zeros_like(carry_ref)

    x = x_ref[...]
    for c0 in range(0, o_ref.shape[1], chunk):
        cols = slice(c0, c0 + chunk)
        acc = jnp.dot(x, wb_ref[:, cols], preferred_element_type=F32)
        if conv:
            tail = carry_ref[:, cols]
            carry_ref[:, cols] = acc[tm - _SUBLANES:, :]
            acc = _causal_conv(acc, tail, cw_ref.at[:, cols], cb_ref.at[:, cols])
        if epilogue.endswith("silu"):
            acc = acc * _sigmoid(acc)
        elif epilogue == "gelu":
            acc = jax.nn.gelu(acc)
        elif epilogue == "scale":
            acc = acc * jnp.where(j < scale_cols, scale, 1.0)
        o_ref[:, cols] = acc.astype(o_ref.dtype)


def _proj_w32(x, w_all, layer, col0, n, out_dtype, tm, tn, name, epilogue="none", conv=None, seq_len=None,
              scale_cols=0, scale=1.0, chunk=256):
    m, k = x.shape
    tm, tn = min(tm, m), min(tn, n)
    assert col0 % tn == 0 and n % tn == 0 and m % tm == 0 and tn % chunk == 0
    jb = col0 // tn
    in_specs = [pl.BlockSpec((tm, k), lambda j, t: (t, 0)),
                pl.BlockSpec((None, tn, k), lambda j, t: (layer, jb + j, 0))]
    args = [x, w_all]
    scratch = [pltpu.VMEM((k, tn), BF16)]
    tiles_per_seq = 1
    if conv is not None:
        assert seq_len % tm == 0
        tiles_per_seq = seq_len // tm
        in_specs += [pl.BlockSpec((CONV_K, tn), lambda j, t: (0, j)),
                     pl.BlockSpec((1, tn), lambda j, t: (0, j))]
        args += [conv[0], conv[1].reshape(1, n)]
        scratch.append(pltpu.VMEM((_SUBLANES, tn), F32))
    return pl.pallas_call(
        functools.partial(_proj_w32_body, epilogue=epilogue, tiles_per_seq=tiles_per_seq, chunk=chunk,
                          scale_cols=scale_cols, scale=scale),
        grid=(n // tn, m // tm),
        in_specs=in_specs,
        out_specs=pl.BlockSpec((tm, tn), lambda j, t: (t, j)),
        out_shape=jax.ShapeDtypeStruct((m, n), out_dtype),
        scratch_shapes=scratch,
        compiler_params=_params("arbitrary", "arbitrary"),
        name=name,
    )(*args)


def _mm_ln_body(*refs, alpha, n_a, nk):
    a_refs = refs[:n_a]
    w_ref, x_ref, g_ref, b_ref, of_ref, ob_ref = refs[n_a:n_a + 6]
    ka = a_refs[0].shape[1]
    part = jnp.dot(a_refs[0][...], w_ref[:ka, :], preferred_element_type=F32)
    for idx in range(1, n_a):
        part = part + jnp.dot(a_refs[idx][...], w_ref[idx * ka:(idx + 1) * ka, :], preferred_element_type=F32)

    def finish(acc):
        y = _layer_norm(alpha * x_ref[...] + acc, g_ref[...], b_ref[...])
        of_ref[...] = y
        ob_ref[...] = y.astype(BF16)

    if nk == 1:
        finish(part)
        return
    acc_ref = refs[n_a + 6]
    k = pl.program_id(1)

    @pl.when(k == 0)
    def _():
        acc_ref[...] = part

    @pl.when(k > 0)
    def _():
        acc_ref[...] += part

    @pl.when(k == nk - 1)
    def _():
        finish(acc_ref[...])


def _matmul_residual_ln(a_list, w_all, layer, x, g, b, alpha, tm, tk, name):
    n_a = len(a_list)
    m, ka = a_list[0].shape
    n = w_all.shape[2]
    tm = min(tm, m)
    tka = ka if n_a > 1 else min(tk, ka)
    nk = ka // tka
    w_mode = dict(pipeline_mode=pl.Buffered(1)) if nk == 1 else {}
    return pl.pallas_call(
        functools.partial(_mm_ln_body, alpha=alpha, n_a=n_a, nk=nk),
        grid=(m // tm, nk),
        in_specs=[pl.BlockSpec((tm, tka), lambda i, k: (i, k))] * n_a + [
                  pl.BlockSpec((None, n_a * tka, n), lambda i, k: (layer, k, 0), **w_mode),
                  pl.BlockSpec((tm, n), lambda i, k: (i, 0)),
                  pl.BlockSpec((1, n), lambda i, k: (0, 0)),
                  pl.BlockSpec((1, n), lambda i, k: (0, 0))],
        out_specs=[pl.BlockSpec((tm, n), lambda i, k: (i, 0)),
                   pl.BlockSpec((tm, n), lambda i, k: (i, 0))],
        out_shape=[jax.ShapeDtypeStruct((m, n), F32), jax.ShapeDtypeStruct((m, n), BF16)],
        scratch_shapes=[pltpu.VMEM((tm, n), F32)] if nk > 1 else [],
        compiler_params=_params("parallel", "arbitrary"),
        name=name,
    )(*a_list, w_all, x, g.reshape(1, n), b.reshape(1, n))


def _fox_prep_body(fl_ref, bf_ref, f_ref):
    x = fl_ref[0] + bf_ref[...]
    f_ref[0] = _cumsum_lanes(-_softplus(-x)) * LOG2E


def _fox_prep(fl_t, b_f):
    bn, h, s = fl_t.shape
    return pl.pallas_call(
        _fox_prep_body,
        grid=(bn,),
        in_specs=[pl.BlockSpec((1, h, s), lambda b: (b, 0, 0)),
                  pl.BlockSpec((h, 1), lambda b: (0, 0))],
        out_specs=pl.BlockSpec((1, h, s), lambda b: (b, 0, 0)),
        out_shape=jax.ShapeDtypeStruct((bn, h, s), F32),
        compiler_params=_params("parallel"),
        name="fox_prep",
    )(fl_t, b_f.reshape(h, 1))


def _lru_body(xc_ref, gg_ref, wr_ref, br_ref, wi_ref, bi_ref, lam_ref, y_ref, h_ref, acum_ref, hloc_ref,
              *, heads, block):
    @pl.when(pl.program_id(1) == 0)
    def _():
        h_ref[...] = jnp.zeros_like(h_ref)

    xcb = xc_ref[0]
    xc = xcb.astype(F32)
    ts = xc.shape[0]
    r_parts, i_parts = [], []
    for hd in range(heads):
        xh = xcb[:, hd * block:(hd + 1) * block]
        r_parts.append(jnp.dot(xh, wr_ref[hd], preferred_element_type=F32))
        i_parts.append(jnp.dot(xh, wi_ref[hd], preferred_element_type=F32))
    r = _sigmoid(jnp.concatenate(r_parts, axis=1) + br_ref[...])
    ig = _sigmoid(jnp.concatenate(i_parts, axis=1) + bi_ref[...])
    log_a = -LRU_C * r * _softplus(-lam_ref[...])
    a = jnp.exp(log_a)
    u = jnp.sqrt(-jnp.tanh(log_a) * (a * a + 1.0)) * (ig * xc)

    width = a.shape[1]
    a3 = a.reshape(ts // _SUBLANES, _SUBLANES, width)
    u3 = u.reshape(ts // _SUBLANES, _SUBLANES, width)
    row = lax.broadcasted_iota(jnp.int32, a3.shape, 1)
    d = 1
    while d < _SUBLANES:
        keep = row >= d
        u3 = u3 + a3 * jnp.where(keep, pltpu.roll(u3, d, axis=1), 0.0)
        a3 = a3 * jnp.where(keep, pltpu.roll(a3, d, axis=1), 1.0)
        d *= 2
    acum_ref[...] = a3
    hloc_ref[...] = u3

    def carry_group(g, h_prev):
        hg = hloc_ref[g] + acum_ref[g] * h_prev
        hloc_ref[g] = hg
        return hg[_SUBLANES - 1:_SUBLANES, :]

    h_ref[...] = lax.fori_loop(0, ts // _SUBLANES, carry_group, h_ref[...], unroll=4)
    h = hloc_ref[...].reshape(ts, width)
    y_ref[0] = (h * gg_ref[0].astype(F32)).astype(y_ref.dtype)


def _lru(xc, gg, w_r, b_r, w_i, b_i, lam, ts):
    bn, s, width = xc.shape
    heads, block, _ = w_r.shape
    ts = min(ts, s)
    vec = lambda v: v.reshape(1, width)
    const2 = lambda b, i: (0, 0)
    const3 = lambda b, i: (0, 0, 0)
    tile = pl.BlockSpec((1, ts, width), lambda b, i: (b, i, 0))
    return pl.pallas_call(
        functools.partial(_lru_body, heads=heads, block=block),
        grid=(bn, s // ts),
        in_specs=[tile, tile,
                  pl.BlockSpec((heads, block, block), const3),
                  pl.BlockSpec((1, width), const2),
                  pl.BlockSpec((heads, block, block), const3),
                  pl.BlockSpec((1, width), const2),
                  pl.BlockSpec((1, width), const2)],
        out_specs=tile,
        out_shape=jax.ShapeDtypeStruct((bn, s, width), BF16),
        scratch_shapes=[pltpu.VMEM((1, width), F32),
                        pltpu.VMEM((ts // _SUBLANES, _SUBLANES, width), F32),
                        pltpu.VMEM((ts // _SUBLANES, _SUBLANES, width), F32)],
        compiler_params=_params("parallel", "arbitrary"),
        name="rg_lru",
    )(xc, gg, w_r.astype(BF16), vec(b_r), w_i.astype(BF16), vec(b_i), vec(lam))


def _split3(f):
    hi = f.astype(BF16).astype(F32)
    r = f - hi
    mid = r.astype(BF16).astype(F32)
    return hi, mid, r - mid


def _fox_attn_body(q_ref, k_ref, v_ref, fc_ref, o_ref, kx_ref, *, tq, hp, dh):
    qi = pl.program_id(2)
    s_len = k_ref.shape[1]
    heads = range(hp)
    lane = lax.broadcasted_iota(jnp.int32, (tq, _LANES), 1)

    @pl.when(qi == 0)
    def _():
        for e in heads:
            for c0 in range(0, s_len, tq):
                hi, mid, lo = _split3(fc_ref[0, e, c0:c0 + tq, :])
                kx = jnp.where(lane < 3, 1.0, jnp.where(lane == 3, -hi, jnp.where(lane == 4, -mid,
                               jnp.where(lane == 5, -lo, 0.0))))
                kx_ref[e, c0:c0 + tq, :] = kx.astype(BF16)

    qs = []
    for e in heads:
        hi, mid, lo = _split3(fc_ref[0, e, pl.ds(pl.multiple_of(qi * tq, tq), tq), :])
        qx = jnp.where(lane == 0, hi, jnp.where(lane == 1, mid, jnp.where(lane == 2, lo,
                       jnp.where(lane < 6, 1.0, 0.0))))
        qs.append(jnp.concatenate([q_ref[0, :, e * dh:(e + 1) * dh], qx.astype(BF16)], axis=1))

    def scores(e, j):
        start = pl.multiple_of(j * tq, tq)
        k = jnp.concatenate([k_ref[0, pl.ds(start, tq), e * dh:(e + 1) * dh], kx_ref[e, pl.ds(start, tq), :]], axis=1)
        return lax.dot_general(qs[e], k, (((1,), (1,)), ((), ())), preferred_element_type=F32)

    def update(e, carry, sc, j):
        m, l, acc = carry
        m_new = jnp.maximum(m, jnp.max(sc, axis=1, keepdims=True))
        corr = jnp.exp2(m - m_new)
        p = jnp.exp2(sc - m_new)
        l = corr * l + jnp.sum(p, axis=1, keepdims=True)
        v = v_ref[0, pl.ds(pl.multiple_of(j * tq, tq), tq), e * dh:(e + 1) * dh]
        acc = corr * acc + jnp.dot(p.astype(BF16), v, preferred_element_type=F32)
        return m_new, l, acc

    def full_blocks(carries, js):
        scs = [[scores(e, j) for j in js] for e in heads]
        out = []
        for e in heads:
            c = carries[e]
            for sc, j in zip(scs[e], js):
                c = update(e, c, sc, j)
            out.append(c)
        return tuple(out)

    def diagonal(carries):
        scs = [scores(e, qi) for e in heads]
        rq = lax.broadcasted_iota(jnp.int32, (tq, tq), 0)
        ck = lax.broadcasted_iota(jnp.int32, (tq, tq), 1)
        return tuple(update(e, carries[e], jnp.where(ck <= rq, scs[e], -1e30), qi) for e in heads)

    def two_blocks(jj, carries):
        return full_blocks(carries, (2 * jj, 2 * jj + 1))

    def odd_tail(carries):
        return diagonal(full_blocks(carries, (qi - 1,)))

    init = tuple((jnp.full((tq, 1), -1e30, F32), jnp.zeros((tq, 1), F32), jnp.zeros((tq, dh), F32))
                 for _ in heads)
    carries = lax.fori_loop(0, qi // 2, two_blocks, init)
    carries = lax.cond(qi % 2 == 1, odd_tail, diagonal, carries)
    o_ref[0] = jnp.concatenate([acc / l for _, l, acc in carries], axis=1).astype(o_ref.dtype)


def _fox_attention(qkv, f2, heads, dh, tq, hp):
    bn, s, _ = qkv.shape
    tq = min(tq, s)
    assert heads % hp == 0
    hb = heads // hp
    return pl.pallas_call(
        functools.partial(_fox_attn_body, tq=tq, hp=hp, dh=dh),
        grid=(bn, hb, s // tq),
        in_specs=[pl.BlockSpec((1, tq, hp * dh), lambda b, h, i: (b, i, h)),
                  pl.BlockSpec((1, s, hp * dh), lambda b, h, i: (b, 0, hb + h)),
                  pl.BlockSpec((1, s, hp * dh), lambda b, h, i: (b, 0, 2 * hb + h)),
                  pl.BlockSpec((1, hp, s, 1), lambda b, h, i: (b, h, 0, 0))],
        out_specs=pl.BlockSpec((1, tq, hp * dh), lambda b, h, i: (b, i, h)),
        out_shape=jax.ShapeDtypeStruct((bn, s, heads * dh), BF16),
        scratch_shapes=[pltpu.VMEM((hp, s, _LANES), BF16)],
        compiler_params=_params("parallel", "parallel", "arbitrary"),
        name="fox_attention",
    )(qkv, qkv, qkv, f2[..., None])


def _ssd_body(xbc_ref, z_ref, dt_ref, dtb_ref, alog_ref, dexp_ref, ng_ref, o_ref,
              state_ref, acst_ref, dtt_ref, wt_ref, *, groups, inner):
    L = SSD_CHUNK
    gw = inner // groups
    pairs = gw // _LANES
    hpg = gw // SSD_HEAD_DIM

    @pl.when(pl.program_id(1) == 0)
    def _():
        state_ref[...] = jnp.zeros_like(state_ref)

    dt = _softplus(dt_ref[0] + dtb_ref[...])
    acs = _cumsum_rows(dt * (-jnp.exp(alog_ref[...]))) * LOG2E
    acs_t = acs.T
    dt_t = dt.T
    acst_ref[...] = acs_t
    dtt_ref[...] = dt_t
    wt_ref[...] = dt_t * jnp.exp2(acs_t[:, L - 1:L] - acs_t)
    dec_last = jnp.exp2(acs[L - 1:L, :])

    rq = lax.broadcasted_iota(jnp.int32, (L, L), 0)
    ck = lax.broadcasted_iota(jnp.int32, (L, L), 1)
    causal = ck <= rq
    lo = lax.broadcasted_iota(jnp.int32, (L, _LANES), 1) < SSD_HEAD_DIM
    lo_row = lo[:1]

    for g in range(groups):
        bmat = xbc_ref[0, :, inner + g * SSD_STATE:inner + (g + 1) * SSD_STATE]
        cmat = xbc_ref[0, :, inner + (groups + g) * SSD_STATE:inner + (groups + g + 1) * SSD_STATE]
        cbm = lax.dot_general(cmat, bmat, (((1,), (1,)), ((), ())), preferred_element_type=F32)
        b_t = bmat.astype(F32).T
        ys = []
        for j in range(pairs):
            col0 = g * gw + j * _LANES
            xs_p = xbc_ref[0, :, col0:col0 + _LANES].astype(F32)
            m_parts, bw_parts, cols, dl = [], [], [], []
            for e in range(2):
                hd = g * hpg + 2 * j + e
                col = jnp.broadcast_to(acs[:, hd:hd + 1], (L, L))
                row = acst_ref[hd:hd + 1, :]
                dec = jnp.exp2(jnp.where(causal, col - row, -jnp.inf))
                m_parts.append((cbm * dec * dtt_ref[hd:hd + 1, :]).astype(BF16))
                bw_parts.append((b_t * wt_ref[hd:hd + 1, :]).astype(BF16))
                cols.append(col)
                dl.append(dec_last[:, hd:hd + 1])
            prev = state_ref[:, col0:col0 + _LANES]
            x2 = jnp.concatenate([jnp.where(lo, xs_p, 0.0).astype(BF16),
                                  jnp.where(lo, 0.0, xs_p).astype(BF16)], axis=0)
            y_diag = jnp.dot(jnp.concatenate(m_parts, axis=1), x2, preferred_element_type=F32)
            y_off = jnp.dot(cmat, prev.astype(BF16), preferred_element_type=F32)
            y = y_diag + jnp.exp2(jnp.where(lo, cols[0], cols[1])) * y_off
            st = jnp.dot(jnp.concatenate(bw_parts, axis=1), x2, preferred_element_type=F32)
            state_ref[:, col0:col0 + _LANES] = prev * jnp.where(lo_row, dl[0], dl[1]) + st
            y = y + dexp_ref[:, col0:col0 + _LANES] * xs_p
            ys.append(y * z_ref[0, :, col0:col0 + _LANES].astype(F32))
        yg = jnp.concatenate(ys, axis=1)
        ms = jnp.mean(yg * yg, axis=1, keepdims=True)
        gcols = slice(g * gw, (g + 1) * gw)
        o_ref[0, :, gcols] = (yg * lax.rsqrt(ms + RMS_EPS) * ng_ref[:, gcols]).astype(o_ref.dtype)


def _ssd_core(xbc, zs, dt, dt_bias, a_log, d_skip, norm_g, groups):
    bn, s, conv_dim = xbc.shape
    inner = zs.shape[2]
    heads = a_log.shape[0]
    L = SSD_CHUNK
    pad = lambda v: jnp.pad(v, (0, _LANES - heads)).reshape(1, _LANES)
    const2 = lambda b, c: (0, 0)
    return pl.pallas_call(
        functools.partial(_ssd_body, groups=groups, inner=inner),
        grid=(bn, s // L),
        in_specs=[pl.BlockSpec((1, L, conv_dim), lambda b, c: (b, c, 0)),
                  pl.BlockSpec((1, L, inner), lambda b, c: (b, c, 0)),
                  pl.BlockSpec((1, L, _LANES), lambda b, c: (b, c, 0)),
                  pl.BlockSpec((1, _LANES), const2),
                  pl.BlockSpec((1, _LANES), const2),
                  pl.BlockSpec((1, inner), const2),
                  pl.BlockSpec((1, inner), const2)],
        out_specs=pl.BlockSpec((1, L, inner), lambda b, c: (b, c, 0)),
        out_shape=jax.ShapeDtypeStruct((bn, s, inner), BF16),
        scratch_shapes=[pltpu.VMEM((SSD_STATE, inner), F32),
                        pltpu.VMEM((_LANES, L), F32),
                        pltpu.VMEM((_LANES, L), F32),
                        pltpu.VMEM((_LANES, L), F32)],
        compiler_params=_params("parallel", "arbitrary"),
        name="ssd_core",
    )(xbc, zs, dt, pad(dt_bias), pad(a_log),
      jnp.repeat(d_skip, SSD_HEAD_DIM).reshape(1, inner), norm_g.reshape(1, inner))


def _router_body(x_ref, whl_ref, wh_ref, b_ref, oi_ref, of_ref, cnt_ref, base_ref):
    i = pl.program_id(0)

    @pl.when(i == 0)
    def _():
        base_ref[...] = jnp.zeros_like(base_ref)

    x = x_ref[...]
    tm = x.shape[0]
    xh = x.astype(BF16)
    xl = (x - xh.astype(F32)).astype(BF16)
    r = jnp.dot(xh, whl_ref[...], preferred_element_type=F32)
    logits = (r[:, :ROUTER_COLS] + r[:, ROUTER_COLS:]
              + jnp.dot(xl, wh_ref[...], preferred_element_type=F32) + b_ref[...])
    lt = logits.T

    row = lax.broadcasted_iota(jnp.int32, (_SUBLANES, tm), 0)
    neg = -jnp.inf
    l1 = jnp.where(row < MOE_GROUPS, lt[:_SUBLANES], neg)
    e1 = jnp.exp(l1 - jnp.max(l1, axis=0, keepdims=True))
    p1 = e1 / jnp.sum(e1, axis=0, keepdims=True)
    pg = jnp.max(p1, axis=0, keepdims=True)
    gidx = jnp.min(jnp.where(p1 == pg, row, _SUBLANES), axis=0, keepdims=True)

    sel = jnp.full((_SUBLANES, tm), neg, F32)
    for gi in range(MOE_GROUPS):
        r0 = ROUTER_L2_ROW0 + _SUBLANES * gi
        sel = jnp.where((gidx == gi) & (row < MOE_PER_GROUP), lt[r0:r0 + _SUBLANES], sel)
    va = jnp.max(sel, axis=0, keepdims=True)
    ia = jnp.min(jnp.where(sel == va, row, _SUBLANES), axis=0, keepdims=True)
    sel_b = jnp.where(row == ia, neg, sel)
    vb = jnp.max(sel_b, axis=0, keepdims=True)
    ib = jnp.min(jnp.where(sel_b == vb, row, _SUBLANES), axis=0, keepdims=True)
    t = jnp.exp(vb - va)
    ca = pg * (1.0 / (1.0 + t))
    cb = pg * (t / (1.0 + t))
    ea = gidx * MOE_PER_GROUP + ia
    eb = gidx * MOE_PER_GROUP + ib

    erow = lax.broadcasted_iota(jnp.int32, (N_EXPERTS, tm), 0)
    hit_a = erow == ea
    hit_b = erow == eb
    oh = jnp.where(hit_a, 1.0, 0.0) + jnp.where(hit_b, 1.0, 0.0)
    before = (lax.broadcasted_iota(jnp.int32, (tm, tm), 0)
              < lax.broadcasted_iota(jnp.int32, (tm, tm), 1))
    prefix = jnp.dot(oh.astype(BF16), jnp.where(before, 1.0, 0.0).astype(BF16),
                     preferred_element_type=F32) + base_ref[...]
    rank_a = jnp.sum(jnp.where(hit_a, prefix, 0.0), axis=0, keepdims=True)
    rank_b = jnp.sum(jnp.where(hit_b, prefix, 0.0), axis=0, keepdims=True)
    base_ref[...] += jnp.sum(oh, axis=1, keepdims=True)
    cnt_ref[...] = jnp.broadcast_to(base_ref[...], cnt_ref.shape)

    zi = jnp.zeros((1, tm), jnp.int32)
    oi_ref[...] = jnp.concatenate(
        [ea, eb, rank_a.astype(jnp.int32), rank_b.astype(jnp.int32), zi, zi, zi, zi], axis=0)
    zf = jnp.zeros((1, tm), F32)
    of_ref[...] = jnp.concatenate([ca, cb, zf, zf, zf, zf, zf, zf], axis=0)


def _router(x, w_r1, b_r1, w_r2, b_r2, tm):
    t, d = x.shape
    tm = min(tm, t)
    def slots(first, per_group):
        pad8 = lambda v: jnp.pad(v, [(0, 0)] * (v.ndim - 1) + [(0, _SUBLANES - v.shape[-1])])
        parts = [pad8(first)] + [pad8(per_group[gi]) for gi in range(MOE_GROUPS)]
        out = jnp.concatenate(parts, axis=-1)
        return jnp.pad(out, [(0, 0)] * (out.ndim - 1) + [(0, ROUTER_COLS - out.shape[-1])])

    assert ROUTER_L2_ROW0 == _SUBLANES and MOE_GROUPS <= _SUBLANES and MOE_PER_GROUP <= _SUBLANES
    w = slots(w_r1, w_r2)
    b = slots(b_r1, b_r2)
    wh = w.astype(BF16)
    wl = (w - wh.astype(F32)).astype(BF16)
    const = lambda i: (0, 0)
    return pl.pallas_call(
        _router_body,
        grid=(t // tm,),
        in_specs=[pl.BlockSpec((tm, d), lambda i: (i, 0)),
                  pl.BlockSpec((d, 2 * ROUTER_COLS), const),
                  pl.BlockSpec((d, ROUTER_COLS), const),
                  pl.BlockSpec((1, ROUTER_COLS), const)],
        out_specs=[pl.BlockSpec((_SUBLANES, tm), lambda i: (0, i)),
                   pl.BlockSpec((_SUBLANES, tm), lambda i: (0, i)),
                   pl.BlockSpec((N_EXPERTS, _LANES), const)],
        out_shape=[jax.ShapeDtypeStruct((_SUBLANES, t), jnp.int32),
                   jax.ShapeDtypeStruct((_SUBLANES, t), F32),
                   jax.ShapeDtypeStruct((N_EXPERTS, _LANES), F32)],
        scratch_shapes=[pltpu.VMEM((N_EXPERTS, 1), F32)],
        compiler_params=_params("arbitrary"),
        name="moe_router",
    )(x, jnp.concatenate([wh, wl], axis=1), wh, b.reshape(1, ROUTER_COLS))


def _row_copy(src, src_row, dst, dst_row, sem):
    return pltpu.make_async_copy(src.at[pl.ds(src_row, 1)], dst.at[pl.ds(dst_row, 1)], sem)


def _dispatch_body(pos_ref, zt_ref, x_ref, xs_ref, zeros_ref, sem, zsem, *, td, tz, t_total):
    step = pl.program_id(0)
    base = step * td

    def zero_tile(i):
        return pltpu.make_async_copy(zeros_ref, xs_ref.at[pl.ds(pl.multiple_of(i * tz, tz), tz)], zsem)

    @pl.when(step == 0)
    def _():
        zeros_ref[...] = jnp.zeros_like(zeros_ref)

        def start(i, carry):
            @pl.when(zt_ref[i] == 1)
            def _():
                zero_tile(i).start()
            return carry

        def wait(i, carry):
            @pl.when(zt_ref[i] == 1)
            def _():
                zero_tile(i).wait()
            return carry

        lax.fori_loop(0, zt_ref.shape[0], start, 0)
        lax.fori_loop(0, zt_ref.shape[0], wait, 0)

    def issue(r, carry):
        _row_copy(x_ref, r, xs_ref, pos_ref[base + r], sem).start()
        _row_copy(x_ref, r, xs_ref, pos_ref[t_total + base + r], sem).start(priority=1)
        return carry

    lax.fori_loop(0, td, issue, 0, unroll=4)
    tile_rows = xs_ref.at[pl.ds(0, td)]
    pltpu.make_async_copy(x_ref, tile_rows, sem).wait()
    pltpu.make_async_copy(x_ref, tile_rows, sem).wait()


def _dispatch(x, pos, zero_tiles, n_rows, td, tz):
    t, d = x.shape
    td = min(td, t)
    return pl.pallas_call(
        functools.partial(_dispatch_body, td=td, tz=tz, t_total=t),
        grid_spec=pltpu.PrefetchScalarGridSpec(
            num_scalar_prefetch=2,
            grid=(t // td,),
            in_specs=[pl.BlockSpec((td, d), lambda i, pos, zt: (i, 0))],
            out_specs=pl.BlockSpec(memory_space=pl.ANY),
            scratch_shapes=[pltpu.VMEM((tz, d), x.dtype),
                            pltpu.SemaphoreType.DMA(()), pltpu.SemaphoreType.DMA(())]),
        out_shape=jax.ShapeDtypeStruct((n_rows, d), x.dtype),
        compiler_params=_params("arbitrary"),
        name="moe_dispatch",
    )(pos, zero_tiles, x)


def _expert_body(blk_ref, te_ref, nv_ref, xs_ref, wg_ref, wu_ref, wd_ref, o_ref):
    del blk_ref, te_ref
    valid = pl.program_id(0) < nv_ref[0]

    @pl.when(jnp.logical_not(valid))
    def _():
        o_ref[...] = jnp.zeros_like(o_ref)

    @pl.when(valid)
    def _():
        x = xs_ref[...].astype(BF16)
        g = jnp.dot(x, wg_ref[0], preferred_element_type=F32)
        u = jnp.dot(x, wu_ref[0], preferred_element_type=F32)
        hid = (g * _sigmoid(g)) * u
        o_ref[...] = jnp.dot(hid.astype(BF16), wd_ref[0], preferred_element_type=F32)


def _experts(xs, w_gate, w_up, w_down, expert0, blk, tile_expert, n_valid, tm):
    p, d = xs.shape
    ff = w_gate.shape[2]
    n_tiles = p // tm
    rows = lambda i, blk, te, nv: (blk[i], 0)
    wsel = lambda i, blk, te, nv: (expert0 + te[i], 0, 0)
    return pl.pallas_call(
        _expert_body,
        grid_spec=pltpu.PrefetchScalarGridSpec(
            num_scalar_prefetch=3,
            grid=(n_tiles,),
            in_specs=[pl.BlockSpec((tm, d), rows),
                      pl.BlockSpec((1, d, ff), wsel),
                      pl.BlockSpec((1, d, ff), wsel),
                      pl.BlockSpec((1, ff, d), wsel)],
            out_specs=pl.BlockSpec((tm, d), lambda i, blk, te, nv: (i, 0))),
        out_shape=jax.ShapeDtypeStruct((p, d), F32),
        compiler_params=_params("arbitrary"),
        name="moe_experts",
    )(blk, tile_expert, n_valid, xs, w_gate, w_up, w_down)


def _combine_body(pos_ref, ys_ref, x_ref, c_ref, g_ref, b_ref, of_ref, ob_ref, buf_a, buf_b, sem,
                  *, tc, t_total, alpha):
    step = pl.program_id(0)
    slot = step % 2

    def issue(tile, which):
        base = tile * tc

        def body(r, carry):
            _row_copy(ys_ref, pos_ref[base + r], buf_a.at[which], r, sem.at[which]).start()
            _row_copy(ys_ref, pos_ref[t_total + base + r], buf_b.at[which], r, sem.at[which]).start(priority=1)
            return carry

        lax.fori_loop(0, tc, body, 0, unroll=4)

    @pl.when(step == 0)
    def _():
        issue(0, 0)

    tile_rows = ys_ref.at[pl.ds(0, tc)]
    pltpu.make_async_copy(tile_rows, buf_a.at[slot], sem.at[slot]).wait()
    pltpu.make_async_copy(tile_rows, buf_b.at[slot], sem.at[slot]).wait()

    last = pl.num_programs(0) - 1
    nbase = jnp.minimum(step + 1, last) * tc
    for r in range(tc):
        _row_copy(ys_ref, pos_ref[nbase + r], buf_a.at[1 - slot], r, sem.at[1 - slot]).start()
        _row_copy(ys_ref, pos_ref[t_total + nbase + r], buf_b.at[1 - slot], r, sem.at[1 - slot]).start(priority=1)
    c = c_ref[...]
    f = c[:, 0:1] * buf_a[slot] + c[:, 1:2] * buf_b[slot]
    y = _layer_norm(alpha * x_ref[...] + f, g_ref[...], b_ref[...])
    of_ref[...] = y
    ob_ref[...] = y.astype(BF16)

    @pl.when(step == last)
    def _():
        pltpu.make_async_copy(tile_rows, buf_a.at[1 - slot], sem.at[1 - slot]).wait()
        pltpu.make_async_copy(tile_rows, buf_b.at[1 - slot], sem.at[1 - slot]).wait()


def _combine_ln(ys, pos, x, c, g, b, alpha, tc):
    t, d = x.shape
    tc = min(tc, t)
    const = lambda i, pos: (0, 0)
    tile = lambda i, pos: (i, 0)
    return pl.pallas_call(
        functools.partial(_combine_body, tc=tc, t_total=t, alpha=alpha),
        grid_spec=pltpu.PrefetchScalarGridSpec(
            num_scalar_prefetch=1,
            grid=(t // tc,),
            in_specs=[pl.BlockSpec(memory_space=pl.ANY),
                      pl.BlockSpec((tc, d), tile),
                      pl.BlockSpec((tc, 2), tile),
                      pl.BlockSpec((1, d), const),
                      pl.BlockSpec((1, d), const)],
            out_specs=[pl.BlockSpec((tc, d), tile), pl.BlockSpec((tc, d), tile)],
            scratch_shapes=[pltpu.VMEM((2, tc, d), F32), pltpu.VMEM((2, tc, d), F32),
                            pltpu.SemaphoreType.DMA((2,))]),
        out_shape=[jax.ShapeDtypeStruct((t, d), F32), jax.ShapeDtypeStruct((t, d), BF16)],
        compiler_params=_params("arbitrary"),
        name="moe_combine_ln",
    )(pos, ys, x, c, g.reshape(1, d), b.reshape(1, d))


def _moe_ln(x, w_r1, b_r1, w_r2, b_r2, w_gate, w_up, w_down, layer, g, b, alpha, tiles):
    t, d = x.shape
    tm_e = min(tiles["expert_rows"], t)
    oi, of, cnt = _router(x, w_r1, b_r1, w_r2, b_r2, tiles["router"])
    counts = cnt[:, 0].astype(jnp.int32)
    seg_tiles = (counts + tm_e - 1) // tm_e
    tile_end = jnp.cumsum(seg_tiles)
    seg_start = (tile_end - seg_tiles) * tm_e
    start_of = lambda e: jnp.sum(jnp.where(e[:, None] == jnp.arange(N_EXPERTS, dtype=jnp.int32)[None, :],
                                           seg_start[None, :], 0), axis=1)
    pos = jnp.concatenate([start_of(oi[0]) + oi[2], start_of(oi[1]) + oi[3]])
    n_tiles = 2 * t // tm_e + N_EXPERTS
    n_valid = tile_end[-1:]
    tile_id = jnp.arange(n_tiles, dtype=jnp.int32)
    blk = jnp.minimum(tile_id, n_valid - 1)
    tile_expert = jnp.minimum(jnp.sum((blk[:, None] >= tile_end[None, :]).astype(jnp.int32), axis=1), N_EXPERTS - 1)
    zero_tiles = (jnp.any(tile_id[:, None] == (tile_end - 1)[None, :], axis=1) | (tile_id >= n_valid)).astype(jnp.int32)

    xs = _dispatch(x, pos, zero_tiles, n_tiles * tm_e, tiles["dispatch"], tm_e)
    ys = _experts(xs, w_gate, w_up, w_down, layer * N_EXPERTS, blk, tile_expert, n_valid, tm_e)
    return _combine_ln(ys, pos, x, of[:2].T, g, b, alpha, tiles["combine"])


def _hybrid_mixer_ln(xf, xb, bn, j, w_in_all, conv_w, conv_b, w_r, b_r, w_i, b_i, lam, b_f, w_out_all, g, b, alpha, tiles):
    t, d = xf.shape
    s = t // bn
    heads, block, _ = w_r.shape
    lru_w = heads * block
    fox_heads = b_f.shape[0]
    main = w_in_all.shape[1] - fox_heads
    fox_w = (main - 2 * lru_w) // 3
    dh = fox_w // fox_heads
    tm, tn = tiles["mm_m"], tiles["mm_n"]
    xc = _proj_w32(xb, w_in_all, j, 0, lru_w, BF16, tm, tn, "hy_lru_x_proj_conv", "conv", (conv_w, conv_b), s)
    gg = _proj_w32(xb, w_in_all, j, lru_w, lru_w, BF16, tm, tn, "hy_lru_gate_proj_gelu", "gelu")
    qkv = _proj_w32(xb, w_in_all, j, 2 * lru_w, 3 * fox_w, BF16, tm, tn, "hy_qkv_proj", "scale",
                     scale_cols=fox_w // min(tn, fox_w), scale=dh ** -0.5 * LOG2E)
    fl = _proj_small(xb, w_in_all[j, main:, :], tiles["small_m"], "hy_forget_proj")[:, :fox_heads]
    f2 = _fox_prep(fl.reshape(bn, s, fox_heads).transpose(0, 2, 1), b_f)
    y_lru = _lru(xc.reshape(bn, s, lru_w), gg.reshape(bn, s, lru_w), w_r, b_r, w_i, b_i, lam, tiles["lru"])
    y_att = _fox_attention(qkv.reshape(bn, s, 3 * fox_w), f2, fox_heads, dh, tiles["attn"], tiles["attn_heads"])
    return _matmul_residual_ln([y_lru.reshape(t, lru_w), y_att.reshape(t, fox_w)], w_out_all, j, xf, g, b, alpha,
                               tiles["ln_m"], tiles["ln_k"], "hy_out_proj_ln")


def _ssd_mixer_ln(xf, xb, bn, j, w_in_all, conv_w, conv_b, dt_bias, a_log, d_skip, norm_g, w_out_all, g, b, alpha, tiles):
    t, d = xf.shape
    s = t // bn
    heads = a_log.shape[0]
    inner = norm_g.shape[0]
    conv_dim = conv_w.shape[1]
    groups = (conv_dim - inner) // (2 * SSD_STATE)
    tm, tn = tiles["mm_m"], tiles["mm_n"]
    zs = _proj_w32(xb, w_in_all, j, 0, inner, BF16, tm, tn, "ssd_z_proj_silu", "silu")
    xbc = _proj_w32(xb, w_in_all, j, inner, conv_dim, BF16, tm, tn, "ssd_xbc_proj_conv_silu", "conv_silu",
                     (conv_w, conv_b), s)
    dt = _proj_small(xb, w_in_all[j, inner + conv_dim:, :], tiles["small_m"], "ssd_dt_proj")
    y = _ssd_core(xbc.reshape(bn, s, conv_dim), zs.reshape(bn, s, inner), dt.reshape(bn, s, _LANES),
                  dt_bias, a_log, d_skip, norm_g, groups)
    return _matmul_residual_ln([y.reshape(t, inner)], w_out_all, j, xf, g, b, alpha,
                               tiles["ln_m"], tiles["ln_k"], "ssd_out_proj_ln")


_TILES = dict(mm_m=1024, mm_n=1024, small_m=1024, ln_m=512, ln_k=4096, lru=256, attn=512, attn_heads=1,
              router=512, dispatch=1024, expert_rows=256, combine=256)


def kernel(x, ln_g, ln_b, hy_w_in, hy_conv_w, hy_conv_b, hy_w_r, hy_b_r, hy_w_i, hy_b_i, hy_lambda, hy_b_f, hy_w_out, ssd_w_in, ssd_conv_w, ssd_conv_b, ssd_dt_bias, ssd_a_log, ssd_d, ssd_norm_g, ssd_w_out, moe_w_r1, moe_b_r1, moe_w_r2, moe_b_r2, moe_w_gate, moe_w_up, moe_w_down):
    bn, s, d = x.shape
    depth = ln_g.shape[0]
    alpha = (2 * depth) ** 0.25
    ff = moe_w_gate.shape[-1]
    hy_w_in_t = jnp.swapaxes(hy_w_in, 1, 2)
    ssd_w_in_t = jnp.swapaxes(ssd_w_in, 1, 2)
    hy_w_out_b = hy_w_out.astype(BF16)
    ssd_w_out_b = ssd_w_out.astype(BF16)
    w_gate_s = moe_w_gate.reshape(depth * N_EXPERTS, d, ff).astype(BF16)
    w_up_s = moe_w_up.reshape(depth * N_EXPERTS, d, ff).astype(BF16)
    w_down_s = moe_w_down.reshape(depth * N_EXPERTS, ff, d).astype(BF16)
    xf = x.reshape(bn * s, d)
    xb = xf.astype(BF16)
    for layer in range(depth):
        j = layer // 2
        if layer % 2 == 0:
            xf, xb = _hybrid_mixer_ln(xf, xb, bn, j, hy_w_in_t, hy_conv_w[j], hy_conv_b[j], hy_w_r[j], hy_b_r[j],
                                      hy_w_i[j], hy_b_i[j], hy_lambda[j], hy_b_f[j], hy_w_out_b,
                                      ln_g[layer, 0], ln_b[layer, 0], alpha, _TILES)
        else:
            xf, xb = _ssd_mixer_ln(xf, xb, bn, j, ssd_w_in_t, ssd_conv_w[j], ssd_conv_b[j], ssd_dt_bias[j],
                                   ssd_a_log[j], ssd_d[j], ssd_norm_g[j], ssd_w_out_b,
                                   ln_g[layer, 0], ln_b[layer, 0], alpha, _TILES)
        xf, xb = _moe_ln(xf, moe_w_r1[layer], moe_b_r1[layer], moe_w_r2[layer], moe_b_r2[layer],
                         w_gate_s, w_up_s, w_down_s, layer, ln_g[layer, 1], ln_b[layer, 1], alpha, _TILES)
    return xf.reshape(bn, s, d)
```
